```python
import jax
import jax.numpy as jnp
from jax import lax
import numpy as np

D_MODEL = 1024
BATCH = 8
SEQ = 8192
DEPTH = 4

GRID_W = 64
CTX_LEN = 256
N_MIXERS = 3
N_ATTN_LAYERS = (DEPTH + 2) // N_MIXERS
N_POOL_LAYERS = (DEPTH + 1) // N_MIXERS
N_RET_LAYERS = DEPTH // N_MIXERS

ATTN_HEADS = 16
ATTN_KV_HEADS = 4
ATTN_GROUP = ATTN_HEADS // ATTN_KV_HEADS
HEAD_DIM = D_MODEL // ATTN_HEADS
WINDOW = 128
ATTN_BLOCK = 128
ROPE_BASE = 10000.0
NEG_INF = -1e30

POOL_WINDOWS = (2, 4, 8, 16)
POOL_GROUPS = len(POOL_WINDOWS)
POOL_GROUP_DIM = D_MODEL // POOL_GROUPS

RET_HEADS = 4
RET_DK = D_MODEL // RET_HEADS
RET_DV = 2 * RET_DK
RET_CHUNK = 128
RET_BWD_OFFSET = 0.5

FFN_HIDDEN = ((8 * D_MODEL + 3 * 256 - 1) // (3 * 256)) * 256
NORM_EPS = 1e-6

kernel_name = 'hybrid_interleaved_dit_block'


def rmsnorm(x, gain=None):
    xf = x.astype(jnp.float32)
    y = xf * lax.rsqrt(jnp.mean(xf * xf, axis=-1, keepdims=True) + NORM_EPS)
    if gain is not None:
        y = y * gain.astype(jnp.float32)
    return y.astype(x.dtype)


def pre_norm(xs, gain, shift, scale):
    return rmsnorm(xs, gain) * (1 + scale) + shift


def apply_rotary(x, cos, sin):
    half = x.shape[-1] // 2
    shape = (1, x.shape[1]) + (1,) * (x.ndim - 3) + (half,)
    cos = cos.reshape(shape).astype(x.dtype)
    sin = sin.reshape(shape).astype(x.dtype)
    x1, x2 = x[..., :half], x[..., half:]
    return jnp.concatenate([x1 * cos - x2 * sin, x2 * cos + x1 * sin], axis=-1)


def axial_rotary_tables(n_tokens):
    rows = n_tokens // GRID_W
    row = jnp.repeat(jnp.arange(rows, dtype=jnp.float32), GRID_W)
    col = jnp.tile(jnp.arange(GRID_W, dtype=jnp.float32), rows)
    n_freq = HEAD_DIM // 4
    inv = ROPE_BASE ** (-jnp.arange(n_freq, dtype=jnp.float32) / n_freq)
    ang = jnp.concatenate([row[:, None] * inv, col[:, None] * inv], axis=-1)
    return jnp.cos(ang), jnp.sin(ang)


def retention_rotary_tables(n_tokens):
    inv = ROPE_BASE ** (-jnp.linspace(0.0, 1.0, RET_DK // 2, dtype=jnp.float32))
    ang = jnp.arange(n_tokens, dtype=jnp.float32)[:, None] * inv
    return jnp.cos(ang), jnp.sin(ang)


def softmax_with_sink(logits, sink):
    full = jnp.concatenate([logits, jnp.broadcast_to(sink, logits.shape[:-1] + (1,))], axis=-1)
    return jax.nn.softmax(full, axis=-1)[..., :-1]


def banded_window_attention(q, k, v, k_ctx, v_ctx, sink):
    B, T = q.shape[:2]
    n_blk = T // ATTN_BLOCK
    span = 3 * ATTN_BLOCK
    pad = ((0, 0), (ATTN_BLOCK, ATTN_BLOCK), (0, 0), (0, 0))
    kp = jnp.pad(k, pad)
    vp = jnp.pad(v, pad)
    offs = jnp.arange(span) - ATTN_BLOCK
    rel = offs[None, :] - jnp.arange(ATTN_BLOCK)[:, None]
    near = jnp.abs(rel) <= WINDOW
    q_blocks = jnp.moveaxis(q.reshape((B, n_blk, ATTN_BLOCK) + q.shape[2:]), 1, 0)

    def one_block(args):
        qb, i = args
        start = i * ATTN_BLOCK
        kb = lax.dynamic_slice_in_dim(kp, start, span, axis=1)
        vb = lax.dynamic_slice_in_dim(vp, start, span, axis=1)
        key_pos = start + offs
        valid = near & ((key_pos >= 0) & (key_pos < T))[None, :]
        s_loc = jnp.einsum('bqkgd,bskd->bkgqs', qb, kb).astype(jnp.float32)
        s_loc = jnp.where(valid, s_loc, NEG_INF)
        s_ctx = jnp.einsum('bqkgd,bskd->bkgqs', qb, k_ctx).astype(jnp.float32)
        p = softmax_with_sink(jnp.concatenate([s_loc, s_ctx], axis=-1), sink).astype(v.dtype)
        return (jnp.einsum('bkgqs,bskd->bqkgd', p[..., :span], vb)
                + jnp.einsum('bkgqs,bskd->bqkgd', p[..., span:], v_ctx))

    out = lax.map(one_block, (q_blocks, jnp.arange(n_blk)))
    return jnp.moveaxis(out, 0, 1).reshape(B, T, ATTN_HEADS * HEAD_DIM)


def attention_mixer(h_ctx, h_lat, w_qkv, w_o, q_gain, k_gain, sink, need_ctx_out):
    q_cols = ATTN_HEADS * HEAD_DIM
    scale = HEAD_DIM ** -0.5
    sink_logit = sink.astype(jnp.float32).reshape(ATTN_KV_HEADS, ATTN_GROUP, 1, 1)

    def heads_q(q):
        B, T, _ = q.shape
        return rmsnorm(q.reshape(B, T, ATTN_KV_HEADS, ATTN_GROUP, HEAD_DIM), q_gain) * scale

    def heads_kv(kv):
        B, T, _ = kv.shape
        k, v = jnp.split(kv, 2, axis=-1)
        return (rmsnorm(k.reshape(B, T, ATTN_KV_HEADS, HEAD_DIM), k_gain),
                v.reshape(B, T, ATTN_KV_HEADS, HEAD_DIM))

    B, T, _ = h_lat.shape
    cos, sin = axial_rotary_tables(T)
    qkv = h_lat @ w_qkv
    q_l = apply_rotary(heads_q(qkv[..., :q_cols]), cos, sin)
    k_l, v_l = heads_kv(qkv[..., q_cols:])
    k_l = apply_rotary(k_l, cos, sin)
    if need_ctx_out:
        qkv_c = h_ctx @ w_qkv
        q_c = heads_q(qkv_c[..., :q_cols])
        k_c, v_c = heads_kv(qkv_c[..., q_cols:])
    else:
        k_c, v_c = heads_kv(h_ctx @ w_qkv[:, q_cols:])
    y_l = banded_window_attention(q_l, k_l, v_l, k_c, v_c, sink_logit) @ w_o
    if not need_ctx_out:
        return None, y_l
    L = h_ctx.shape[1]
    s = jnp.einsum('bqkgd,bskd->bkgqs', q_c, k_c).astype(jnp.float32)
    p = softmax_with_sink(s, sink_logit).astype(v_c.dtype)
    y_c = jnp.einsum('bkgqs,bskd->bqkgd', p, v_c).reshape(B, L, q_cols) @ w_o
    return y_c, y_l


def window_means(h, window):
    T = h.shape[1]
    cs = jnp.pad(jnp.cumsum(h.astype(jnp.float32), axis=1), ((0, 0), (1, 0), (0, 0)))
    t = jnp.arange(T)
    lo = jnp.maximum(t - window // 2, 0)
    hi = jnp.minimum(t + window // 2, T)
    s = jnp.take(cs, hi, axis=1) - jnp.take(cs, lo, axis=1)
    return (s / (hi - lo).astype(jnp.float32)[None, :, None]).astype(h.dtype)


def pool_mixer(h, w_group, layer_scale):
    B, T, _ = h.shape
    groups = jnp.split(h, POOL_GROUPS, axis=-1)
    pooled = jnp.stack([window_means(g, w) - g for g, w in zip(groups, POOL_WINDOWS)], axis=2)
    y = jnp.einsum('btgc,gcd->btgd', pooled, w_group).reshape(B, T, D_MODEL)
    return y * layer_scale


def retention_chunked(q, k, v, log_g, state):
    B, T, H, _ = q.shape
    dv = v.shape[-1]
    n = T // RET_CHUNK

    def to_chunks(a):
        return jnp.moveaxis(a.reshape(B, n, RET_CHUNK, H, a.shape[-1]), 1, 0)

    pos = jnp.arange(RET_CHUNK, dtype=jnp.float32)
    diff = pos[:, None] - pos[None, :]
    intra = jnp.where(diff >= 0, jnp.exp(jnp.maximum(diff, 0.0)[None] * log_g[:, None, None]), 0.0)
    q_dec = jnp.exp((pos[:, None] + 1.0) * log_g[None, :])[None, :, :, None]
    k_dec = jnp.exp((RET_CHUNK - 1.0 - pos)[:, None] * log_g[None, :])[None, :, :, None]
    chunk_dec = jnp.exp(RET_CHUNK * log_g)[None, :, None, None]

    def step(s, blk):
        qc, kc, vc = blk
        scores = jnp.einsum('bihd,bjhd->bhij', qc, kc) * intra
        y = (jnp.einsum('bhij,bjhv->bihv', scores, vc)
             + jnp.einsum('bihd,bhdv->bihv', qc, s) * q_dec)
        s = s * chunk_dec + jnp.einsum('bjhd,bjhv->bhdv', kc * k_dec, vc)
        return s, y

    state, ys = lax.scan(step, state, (to_chunks(q), to_chunks(k), to_chunks(v)))
    return jnp.moveaxis(ys, 0, 1).reshape(B, T, H, dv), state


def retention_mixer(h_ctx, h_lat, w_in, w_o, need_ctx_out):
    hk = RET_HEADS * RET_DK
    hv = RET_HEADS * RET_DV
    qkv_cols = 2 * hk + hv
    heads = jnp.arange(RET_HEADS, dtype=jnp.float32)
    log_g_f = jnp.log1p(-jnp.exp2(-5.0 - heads))
    log_g_b = jnp.log1p(-jnp.exp2(-5.0 - RET_BWD_OFFSET - heads))

    def project_qkv(h, rotate):
        B, T, _ = h.shape
        qkv = h @ w_in[:, :qkv_cols]
        q = qkv[..., :hk].reshape(B, T, RET_HEADS, RET_DK)
        k = qkv[..., hk:2 * hk].reshape(B, T, RET_HEADS, RET_DK) * (RET_DK ** -0.5)
        v = qkv[..., 2 * hk:].reshape(B, T, RET_HEADS, RET_DV)
        if rotate:
            cos, sin = retention_rotary_tables(T)
            q = apply_rotary(q, cos, sin)
            k = apply_rotary(k, cos, sin)
        return q, k, v

    def combine(h, y_f, y_b):
        B, T, _ = h.shape
        g_f, g_b = jnp.split(h @ w_in[:, qkv_cols:], 2, axis=-1)
        y = (jax.nn.silu(g_f) * rmsnorm(y_f).reshape(B, T, hv).astype(h.dtype)
             + jax.nn.silu(g_b) * rmsnorm(y_b).reshape(B, T, hv).astype(h.dtype))
        return y @ w_o

    def flip(a):
        return jnp.flip(a, axis=1)

    B = h_lat.shape[0]
    zeros = jnp.zeros((B, RET_HEADS, RET_DK, RET_DV), jnp.float32)
    q_c, k_c, v_c = project_qkv(h_ctx, rotate=False)
    yc_f, s_f = retention_chunked(q_c, k_c, v_c, log_g_f, zeros)
    yc_b, s_b = retention_chunked(flip(q_c), flip(k_c), flip(v_c), log_g_b, zeros)
    q_l, k_l, v_l = project_qkv(h_lat, rotate=True)
    yl_f, _ = retention_chunked(q_l, k_l, v_l, log_g_f, s_f)
    yl_b, _ = retention_chunked(flip(q_l), flip(k_l), flip(v_l), log_g_b, s_b)
    y_l = combine(h_lat, yl_f, flip(yl_b))
    y_c = combine(h_ctx, yc_f, flip(yc_b)) if need_ctx_out else None
    return y_c, y_l


def swiglu(h, w_gate, w_up, w_down):
    return (jax.nn.silu(h @ w_gate) * (h @ w_up)) @ w_down


def setup_inputs(seed: int = 0) -> dict:
    key = jax.random.key(seed)
    ks = jax.random.split(key, 20)

    def nrm(k, shape, scale):
        return jax.random.normal(k, shape, jnp.float32) * scale

    qkv_cols = (ATTN_HEADS + 2 * ATTN_KV_HEADS) * HEAD_DIM
    ret_cols = 2 * RET_HEADS * RET_DK + 3 * RET_HEADS * RET_DV
    return {
        'x': nrm(ks[0], (BATCH, SEQ, D_MODEL), 1.0),
        'c': nrm(ks[1], (BATCH, D_MODEL), 1.0),
        'ctx': nrm(ks[2], (BATCH, CTX_LEN, D_MODEL), 1.0),
        'c_ctx': nrm(ks[3], (D_MODEL,), 1.0),
        'ada_w': nrm(ks[4], (DEPTH, D_MODEL, 6 * D_MODEL), 0.5 * D_MODEL ** -0.5),
        'ada_b': nrm(ks[5], (DEPTH, 6 * D_MODEL), 0.02),
        'norm_mix': 1.0 + nrm(ks[6], (DEPTH, D_MODEL), 0.1),
        'norm_ffn': 1.0 + nrm(ks[7], (DEPTH, D_MODEL), 0.1),
        'attn_w_qkv': nrm(ks[8], (N_ATTN_LAYERS, D_MODEL, qkv_cols), D_MODEL ** -0.5),
        'attn_w_o': nrm(ks[9], (N_ATTN_LAYERS, ATTN_HEADS * HEAD_DIM, D_MODEL), (ATTN_HEADS * HEAD_DIM) ** -0.5),
        'attn_q_norm': 1.0 + nrm(ks[10], (N_ATTN_LAYERS, HEAD_DIM), 0.1),
        'attn_k_norm': 1.0 + nrm(ks[11], (N_ATTN_LAYERS, HEAD_DIM), 0.1),
        'attn_sink': nrm(ks[12], (N_ATTN_LAYERS, ATTN_HEADS), 1.0),
        'pool_w': nrm(ks[13], (N_POOL_LAYERS, POOL_GROUPS, POOL_GROUP_DIM, POOL_GROUP_DIM), POOL_GROUP_DIM ** -0.5),
        'pool_scale': 1.0 + nrm(ks[14], (N_POOL_LAYERS, D_MODEL), 0.1),
        'ret_w_in': nrm(ks[15], (N_RET_LAYERS, D_MODEL, ret_cols), D_MODEL ** -0.5),
        'ret_w_o': nrm(ks[16], (N_RET_LAYERS, RET_HEADS * RET_DV, D_MODEL), (RET_HEADS * RET_DV) ** -0.5),
        'ffn_w_gate': nrm(ks[17], (DEPTH, D_MODEL, FFN_HIDDEN), D_MODEL ** -0.5),
        'ffn_w_up': nrm(ks[18], (DEPTH, D_MODEL, FFN_HIDDEN), D_MODEL ** -0.5),
        'ffn_w_down': nrm(ks[19], (DEPTH, FFN_HIDDEN, D_MODEL), FFN_HIDDEN ** -0.5),
    }


def reference(x, c, ctx, c_ctx, ada_w, ada_b, norm_mix, norm_ffn, attn_w_qkv, attn_w_o, attn_q_norm,
              attn_k_norm, attn_sink, pool_w, pool_scale, ret_w_in, ret_w_o, ffn_w_gate, ffn_w_up, ffn_w_down):
    x_lat, x_ctx = x, ctx
    cond_lat = jax.nn.silu(c)[:, None, :]
    cond_ctx = jax.nn.silu(c_ctx)[None, None, :]
    for i in range(DEPTH):
        kind, slot = i % N_MIXERS, i // N_MIXERS
        need_ctx_out = i < DEPTH - 1
        sh_m, sc_m, g_m, sh_f, sc_f, g_f = jnp.split(cond_lat @ ada_w[i] + ada_b[i], 6, axis=-1)
        h_l = pre_norm(x_lat, norm_mix[i], sh_m, sc_m)
        if need_ctx_out or kind != 1:
            csh_m, csc_m, cg_m, csh_f, csc_f, cg_f = jnp.split(cond_ctx @ ada_w[i] + ada_b[i], 6, axis=-1)
            h_c = pre_norm(x_ctx, norm_mix[i], csh_m, csc_m)
        if kind == 0:
            y_c, y_l = attention_mixer(h_c, h_l, attn_w_qkv[slot], attn_w_o[slot], attn_q_norm[slot],
                                       attn_k_norm[slot], attn_sink[slot], need_ctx_out)
        elif kind == 1:
            y_l = pool_mixer(h_l, pool_w[slot], pool_scale[slot])
            y_c = pool_mixer(h_c, pool_w[slot], pool_scale[slot]) if need_ctx_out else None
        else:
            y_c, y_l = retention_mixer(h_c, h_l, ret_w_in[slot], ret_w_o[slot], need_ctx_out)
        x_lat = x_lat + g_m * y_l
        x_lat = x_lat + g_f * swiglu(pre_norm(x_lat, norm_ffn[i], sh_f, sc_f),
                                     ffn_w_gate[i], ffn_w_up[i], ffn_w_down[i])
        if need_ctx_out:
            x_ctx = x_ctx + cg_m * y_c
            x_ctx = x_ctx + cg_f * swiglu(pre_norm(x_ctx, norm_ffn[i], csh_f, csc_f),
                                          ffn_w_gate[i], ffn_w_up[i], ffn_w_down[i])
    return x_lat
```

```python
import functools

import jax
import jax.numpy as jnp
import numpy as np
from jax import lax
from jax.experimental import pallas as pl
from jax.experimental.pallas import tpu as pltpu

F32 = jnp.float32
BF16 = jnp.bfloat16

DEPTH = 4
GRID_W = 64
N_MIXERS = 3
ATTN_HEADS = 16
ATTN_KV_HEADS = 4
ATTN_GROUP = ATTN_HEADS // ATTN_KV_HEADS
HEAD_DIM = 64
WINDOW = 128
ATTN_BLOCK = 128
ROPE_BASE = 10000.0
NEG_INF = -1e30
POOL_WINDOWS = (2, 4, 8, 16)
POOL_HALO = 8
RET_HEADS = 4
RET_DK = 256
RET_DV = 512
RET_CHUNK = 128
RET_BWD_OFFSET = 0.5
NORM_EPS = 1e-6

VMEM_LIMIT_BYTES = 56 * 1024 * 1024
LANES = 128


def _params(*sem):
    return pltpu.CompilerParams(dimension_semantics=sem, vmem_limit_bytes=VMEM_LIMIT_BYTES)


def _silu(x):
    return x * jax.nn.sigmoid(x)


def _prenorm(x, gain, shift, scale):
    y = x * lax.rsqrt(jnp.mean(x * x, axis=-1, keepdims=True) + NORM_EPS)
    return (y * gain) * (1.0 + scale) + shift


def _resident(shape):
    nd = len(shape)
    return pl.BlockSpec(shape, lambda *_: (0,) * nd)


def _ada_kernel(c_ref, w_ref, b_ref, o_ref):
    cond = _silu(c_ref[...])
    o_ref[...] = jnp.dot(cond.astype(BF16), w_ref[...].astype(BF16), preferred_element_type=F32) + b_ref[...]


def _ada_mods(cvec, ada_w, ada_b):
    depth, d, n = ada_w.shape
    rows = cvec.shape[0]
    tn = 1536
    return pl.pallas_call(
        _ada_kernel,
        grid=(depth, n // tn),
        in_specs=[
            pl.BlockSpec((rows, d), lambda l, j: (0, 0)),
            pl.BlockSpec((None, d, tn), lambda l, j: (l, 0, j)),
            pl.BlockSpec((None, 1, tn), lambda l, j: (l, 0, j)),
        ],
        out_specs=pl.BlockSpec((None, rows, tn), lambda l, j: (l, 0, j)),
        out_shape=jax.ShapeDtypeStruct((depth, rows, n), F32),
        compiler_params=_params("parallel", "parallel"),
        name="ada_mods",
    )(cvec, ada_w, ada_b.reshape(depth, 1, n))


def _row_vec_spec(d, row_fn):
    return pl.BlockSpec((None, 1, d), lambda i, *_: (row_fn(i), 0, 0))


def _qkv_kernel(x_ref, gain_ref, sh_ref, sc_ref, w_ref, qg_ref, kg_ref, cos_ref, sin_ref, bd_ref,
                q_ref, k_ref, v_ref, *, rotate):
    tm = x_ref.shape[0]
    h = _prenorm(x_ref[...], gain_ref[...], sh_ref[...], sc_ref[...]).astype(BF16)
    acc = jnp.dot(h, w_ref[...], preferred_element_type=F32)
    bd = bd_ref[...]
    lane = lax.broadcasted_iota(jnp.int32, (tm, LANES), 1)
    first_half = (lane & (HEAD_DIM // 2)) == 0
    if rotate:
        cos = cos_ref[...]
        sin = sin_ref[...]

    def head_norm(xg, g):
        x2 = xg * xg
        hi = x2.astype(BF16)
        lo = (x2 - hi.astype(F32)).astype(BF16)
        ssq = jnp.dot(hi, bd, preferred_element_type=F32) + jnp.dot(lo, bd, preferred_element_type=F32)
        return (xg * lax.rsqrt(ssq * (1.0 / HEAD_DIM) + NORM_EPS)) * g

    def rot(y):
        partner = jnp.where(first_half, pltpu.roll(y, LANES - HEAD_DIM // 2, 1), pltpu.roll(y, HEAD_DIM // 2, 1))
        return y * cos + partner * sin

    qg = qg_ref[...]
    kg = kg_ref[...]
    nq = ATTN_HEADS * HEAD_DIM
    nk = ATTN_KV_HEADS * HEAD_DIM
    for j in range(nq // LANES):
        y = head_norm(acc[:, j * LANES:(j + 1) * LANES], qg) * (HEAD_DIM ** -0.5)
        if rotate:
            y = rot(y)
        q_ref[:, j * LANES:(j + 1) * LANES] = y.astype(BF16)
    for j in range(nk // LANES):
        y = head_norm(acc[:, nq + j * LANES:nq + (j + 1) * LANES], kg)
        if rotate:
            y = rot(y)
        k_ref[:, j * LANES:(j + 1) * LANES] = y.astype(BF16)
    v_ref[...] = acc[:, nq + nk:].astype(BF16)


def _attn_qkv(x, gain, mods, row_fn, w, qg, kg, cos_t, sin_t, bd, *, rotate, tm, tiles_per_seq):
    m, d = x.shape
    n = w.shape[1]
    nq = ATTN_HEADS * HEAD_DIM
    nk = ATTN_KV_HEADS * HEAD_DIM
    sh, sc = mods
    tab_spec = pl.BlockSpec((tm, LANES), lambda i: (i % tiles_per_seq, 0))
    return pl.pallas_call(
        functools.partial(_qkv_kernel, rotate=rotate),
        grid=(m // tm,),
        in_specs=[
            pl.BlockSpec((tm, d), lambda i: (i, 0)),
            _resident((1, d)),
            _row_vec_spec(d, row_fn),
            _row_vec_spec(d, row_fn),
            _resident((d, n)),
            _resident((1, LANES)),
            _resident((1, LANES)),
            tab_spec,
            tab_spec,
            _resident((LANES, LANES)),
        ],
        out_specs=[
            pl.BlockSpec((tm, nq), lambda i: (i, 0)),
            pl.BlockSpec((tm, nk), lambda i: (i, 0)),
            pl.BlockSpec((tm, nk), lambda i: (i, 0)),
        ],
        out_shape=[
            jax.ShapeDtypeStruct((m, nq), BF16),
            jax.ShapeDtypeStruct((m, nk), BF16),
            jax.ShapeDtypeStruct((m, nk), BF16),
        ],
        compiler_params=_params("parallel"),
        name="attn_qkv_rot" if rotate else "attn_qkv",
    )(x, gain, sh, sc, w, qg, kg, cos_t, sin_t, bd)


def _attn_kernel(sink_ref, q_ref, *refs, local):
    if local:
        kp_ref, kc_ref, kn_ref, vp_ref, vc_ref, vn_ref, kx_ref, vx_ref, mask_ref, o_ref = refs
    else:
        kx_ref, vx_ref, o_ref = refs
    blk = q_ref.shape[0]
    for h in range(ATTN_KV_HEADS):
        cs = slice(h * HEAD_DIM, (h + 1) * HEAD_DIM)
        heads = [h * ATTN_GROUP + g for g in range(ATTN_GROUP)]
        qh = jnp.concatenate([q_ref[:, hd * HEAD_DIM:(hd + 1) * HEAD_DIM] for hd in heads], axis=0)
        if local:
            kh = jnp.concatenate([kp_ref[:, cs], kc_ref[:, cs], kn_ref[:, cs], kx_ref[:, cs]], axis=0)
            vh = jnp.concatenate([vp_ref[:, cs], vc_ref[:, cs], vn_ref[:, cs], vx_ref[:, cs]], axis=0)
        else:
            kh = kx_ref[:, cs]
            vh = vx_ref[:, cs]
        s = lax.dot_general(qh, kh, (((1,), (1,)), ((), ())), preferred_element_type=F32)
        if local:
            s = s + mask_ref[...]
        sink = jnp.concatenate([jnp.full((blk, 1), sink_ref[hd], F32) for hd in heads], axis=0)
        m = jnp.maximum(jnp.max(s, axis=1, keepdims=True), sink)
        p = jnp.exp(s - m)
        den = jnp.sum(p, axis=1, keepdims=True) + jnp.exp(sink - m)
        o = jnp.dot(p.astype(BF16), vh, preferred_element_type=F32) * (1.0 / den)
        for g, hd in enumerate(heads):
            o_ref[:, hd * HEAD_DIM:(hd + 1) * HEAD_DIM] = o[g * blk:(g + 1) * blk].astype(BF16)


def _attn_mask_table():
    span = 3 * ATTN_BLOCK
    offs = np.arange(span) - ATTN_BLOCK
    rel = offs[None, :] - np.arange(ATTN_BLOCK)[:, None]
    near = np.abs(rel) <= WINDOW
    tabs = []
    for variant in range(3):
        ok = near.copy()
        if variant == 0:
            ok[:, :ATTN_BLOCK] = False
        if variant == 2:
            ok[:, 2 * ATTN_BLOCK:] = False
        tabs.append(np.where(ok, 0.0, NEG_INF).astype(np.float32))
    return np.stack(tabs)


def _attention(sink, q, k, v, kx, vx, *, batch, seq, ctx_len, local):
    nblk = seq // ATTN_BLOCK
    nq = ATTN_HEADS * HEAD_DIM
    nk = ATTN_KV_HEADS * HEAD_DIM
    q_spec = pl.BlockSpec((ATTN_BLOCK, nq), lambda b, i: (b * nblk + i, 0))
    x_spec = pl.BlockSpec((ctx_len, nk), lambda b, i: (b, 0))
    smem = pl.BlockSpec(memory_space=pltpu.SMEM)
    if local:
        local_mask = _attn_mask_table()
        mask = np.concatenate([local_mask, np.zeros((3, ATTN_BLOCK, ctx_len), np.float32)], axis=2)
        mask = jnp.asarray(np.tile(mask, (1, ATTN_GROUP, 1)))
        prev = pl.BlockSpec((ATTN_BLOCK, nk), lambda b, i: (b * nblk + jnp.maximum(i - 1, 0), 0))
        cur = pl.BlockSpec((ATTN_BLOCK, nk), lambda b, i: (b * nblk + i, 0))
        nxt = pl.BlockSpec((ATTN_BLOCK, nk), lambda b, i: (b * nblk + jnp.minimum(i + 1, nblk - 1), 0))
        mask_spec = pl.BlockSpec((None,) + mask.shape[1:],
                                 lambda b, i: (jnp.where(i == 0, 0, jnp.where(i == nblk - 1, 2, 1)), 0, 0))
        in_specs = [smem, q_spec, prev, cur, nxt, prev, cur, nxt, x_spec, x_spec, mask_spec]
        args = (sink, q, k, k, k, v, v, v, kx, vx, mask)
    else:
        in_specs = [smem, q_spec, x_spec, x_spec]
        args = (sink, q, kx, vx)
    return pl.pallas_call(
        functools.partial(_attn_kernel, local=local),
        grid=(batch, nblk),
        in_specs=in_specs,
        out_specs=q_spec,
        out_shape=jax.ShapeDtypeStruct(q.shape, BF16),
        compiler_params=_params("parallel", "parallel"),
        name="attn_local" if local else "attn_ctx",
    )(*args)


def _resid_kernel(x_ref, a_ref, w_ref, g_ref, o_ref):
    y = jnp.dot(a_ref[...], w_ref[...], preferred_element_type=F32)
    o_ref[...] = x_ref[...] + g_ref[...] * y


def _resid_matmul(x, a, w, gate, row_fn, *, tm, name):
    m, d = x.shape
    kdim = a.shape[1]
    return pl.pallas_call(
        _resid_kernel,
        grid=(m // tm,),
        in_specs=[
            pl.BlockSpec((tm, d), lambda i: (i, 0)),
            pl.BlockSpec((tm, kdim), lambda i: (i, 0)),
            _resident((kdim, d)),
            _row_vec_spec(d, row_fn),
        ],
        out_specs=pl.BlockSpec((tm, d), lambda i: (i, 0)),
        out_shape=jax.ShapeDtypeStruct((m, d), F32),
        compiler_params=_params("parallel"),
        name=name,
    )(x, a, w, gate)


def _pool_kernel(xp_ref, x_ref, xn_ref, gain_ref, sh_ref, sc_ref, g_ref, w_ref, ls_ref, o_ref, h_ref,
                 *, tiles_per_seq, seq):
    tm, d = x_ref.shape
    it = pl.program_id(0) % tiles_per_seq
    gain, sh, sc = gain_ref[...], sh_ref[...], sc_ref[...]
    hp = jnp.where(it > 0, _prenorm(xp_ref[...], gain, sh, sc), 0.0)
    hn = jnp.where(it < tiles_per_seq - 1, _prenorm(xn_ref[...], gain, sh, sc), 0.0)
    x = x_ref[...]
    h_ref[0:POOL_HALO, :] = hp
    h_ref[POOL_HALO:POOL_HALO + tm, :] = _prenorm(x, gain, sh, sc)
    h_ref[POOL_HALO + tm:, :] = hn
    pos = it * tm + lax.broadcasted_iota(jnp.int32, (tm, 1), 0)
    gd = d // len(POOL_WINDOWS)
    for gi, win in enumerate(POOL_WINDOWS):
        cols = slice(gi * gd, (gi + 1) * gd)
        half = win // 2
        tot = h_ref[POOL_HALO - half:POOL_HALO - half + tm, cols]
        for off in range(-half + 1, half):
            tot = tot + h_ref[POOL_HALO + off:POOL_HALO + off + tm, cols]
        cnt = jnp.minimum(pos + half, seq) - jnp.maximum(pos - half, 0)
        pooled = tot * (1.0 / cnt.astype(F32)) - h_ref[POOL_HALO:POOL_HALO + tm, cols]
        y = jnp.dot(pooled.astype(BF16), w_ref[gi], preferred_element_type=F32)
        o_ref[:, cols] = x[:, cols] + g_ref[:, cols] * (y * ls_ref[:, cols])


def _pool_mixer(x, gain, mods, row_fn, w, layer_scale, *, tm, seq):
    m, d = x.shape
    sh, sc, g = mods
    tiles_per_seq = seq // tm
    hb = tm // POOL_HALO
    nhalo = m // POOL_HALO
    return pl.pallas_call(
        functools.partial(_pool_kernel, tiles_per_seq=tiles_per_seq, seq=seq),
        grid=(m // tm,),
        in_specs=[
            pl.BlockSpec((POOL_HALO, d), lambda i: (jnp.maximum(i * hb - 1, 0), 0)),
            pl.BlockSpec((tm, d), lambda i: (i, 0)),
            pl.BlockSpec((POOL_HALO, d), lambda i: (jnp.minimum((i + 1) * hb, nhalo - 1), 0)),
            _resident((1, d)),
            _row_vec_spec(d, row_fn),
            _row_vec_spec(d, row_fn),
            _row_vec_spec(d, row_fn),
            _resident(w.shape),
            _resident((1, d)),
        ],
        out_specs=pl.BlockSpec((tm, d), lambda i: (i, 0)),
        out_shape=jax.ShapeDtypeStruct((m, d), F32),
        scratch_shapes=[pltpu.VMEM((tm + 2 * POOL_HALO, d), F32)],
        compiler_params=_params("parallel"),
        name="pool_mixer",
    )(x, x, x, gain, sh, sc, g, w, layer_scale)


RET_TN = 512
RET_QK_COLS = 2 * RET_HEADS * RET_DK
RET_V_COLS = RET_HEADS * RET_DV


def _retproj_kernel(x_ref, gain_ref, sh_ref, sc_ref, w_ref, cos_ref, sin_ref, o_ref, h_ref, *, rotate):
    j = pl.program_id(1)

    @pl.when(j == 0)
    def _():
        h_ref[...] = _prenorm(x_ref[...], gain_ref[...], sh_ref[...], sc_ref[...]).astype(BF16)

    acc = jnp.dot(h_ref[...], w_ref[...], preferred_element_type=F32)
    q_tiles = RET_HEADS * RET_DK // RET_TN
    half = RET_DK // 2

    def qk_tile(scale):
        for hd in range(RET_TN // RET_DK):
            x1 = acc[:, hd * RET_DK:hd * RET_DK + half]
            x2 = acc[:, hd * RET_DK + half:(hd + 1) * RET_DK]
            if rotate:
                cos, sin = cos_ref[...], sin_ref[...]
                x1, x2 = x1 * cos - x2 * sin, x2 * cos + x1 * sin
            o_ref[:, hd * RET_DK:hd * RET_DK + half] = (x1 * scale).astype(BF16)
            o_ref[:, hd * RET_DK + half:(hd + 1) * RET_DK] = (x2 * scale).astype(BF16)

    @pl.when(j < q_tiles)
    def _():
        qk_tile(1.0)

    @pl.when(jnp.logical_and(j >= q_tiles, j < 2 * q_tiles))
    def _():
        qk_tile(RET_DK ** -0.5)

    @pl.when(j >= 2 * q_tiles)
    def _():
        o_ref[...] = acc.astype(BF16)


def _ret_proj(x, gain, mods, row_fn, w, cos_t, sin_t, *, rotate, tm, tiles_per_seq):
    m, d = x.shape
    n = w.shape[1]
    sh, sc = mods
    tab_spec = pl.BlockSpec((tm, RET_DK // 2), lambda i, j: (i % tiles_per_seq, 0))
    return pl.pallas_call(
        functools.partial(_retproj_kernel, rotate=rotate),
        grid=(m // tm, n // RET_TN),
        in_specs=[
            pl.BlockSpec((tm, d), lambda i, j: (i, 0)),
            _resident((1, d)),
            _row_vec_spec(d, row_fn),
            _row_vec_spec(d, row_fn),
            pl.BlockSpec((d, RET_TN), lambda i, j: (0, j)),
            tab_spec,
            tab_spec,
        ],
        out_specs=pl.BlockSpec((tm, RET_TN), lambda i, j: (i, j)),
        out_shape=jax.ShapeDtypeStruct((m, n), BF16),
        scratch_shapes=[pltpu.VMEM((tm, d), BF16)],
        compiler_params=_params("parallel", "arbitrary"),
        name="ret_proj_rot" if rotate else "ret_proj",
    )(x, gain, sh, sc, w, cos_t, sin_t)


def _ret_kernel(qf_ref, kf_ref, vf_ref, qb_ref, kb_ref, vb_ref, intra_ref, qdec_ref, kdec_ref, cdec_ref,
                s0_ref, yf_ref, yb_ref, sout_ref, state_ref):
    c = pl.program_id(2)
    hd = pl.program_id(1)

    @pl.when(c == 0)
    def _():
        state_ref[...] = s0_ref[...]

    def step(d, q_ref, k_ref, v_ref, y_ref):
        q, k, v = q_ref[...], k_ref[...], v_ref[...]
        s = state_ref[d]
        scores = lax.dot_general(q, k, (((1,), (1,)), ((), ())), preferred_element_type=F32) * intra_ref[d]
        y = (jnp.dot(scores.astype(BF16), v, preferred_element_type=F32)
             + jnp.dot(q, s.astype(BF16), preferred_element_type=F32) * qdec_ref[d])
        kd = (k.astype(F32) * kdec_ref[d]).astype(BF16)
        state_ref[d] = s * cdec_ref[d * RET_HEADS + hd] + lax.dot_general(
            kd, v, (((0,), (0,)), ((), ())), preferred_element_type=F32)
        y = y * lax.rsqrt(jnp.mean(y * y, axis=-1, keepdims=True) + NORM_EPS)
        y_ref[...] = y.astype(BF16)

    step(0, qf_ref, kf_ref, vf_ref, yf_ref)
    step(1, qb_ref, kb_ref, vb_ref, yb_ref)

    @pl.when(c == pl.num_programs(2) - 1)
    def _():
        sout_ref[...] = state_ref[...]


def _ret_tables():
    heads = np.arange(RET_HEADS, dtype=np.float64)
    log_g = [np.log1p(-np.exp2(-5.0 - heads)), np.log1p(-np.exp2(-5.0 - RET_BWD_OFFSET - heads))]
    pos = np.arange(RET_CHUNK, dtype=np.float64)
    diff = pos[:, None] - pos[None, :]
    intra, qdec, kdec, cdec = [], [], [], []
    for d, lg in enumerate(log_g):
        lg3 = lg[:, None, None]
        if d == 0:
            intra.append(np.where(diff >= 0, np.exp(np.maximum(diff, 0.0)[None] * lg3), 0.0))
            qdec.append(np.exp((pos[None, :] + 1.0) * lg[:, None]))
            kdec.append(np.exp((RET_CHUNK - 1.0 - pos)[None, :] * lg[:, None]))
        else:
            intra.append(np.where(diff <= 0, np.exp(np.maximum(-diff, 0.0)[None] * lg3), 0.0))
            qdec.append(np.exp((RET_CHUNK - pos)[None, :] * lg[:, None]))
            kdec.append(np.exp(pos[None, :] * lg[:, None]))
        cdec.append(np.exp(RET_CHUNK * lg))
    intra = np.stack(intra).astype(np.float32)
    qdec = np.stack(qdec).astype(np.float32)[..., None]
    kdec = np.stack(kdec).astype(np.float32)[..., None]
    cdec = np.stack(cdec).astype(np.float32).reshape(-1)
    return jnp.asarray(intra), jnp.asarray(qdec), jnp.asarray(kdec), jnp.asarray(cdec)


def _retention(proj, s0, tables, *, batch, seq):
    intra, qdec, kdec, cdec = tables
    n = seq // RET_CHUNK
    kq = RET_HEADS
    kv = RET_QK_COLS // RET_DV

    def fwd(off, width):
        return pl.BlockSpec((RET_CHUNK, width), lambda b, h, c: (b * n + c, off + h))

    def bwd(off, width):
        return pl.BlockSpec((RET_CHUNK, width), lambda b, h, c: (b * n + n - 1 - c, off + h))

    tab = lambda a: pl.BlockSpec((2, None) + a.shape[2:], lambda b, h, c: (0, h) + (0,) * (a.ndim - 2))
    state_spec = pl.BlockSpec((None, None, 2, RET_DK, RET_DV), lambda b, h, c: (b, h, 0, 0, 0))
    y_shape = jax.ShapeDtypeStruct((batch * seq, RET_V_COLS), BF16)
    return pl.pallas_call(
        _ret_kernel,
        grid=(batch, RET_HEADS, n),
        in_specs=[
            fwd(0, RET_DK), fwd(kq, RET_DK), fwd(kv, RET_DV),
            bwd(0, RET_DK), bwd(kq, RET_DK), bwd(kv, RET_DV),
            tab(intra), tab(qdec), tab(kdec),
            pl.BlockSpec(memory_space=pltpu.SMEM),
            state_spec,
        ],
        out_specs=[
            pl.BlockSpec((RET_CHUNK, RET_DV), lambda b, h, c: (b * n + c, h)),
            pl.BlockSpec((RET_CHUNK, RET_DV), lambda b, h, c: (b * n + n - 1 - c, h)),
            state_spec,
        ],
        out_shape=[y_shape, y_shape, jax.ShapeDtypeStruct(s0.shape, F32)],
        scratch_shapes=[pltpu.VMEM((2, RET_DK, RET_DV), F32)],
        compiler_params=_params("parallel", "parallel", "arbitrary"),
        name="retention",
    )(proj, proj, proj, proj, proj, proj, intra, qdec, kdec, cdec, s0)


def _retout_kernel(x_ref, yf_ref, yb_ref, gf_ref, gb_ref, w_ref, g_ref, o_ref):
    y = (_silu(gf_ref[...].astype(F32)) * yf_ref[...].astype(F32)
         + _silu(gb_ref[...].astype(F32)) * yb_ref[...].astype(F32))
    out = jnp.dot(y.astype(BF16), w_ref[...], preferred_element_type=F32)
    o_ref[...] = x_ref[...] + g_ref[...] * out


def _ret_out(x, yf, yb, proj, w, gate, row_fn, *, tm):
    m, d = x.shape
    hv = RET_V_COLS
    gate_off = (RET_QK_COLS + RET_V_COLS) // hv
    return pl.pallas_call(
        _retout_kernel,
        grid=(m // tm,),
        in_specs=[
            pl.BlockSpec((tm, d), lambda i: (i, 0)),
            pl.BlockSpec((tm, hv), lambda i: (i, 0)),
            pl.BlockSpec((tm, hv), lambda i: (i, 0)),
            pl.BlockSpec((tm, hv), lambda i: (i, gate_off)),
            pl.BlockSpec((tm, hv), lambda i: (i, gate_off + 1)),
            _resident((hv, d)),
            _row_vec_spec(d, row_fn),
        ],
        out_specs=pl.BlockSpec((tm, d), lambda i: (i, 0)),
        out_shape=jax.ShapeDtypeStruct((m, d), F32),
        compiler_params=_params("parallel"),
        name="ret_out",
    )(x, yf, yb, proj, proj, w, gate)


FFN_CHUNK = 256


def _ffn_kernel(x_ref, gain_ref, sh_ref, sc_ref, g_ref, wg_ref, wu_ref, wd_ref, o_ref, h_ref, acc_ref):
    x = x_ref[...]
    h_ref[...] = _prenorm(x, gain_ref[...], sh_ref[...], sc_ref[...]).astype(BF16)
    acc_ref[...] = jnp.zeros_like(acc_ref)
    n_chunks = wg_ref.shape[1] // FFN_CHUNK

    def body(ci, carry):
        c0 = pl.multiple_of(ci * FFN_CHUNK, FFN_CHUNK)
        h = h_ref[...]
        gate = jnp.dot(h, wg_ref[:, pl.ds(c0, FFN_CHUNK)], preferred_element_type=F32)
        up = jnp.dot(h, wu_ref[:, pl.ds(c0, FFN_CHUNK)], preferred_element_type=F32)
        a = (_silu(gate) * up).astype(BF16)
        acc_ref[...] += jnp.dot(a, wd_ref[pl.ds(c0, FFN_CHUNK), :], preferred_element_type=F32)
        return carry

    lax.fori_loop(0, n_chunks, body, 0)
    o_ref[...] = x + g_ref[...] * acc_ref[...]


def _ffn(x, gain, mods, row_fn, wg, wu, wd, *, tm):
    m, d = x.shape
    sh, sc, g = mods
    single = pl.Buffered(1)
    return pl.pallas_call(
        _ffn_kernel,
        grid=(m // tm,),
        in_specs=[
            pl.BlockSpec((tm, d), lambda i: (i, 0)),
            _resident((1, d)),
            _row_vec_spec(d, row_fn),
            _row_vec_spec(d, row_fn),
            _row_vec_spec(d, row_fn),
            pl.BlockSpec(wg.shape, lambda i: (0, 0), pipeline_mode=single),
            pl.BlockSpec(wu.shape, lambda i: (0, 0), pipeline_mode=single),
            pl.BlockSpec(wd.shape, lambda i: (0, 0), pipeline_mode=single),
        ],
        out_specs=pl.BlockSpec((tm, d), lambda i: (i, 0)),
        out_shape=jax.ShapeDtypeStruct((m, d), F32),
        scratch_shapes=[pltpu.VMEM((tm, d), BF16), pltpu.VMEM((tm, d), F32)],
        compiler_params=_params("parallel"),
        name="ffn",
    )(x, gain, sh, sc, g, wg, wu, wd)


def _axial_tables(n_tokens):
    rows = n_tokens // GRID_W
    row = np.repeat(np.arange(rows, dtype=np.float32), GRID_W)
    col = np.tile(np.arange(GRID_W, dtype=np.float32), rows)
    n_freq = HEAD_DIM // 4
    inv = jnp.asarray(ROPE_BASE, F32) ** (-jnp.arange(n_freq, dtype=F32) / n_freq)
    ang = jnp.concatenate([jnp.asarray(row)[:, None] * inv, jnp.asarray(col)[:, None] * inv], axis=-1)
    cos, sin = jnp.cos(ang), jnp.sin(ang)
    reps = LANES // HEAD_DIM
    cos_t = jnp.tile(jnp.concatenate([cos, cos], axis=-1), (1, reps))
    sin_t = jnp.tile(jnp.concatenate([-sin, sin], axis=-1), (1, reps))
    return cos_t, sin_t


def _retention_rot_tables(n_tokens):
    inv = jnp.asarray(ROPE_BASE, F32) ** (-jnp.linspace(0.0, 1.0, RET_DK // 2, dtype=F32))
    ang = jnp.arange(n_tokens, dtype=F32)[:, None] * inv
    return jnp.cos(ang), jnp.sin(ang)


def kernel(x, c, ctx, c_ctx, ada_w, ada_b, norm_mix, norm_ffn, attn_w_qkv, attn_w_o, attn_q_norm, attn_k_norm,
           attn_sink, pool_w, pool_scale, ret_w_in, ret_w_o, ffn_w_gate, ffn_w_up, ffn_w_down):
    batch, seq, d = x.shape
    ctx_len = ctx.shape[1]
    mod_rows = 16
    assert batch < mod_rows and seq % 1024 == 0 and ctx_len % 256 == 0 and seq % GRID_W == 0

    cvec = jnp.zeros((mod_rows, d), F32).at[:batch].set(c).at[batch].set(c_ctx)
    mods = _ada_mods(cvec, ada_w, ada_b)
    mods = mods.reshape(DEPTH, mod_rows, 6, 1, d)

    x_lat = x.reshape(batch * seq, d)
    x_ctx = ctx.reshape(batch * ctx_len, d)

    tm_lat, tm_ctx = 512, 256
    lat_row = lambda i: i // (seq // tm_lat)
    ctx_row = lambda i: batch
    tm_ret = 1024
    lat_row_ret = lambda i: i // (seq // tm_ret)

    attn_cos, attn_sin = _axial_tables(seq)
    ret_cos, ret_sin = _retention_rot_tables(seq)
    bd = jnp.asarray(np.kron(np.eye(LANES // HEAD_DIM), np.ones((HEAD_DIM, HEAD_DIM))), BF16)
    ret_tables = _ret_tables()

    for i in range(DEPTH):
        kind, slot = i % N_MIXERS, i // N_MIXERS
        need_ctx_out = i < DEPTH - 1
        mod = [mods[i, :, k] for k in range(6)]
        sh_m, sc_m, g_m, sh_f, sc_f, g_f = mod
        gain_m = norm_mix[i].reshape(1, d)
        gain_f = norm_ffn[i].reshape(1, d)

        if kind == 0:
            w_qkv = attn_w_qkv[slot].astype(BF16)
            w_o = attn_w_o[slot].astype(BF16)
            qg = jnp.tile(attn_q_norm[slot], LANES // HEAD_DIM).reshape(1, LANES)
            kg = jnp.tile(attn_k_norm[slot], LANES // HEAD_DIM).reshape(1, LANES)
            sink = attn_sink[slot].astype(F32)
            q_c, k_c, v_c = _attn_qkv(x_ctx, gain_m, (sh_m, sc_m), ctx_row, w_qkv, qg, kg, attn_cos, attn_sin, bd,
                                      rotate=False, tm=tm_ctx, tiles_per_seq=1)
            q_l, k_l, v_l = _attn_qkv(x_lat, gain_m, (sh_m, sc_m), lat_row, w_qkv, qg, kg, attn_cos, attn_sin, bd,
                                      rotate=True, tm=tm_lat, tiles_per_seq=seq // tm_lat)
            a_l = _attention(sink, q_l, k_l, v_l, k_c, v_c, batch=batch, seq=seq, ctx_len=ctx_len, local=True)
            x_lat = _resid_matmul(x_lat, a_l, w_o, g_m, lat_row, tm=tm_lat, name="attn_out")
            if need_ctx_out:
                a_c = _attention(sink, q_c, None, None, k_c, v_c, batch=batch, seq=ctx_len, ctx_len=ctx_len,
                                 local=False)
                x_ctx = _resid_matmul(x_ctx, a_c, w_o, g_m, ctx_row, tm=tm_ctx, name="attn_out_ctx")
        elif kind == 1:
            w_p = pool_w[slot].astype(BF16)
            ls = pool_scale[slot].reshape(1, d)
            x_lat = _pool_mixer(x_lat, gain_m, (sh_m, sc_m, g_m), lat_row, w_p, ls, tm=tm_lat, seq=seq)
            if need_ctx_out:
                x_ctx = _pool_mixer(x_ctx, gain_m, (sh_m, sc_m, g_m), ctx_row, w_p, ls, tm=tm_ctx, seq=ctx_len)
        else:
            w_in = ret_w_in[slot].astype(BF16)
            w_o = ret_w_o[slot].astype(BF16)
            zeros = jnp.zeros((batch, RET_HEADS, 2, RET_DK, RET_DV), F32)
            p_c = _ret_proj(x_ctx, gain_m, (sh_m, sc_m), ctx_row, w_in, ret_cos, ret_sin,
                            rotate=False, tm=tm_ctx, tiles_per_seq=1)
            yc_f, yc_b, s_c = _retention(p_c, zeros, ret_tables, batch=batch, seq=ctx_len)
            p_l = _ret_proj(x_lat, gain_m, (sh_m, sc_m), lat_row_ret, w_in, ret_cos, ret_sin,
                            rotate=True, tm=tm_ret, tiles_per_seq=seq // tm_ret)
            yl_f, yl_b, _ = _retention(p_l, s_c, ret_tables, batch=batch, seq=seq)
            x_lat = _ret_out(x_lat, yl_f, yl_b, p_l, w_o, g_m, lat_row, tm=tm_lat)
            if need_ctx_out:
                x_ctx = _ret_out(x_ctx, yc_f, yc_b, p_c, w_o, g_m, ctx_row, tm=tm_ctx)

        wg = ffn_w_gate[i].astype(BF16)
        wu = ffn_w_up[i].astype(BF16)
        wd = ffn_w_down[i].astype(BF16)
        x_lat = _ffn(x_lat, gain_f, (sh_f, sc_f, g_f), lat_row, wg, wu, wd, tm=tm_lat)
        if need_ctx_out:
            x_ctx = _ffn(x_ctx, gain_f, (sh_f, sc_f, g_f), ctx_row, wg, wu, wd, tm=tm_ctx)

    return x_lat.reshape(batch, seq, d)
```

```python
import functools

import jax
import jax.numpy as jnp
import numpy as np
from jax import lax
from jax.experimental import pallas as pl
from jax.experimental.pallas import tpu as pltpu

F32 = jnp.float32
BF16 = jnp.bfloat16

DEPTH = 4
GRID_W = 64
N_MIXERS = 3
ATTN_HEADS = 16
ATTN_KV_HEADS = 4
ATTN_GROUP = ATTN_HEADS // ATTN_KV_HEADS
HEAD_DIM = 64
WINDOW = 128
ATTN_BLOCK = 128
ROPE_BASE = 10000.0
NEG_INF = -1e30
POOL_WINDOWS = (2, 4, 8, 16)
POOL_HALO = 8
RET_HEADS = 4
RET_DK = 256
RET_DV = 512
RET_CHUNK = 128
RET_BWD_OFFSET = 0.5
NORM_EPS = 1e-6

VMEM_LIMIT_BYTES = 56 * 1024 * 1024
LANES = 128


def _params(*sem):
    return pltpu.CompilerParams(dimension_semantics=sem, vmem_limit_bytes=VMEM_LIMIT_BYTES)


def _silu(x):
    return x * jax.nn.sigmoid(x)


def _prenorm(x, gain, shift, scale):
    y = x * lax.rsqrt(jnp.mean(x * x, axis=-1, keepdims=True) + NORM_EPS)
    return (y * gain) * (1.0 + scale) + shift


def _resident(shape):
    nd = len(shape)
    return pl.BlockSpec(shape, lambda *_: (0,) * nd)


def _ada_kernel(c_ref, w_ref, b_ref, o_ref):
    cond = _silu(c_ref[...])
    o_ref[...] = jnp.dot(cond.astype(BF16), w_ref[...].astype(BF16), preferred_element_type=F32) + b_ref[...]


def _ada_mods(cvec, ada_w, ada_b):
    depth, d, n = ada_w.shape
    rows = cvec.shape[0]
    tn = 1536
    return pl.pallas_call(
        _ada_kernel,
        grid=(depth, n // tn),
        in_specs=[
            pl.BlockSpec((rows, d), lambda l, j: (0, 0)),
            pl.BlockSpec((None, d, tn), lambda l, j: (l, 0, j)),
            pl.BlockSpec((None, 1, tn), lambda l, j: (l, 0, j)),
        ],
        out_specs=pl.BlockSpec((None, rows, tn), lambda l, j: (l, 0, j)),
        out_shape=jax.ShapeDtypeStruct((depth, rows, n), F32),
        compiler_params=_params("parallel", "parallel"),
        name="ada_mods",
    )(cvec, ada_w, ada_b.reshape(depth, 1, n))


def _row_vec_spec(d, row_fn):
    return pl.BlockSpec((None, 1, d), lambda i, *_: (row_fn(i), 0, 0))


def _qkv_kernel(x_ref, gain_ref, sh_ref, sc_ref, w_ref, qg_ref, kg_ref, cos_ref, sin_ref, bd_ref,
                q_ref, k_ref, v_ref, *, rotate):
    tm = x_ref.shape[0]
    h = _prenorm(x_ref[...], gain_ref[...], sh_ref[...], sc_ref[...]).astype(BF16)
    acc = jnp.dot(h, w_ref[...], preferred_element_type=F32)
    bd = bd_ref[...]
    lane = lax.broadcasted_iota(jnp.int32, (tm, LANES), 1)
    first_half = (lane & (HEAD_DIM // 2)) == 0
    if rotate:
        cos = cos_ref[...]
        sin = sin_ref[...]

    def head_norm(xg, g):
        x2 = xg * xg
        hi = x2.astype(BF16)
        lo = (x2 - hi.astype(F32)).astype(BF16)
        ssq = jnp.dot(hi, bd, preferred_element_type=F32) + jnp.dot(lo, bd, preferred_element_type=F32)
        return (xg * lax.rsqrt(ssq * (1.0 / HEAD_DIM) + NORM_EPS)) * g

    def rot(y):
        partner = jnp.where(first_half, pltpu.roll(y, LANES - HEAD_DIM // 2, 1), pltpu.roll(y, HEAD_DIM // 2, 1))
        return y * cos + partner * sin

    qg = qg_ref[...]
    kg = kg_ref[...]
    nq = ATTN_HEADS * HEAD_DIM
    nk = ATTN_KV_HEADS * HEAD_DIM
    for j in range(nq // LANES):
        y = head_norm(acc[:, j * LANES:(j + 1) * LANES], qg) * (HEAD_DIM ** -0.5)
        if rotate:
            y = rot(y)
        q_ref[:, j * LANES:(j + 1) * LANES] = y.astype(BF16)
    for j in range(nk // LANES):
        y = head_norm(acc[:, nq + j * LANES:nq + (j + 1) * LANES], kg)
        if rotate:
            y = rot(y)
        k_ref[:, j * LANES:(j + 1) * LANES] = y.astype(BF16)
    v_ref[...] = acc[:, nq + nk:].astype(BF16)


def _attn_qkv(x, gain, mods, row_fn, w, qg, kg, cos_t, sin_t, bd, *, rotate, tm, tiles_per_seq):
    m, d = x.shape
    n = w.shape[1]
    nq = ATTN_HEADS * HEAD_DIM
    nk = ATTN_KV_HEADS * HEAD_DIM
    sh, sc = mods
    tab_spec = pl.BlockSpec((tm, LANES), lambda i: (i % tiles_per_seq, 0))
    return pl.pallas_call(
        functools.partial(_qkv_kernel, rotate=rotate),
        grid=(m // tm,),
        in_specs=[
            pl.BlockSpec((tm, d), lambda i: (i, 0)),
            _resident((1, d)),
            _row_vec_spec(d, row_fn),
            _row_vec_spec(d, row_fn),
            _resident((d, n)),
            _resident((1, LANES)),
            _resident((1, LANES)),
            tab_spec,
            tab_spec,
            _resident((LANES, LANES)),
        ],
        out_specs=[
            pl.BlockSpec((tm, nq), lambda i: (i, 0)),
            pl.BlockSpec((tm, nk), lambda i: (i, 0)),
            pl.BlockSpec((tm, nk), lambda i: (i, 0)),
        ],
        out_shape=[
            jax.ShapeDtypeStruct((m, nq), BF16),
            jax.ShapeDtypeStruct((m, nk), BF16),
            jax.ShapeDtypeStruct((m, nk), BF16),
        ],
        compiler_params=_params("parallel"),
        name="attn_qkv_rot" if rotate else "attn_qkv",
    )(x, gain, sh, sc, w, qg, kg, cos_t, sin_t, bd)


def _attn_kernel(sink_ref, q_ref, *refs, local):
    if local:
        kp_ref, kc_ref, kn_ref, vp_ref, vc_ref, vn_ref, kx_ref, vx_ref, mask_ref, o_ref = refs
    else:
        kx_ref, vx_ref, o_ref = refs
    blk = q_ref.shape[0]
    for h in range(ATTN_KV_HEADS):
        cs = slice(h * HEAD_DIM, (h + 1) * HEAD_DIM)
        heads = [h * ATTN_GROUP + g for g in range(ATTN_GROUP)]
        qh = jnp.concatenate([q_ref[:, hd * HEAD_DIM:(hd + 1) * HEAD_DIM] for hd in heads], axis=0)
        if local:
            kh = jnp.concatenate([kp_ref[:, cs], kc_ref[:, cs], kn_ref[:, cs], kx_ref[:, cs]], axis=0)
            vh = jnp.concatenate([vp_ref[:, cs], vc_ref[:, cs], vn_ref[:, cs], vx_ref[:, cs]], axis=0)
        else:
            kh = kx_ref[:, cs]
            vh = vx_ref[:, cs]
        s = lax.dot_general(qh, kh, (((1,), (1,)), ((), ())), preferred_element_type=F32)
        if local:
            s = s + mask_ref[...]
        sink = jnp.concatenate([jnp.full((blk, 1), sink_ref[hd], F32) for hd in heads], axis=0)
        m = jnp.maximum(jnp.max(s, axis=1, keepdims=True), sink)
        p = jnp.exp(s - m)
        den = jnp.sum(p, axis=1, keepdims=True) + jnp.exp(sink - m)
        o = jnp.dot(p.astype(BF16), vh, preferred_element_type=F32) * (1.0 / den)
        for g, hd in enumerate(heads):
            o_ref[:, hd * HEAD_DIM:(hd + 1) * HEAD_DIM] = o[g * blk:(g + 1) * blk].astype(BF16)


def _attn_mask_table():
    span = 3 * ATTN_BLOCK
    offs = np.arange(span) - ATTN_BLOCK
    rel = offs[None, :] - np.arange(ATTN_BLOCK)[:, None]
    near = np.abs(rel) <= WINDOW
    tabs = []
    for variant in range(3):
        ok = near.copy()
        if variant == 0:
            ok[:, :ATTN_BLOCK] = False
        if variant == 2:
            ok[:, 2 * ATTN_BLOCK:] = False
        tabs.append(np.where(ok, 0.0, NEG_INF).astype(np.float32))
    return np.stack(tabs)


def _attention(sink, q, k, v, kx, vx, *, batch, seq, ctx_len, local):
    nblk = seq // ATTN_BLOCK
    nq = ATTN_HEADS * HEAD_DIM
    nk = ATTN_KV_HEADS * HEAD_DIM
    q_spec = pl.BlockSpec((ATTN_BLOCK, nq), lambda b, i: (b * nblk + i, 0))
    x_spec = pl.BlockSpec((ctx_len, nk), lambda b, i: (b, 0))
    smem = pl.BlockSpec(memory_space=pltpu.SMEM)
    if local:
        local_mask = _attn_mask_table()
        mask = np.concatenate([local_mask, np.zeros((3, ATTN_BLOCK, ctx_len), np.float32)], axis=2)
        mask = jnp.asarray(np.tile(mask, (1, ATTN_GROUP, 1)))
        prev = pl.BlockSpec((ATTN_BLOCK, nk), lambda b, i: (b * nblk + jnp.maximum(i - 1, 0), 0))
        cur = pl.BlockSpec((ATTN_BLOCK, nk), lambda b, i: (b * nblk + i, 0))
        nxt = pl.BlockSpec((ATTN_BLOCK, nk), lambda b, i: (b * nblk + jnp.minimum(i + 1, nblk - 1), 0))
        mask_spec = pl.BlockSpec((None,) + mask.shape[1:],
                                 lambda b, i: (jnp.where(i == 0, 0, jnp.where(i == nblk - 1, 2, 1)), 0, 0))
        in_specs = [smem, q_spec, prev, cur, nxt, prev, cur, nxt, x_spec, x_spec, mask_spec]
        args = (sink, q, k, k, k, v, v, v, kx, vx, mask)
    else:
        in_specs = [smem, q_spec, x_spec, x_spec]
        args = (sink, q, kx, vx)
    return pl.pallas_call(
        functools.partial(_attn_kernel, local=local),
        grid=(batch, nblk),
        in_specs=in_specs,
        out_specs=q_spec,
        out_shape=jax.ShapeDtypeStruct(q.shape, BF16),
        compiler_params=_params("parallel", "parallel"),
        name="attn_local" if local else "attn_ctx",
    )(*args)


def _resid_kernel(x_ref, a_ref, w_ref, g_ref, o_ref):
    y = jnp.dot(a_ref[...], w_ref[...], preferred_element_type=F32)
    o_ref[...] = x_ref[...] + g_ref[...] * y


def _resid_matmul(x, a, w, gate, row_fn, *, tm, name):
    m, d = x.shape
    kdim = a.shape[1]
    return pl.pallas_call(
        _resid_kernel,
        grid=(m // tm,),
        in_specs=[
            pl.BlockSpec((tm, d), lambda i: (i, 0)),
            pl.BlockSpec((tm, kdim), lambda i: (i, 0)),
            _resident((kdim, d)),
            _row_vec_spec(d, row_fn),
        ],
        out_specs=pl.BlockSpec((tm, d), lambda i: (i, 0)),
        out_shape=jax.ShapeDtypeStruct((m, d), F32),
        compiler_params=_params("parallel"),
        name=name,
    )(x, a, w, gate)


def _pool_kernel(xp_ref, x_ref, xn_ref, gain_ref, sh_ref, sc_ref, g_ref, w_ref, ls_ref, o_ref, h_ref,
                 *, tiles_per_seq, seq):
    tm, d = x_ref.shape
    it = pl.program_id(0) % tiles_per_seq
    gain, sh, sc = gain_ref[...], sh_ref[...], sc_ref[...]
    hp = jnp.where(it > 0, _prenorm(xp_ref[...], gain, sh, sc), 0.0)
    hn = jnp.where(it < tiles_per_seq - 1, _prenorm(xn_ref[...], gain, sh, sc), 0.0)
    x = x_ref[...]
    h_ref[0:POOL_HALO, :] = hp
    h_ref[POOL_HALO:POOL_HALO + tm, :] = _prenorm(x, gain, sh, sc)
    h_ref[POOL_HALO + tm:, :] = hn
    pos = it * tm + lax.broadcasted_iota(jnp.int32, (tm, 1), 0)
    gd = d // len(POOL_WINDOWS)
    for gi, win in enumerate(POOL_WINDOWS):
        cols = slice(gi * gd, (gi + 1) * gd)
        half = win // 2
        tot = h_ref[POOL_HALO - half:POOL_HALO - half + tm, cols]
        for off in range(-half + 1, half):
            tot = tot + h_ref[POOL_HALO + off:POOL_HALO + off + tm, cols]
        cnt = jnp.minimum(pos + half, seq) - jnp.maximum(pos - half, 0)
        pooled = tot * (1.0 / cnt.astype(F32)) - h_ref[POOL_HALO:POOL_HALO + tm, cols]
        y = jnp.dot(pooled.astype(BF16), w_ref[gi], preferred_element_type=F32)
        o_ref[:, cols] = x[:, cols] + g_ref[:, cols] * (y * ls_ref[:, cols])


def _pool_mixer(x, gain, mods, row_fn, w, layer_scale, *, tm, seq):
    m, d = x.shape
    sh, sc, g = mods
    tiles_per_seq = seq // tm
    hb = tm // POOL_HALO
    nhalo = m // POOL_HALO
    return pl.pallas_call(
        functools.partial(_pool_kernel, tiles_per_seq=tiles_per_seq, seq=seq),
        grid=(m // tm,),
        in_specs=[
            pl.BlockSpec((POOL_HALO, d), lambda i: (jnp.maximum(i * hb - 1, 0), 0)),
            pl.BlockSpec((tm, d), lambda i: (i, 0)),
            pl.BlockSpec((POOL_HALO, d), lambda i: (jnp.minimum((i + 1) * hb, nhalo - 1), 0)),
            _resident((1, d)),
            _row_vec_spec(d, row_fn),
            _row_vec_spec(d, row_fn),
            _row_vec_spec(d, row_fn),
            _resident(w.shape),
            _resident((1, d)),
        ],
        out_specs=pl.BlockSpec((tm, d), lambda i: (i, 0)),
        out_shape=jax.ShapeDtypeStruct((m, d), F32),
        scratch_shapes=[pltpu.VMEM((tm + 2 * POOL_HALO, d), F32)],
        compiler_params=_params("parallel"),
        name="pool_mixer",
    )(x, x, x, gain, sh, sc, g, w, layer_scale)


RET_TN = 512
RET_QK_COLS = 2 * RET_HEADS * RET_DK
RET_V_COLS = RET_HEADS * RET_DV
RET_QKV_COLS = RET_QK_COLS + RET_V_COLS


def _retproj_kernel(x_ref, gain_ref, sh_ref, sc_ref, w_ref, cos_ref, sin_ref, o_ref, *, rotate):
    h = _prenorm(x_ref[...], gain_ref[...], sh_ref[...], sc_ref[...]).astype(BF16)
    half = RET_DK // 2
    q_cols = RET_HEADS * RET_DK
    for c0 in range(0, RET_QKV_COLS, RET_TN):
        acc = jnp.dot(h, w_ref[:, c0:c0 + RET_TN], preferred_element_type=F32)
        if c0 >= RET_QK_COLS:
            o_ref[:, c0:c0 + RET_TN] = acc.astype(BF16)
            continue
        scale = 1.0 if c0 < q_cols else RET_DK ** -0.5
        for hd in range(RET_TN // RET_DK):
            x1 = acc[:, hd * RET_DK:hd * RET_DK + half]
            x2 = acc[:, hd * RET_DK + half:(hd + 1) * RET_DK]
            if rotate:
                cos, sin = cos_ref[...], sin_ref[...]
                x1, x2 = x1 * cos - x2 * sin, x2 * cos + x1 * sin
            o_ref[:, c0 + hd * RET_DK:c0 + hd * RET_DK + half] = (x1 * scale).astype(BF16)
            o_ref[:, c0 + hd * RET_DK + half:c0 + (hd + 1) * RET_DK] = (x2 * scale).astype(BF16)


def _ret_proj(x, gain, mods, row_fn, w, cos_t, sin_t, *, rotate, tm, tiles_per_seq):
    m, d = x.shape
    n = w.shape[1]
    sh, sc = mods
    tab_spec = pl.BlockSpec((tm, RET_DK // 2), lambda i: (i % tiles_per_seq, 0))
    return pl.pallas_call(
        functools.partial(_retproj_kernel, rotate=rotate),
        grid=(m // tm,),
        in_specs=[
            pl.BlockSpec((tm, d), lambda i: (i, 0)),
            _resident((1, d)),
            _row_vec_spec(d, row_fn),
            _row_vec_spec(d, row_fn),
            pl.BlockSpec((d, n), lambda i: (0, 0), pipeline_mode=pl.Buffered(1)),
            tab_spec,
            tab_spec,
        ],
        out_specs=pl.BlockSpec((tm, n), lambda i: (i, 0)),
        out_shape=jax.ShapeDtypeStruct((m, n), BF16),
        compiler_params=_params("parallel"),
        name="ret_proj_rot" if rotate else "ret_proj",
    )(x, gain, sh, sc, w, cos_t, sin_t)


def _ret_kernel(qf_ref, kf_ref, vf_ref, qb_ref, kb_ref, vb_ref, intra_ref, qdec_ref, kdec_ref, cdec_ref,
                s0_ref, yf_ref, yb_ref, sout_ref, state_ref):
    c = pl.program_id(2)
    hd = pl.program_id(1)

    @pl.when(c == 0)
    def _():
        state_ref[...] = s0_ref[...]

    def step(d, q_ref, k_ref, v_ref, y_ref):
        q, k, v = q_ref[...], k_ref[...], v_ref[...]
        s = state_ref[d]
        scores = lax.dot_general(q, k, (((1,), (1,)), ((), ())), preferred_element_type=F32) * intra_ref[d]
        y = (jnp.dot(scores.astype(BF16), v, preferred_element_type=F32)
             + jnp.dot(q, s.astype(BF16), preferred_element_type=F32) * qdec_ref[d])
        kd = (k.astype(F32) * kdec_ref[d]).astype(BF16)
        state_ref[d] = s * cdec_ref[d * RET_HEADS + hd] + lax.dot_general(
            kd, v, (((0,), (0,)), ((), ())), preferred_element_type=F32)
        y = y * lax.rsqrt(jnp.mean(y * y, axis=-1, keepdims=True) + NORM_EPS)
        y_ref[...] = y.astype(BF16)

    step(0, qf_ref, kf_ref, vf_ref, yf_ref)
    step(1, qb_ref, kb_ref, vb_ref, yb_ref)

    @pl.when(c == pl.num_programs(2) - 1)
    def _():
        sout_ref[...] = state_ref[...]


def _ret_tables():
    heads = np.arange(RET_HEADS, dtype=np.float64)
    log_g = [np.log1p(-np.exp2(-5.0 - heads)), np.log1p(-np.exp2(-5.0 - RET_BWD_OFFSET - heads))]
    pos = np.arange(RET_CHUNK, dtype=np.float64)
    diff = pos[:, None] - pos[None, :]
    intra, qdec, kdec, cdec = [], [], [], []
    for d, lg in enumerate(log_g):
        lg3 = lg[:, None, None]
        if d == 0:
            intra.append(np.where(diff >= 0, np.exp(np.maximum(diff, 0.0)[None] * lg3), 0.0))
            qdec.append(np.exp((pos[None, :] + 1.0) * lg[:, None]))
            kdec.append(np.exp((RET_CHUNK - 1.0 - pos)[None, :] * lg[:, None]))
        else:
            intra.append(np.where(diff <= 0, np.exp(np.maximum(-diff, 0.0)[None] * lg3), 0.0))
            qdec.append(np.exp((RET_CHUNK - pos)[None, :] * lg[:, None]))
            kdec.append(np.exp(pos[None, :] * lg[:, None]))
        cdec.append(np.exp(RET_CHUNK * lg))
    intra = np.stack(intra).astype(np.float32)
    qdec = np.stack(qdec).astype(np.float32)[..., None]
    kdec = np.stack(kdec).astype(np.float32)[..., None]
    cdec = np.stack(cdec).astype(np.float32).reshape(-1)
    return jnp.asarray(intra), jnp.asarray(qdec), jnp.asarray(kdec), jnp.asarray(cdec)


def _retention(proj, s0, tables, *, batch, seq):
    intra, qdec, kdec, cdec = tables
    n = seq // RET_CHUNK
    kq = RET_HEADS
    kv = RET_QK_COLS // RET_DV

    def fwd(off, width):
        return pl.BlockSpec((RET_CHUNK, width), lambda b, h, c: (b * n + c, off + h))

    def bwd(off, width):
        return pl.BlockSpec((RET_CHUNK, width), lambda b, h, c: (b * n + n - 1 - c, off + h))

    tab = lambda a: pl.BlockSpec((2, None) + a.shape[2:], lambda b, h, c: (0, h) + (0,) * (a.ndim - 2))
    state_spec = pl.BlockSpec((None, None, 2, RET_DK, RET_DV), lambda b, h, c: (b, h, 0, 0, 0))
    y_shape = jax.ShapeDtypeStruct((batch * seq, RET_V_COLS), BF16)
    return pl.pallas_call(
        _ret_kernel,
        grid=(batch, RET_HEADS, n),
        in_specs=[
            fwd(0, RET_DK), fwd(kq, RET_DK), fwd(kv, RET_DV),
            bwd(0, RET_DK), bwd(kq, RET_DK), bwd(kv, RET_DV),
            tab(intra), tab(qdec), tab(kdec),
            pl.BlockSpec(memory_space=pltpu.SMEM),
            state_spec,
        ],
        out_specs=[
            pl.BlockSpec((RET_CHUNK, RET_DV), lambda b, h, c: (b * n + c, h)),
            pl.BlockSpec((RET_CHUNK, RET_DV), lambda b, h, c: (b * n + n - 1 - c, h)),
            state_spec,
        ],
        out_shape=[y_shape, y_shape, jax.ShapeDtypeStruct(s0.shape, F32)],
        scratch_shapes=[pltpu.VMEM((2, RET_DK, RET_DV), F32)],
        compiler_params=_params("parallel", "parallel", "arbitrary"),
        name="retention",
    )(proj, proj, proj, proj, proj, proj, intra, qdec, kdec, cdec, s0)


def _retout_kernel(x_ref, yf_ref, yb_ref, gain_ref, sh_ref, sc_ref, g_ref, wg_ref, wo_ref, o_ref, a_ref):
    x = x_ref[...]
    h = _prenorm(x, gain_ref[...], sh_ref[...], sc_ref[...]).astype(BF16)
    hv = RET_V_COLS
    for c0 in range(0, hv, RET_DV):
        cols = slice(c0, c0 + RET_DV)
        gf = jnp.dot(h, wg_ref[:, c0:c0 + RET_DV], preferred_element_type=F32)
        gb = jnp.dot(h, wg_ref[:, hv + c0:hv + c0 + RET_DV], preferred_element_type=F32)
        y = _silu(gf) * yf_ref[:, cols].astype(F32) + _silu(gb) * yb_ref[:, cols].astype(F32)
        a_ref[:, cols] = y.astype(BF16)
    out = jnp.dot(a_ref[...], wo_ref[...], preferred_element_type=F32)
    o_ref[...] = x + g_ref[...] * out


def _ret_out(x, yf, yb, gain, mods, row_fn, w_gates, w_o, *, tm):
    m, d = x.shape
    hv = RET_V_COLS
    sh, sc, g = mods
    single = pl.Buffered(1)
    return pl.pallas_call(
        _retout_kernel,
        grid=(m // tm,),
        in_specs=[
            pl.BlockSpec((tm, d), lambda i: (i, 0)),
            pl.BlockSpec((tm, hv), lambda i: (i, 0)),
            pl.BlockSpec((tm, hv), lambda i: (i, 0)),
            _resident((1, d)),
            _row_vec_spec(d, row_fn),
            _row_vec_spec(d, row_fn),
            _row_vec_spec(d, row_fn),
            pl.BlockSpec(w_gates.shape, lambda i: (0, 0), pipeline_mode=single),
            pl.BlockSpec(w_o.shape, lambda i: (0, 0), pipeline_mode=single),
        ],
        out_specs=pl.BlockSpec((tm, d), lambda i: (i, 0)),
        out_shape=jax.ShapeDtypeStruct((m, d), F32),
        scratch_shapes=[pltpu.VMEM((tm, hv), BF16)],
        compiler_params=_params("parallel"),
        name="ret_out",
    )(x, yf, yb, gain, sh, sc, g, w_gates, w_o)


FFN_CHUNK = 256


def _ffn_kernel(x_ref, gain_ref, sh_ref, sc_ref, g_ref, wg_ref, wu_ref, wd_ref, o_ref, a_ref):
    x = x_ref[...]
    h = _prenorm(x, gain_ref[...], sh_ref[...], sc_ref[...]).astype(BF16)
    for c0 in range(0, wg_ref.shape[1], FFN_CHUNK):
        cols = slice(c0, c0 + FFN_CHUNK)
        gate = jnp.dot(h, wg_ref[:, cols], preferred_element_type=F32)
        up = jnp.dot(h, wu_ref[:, cols], preferred_element_type=F32)
        a_ref[:, cols] = (_silu(gate) * up).astype(BF16)
    out = jnp.dot(a_ref[...], wd_ref[...], preferred_element_type=F32)
    o_ref[...] = x + g_ref[...] * out


def _ffn(x, gain, mods, row_fn, wg, wu, wd, *, tm):
    m, d = x.shape
    sh, sc, g = mods
    single = pl.Buffered(1)
    return pl.pallas_call(
        _ffn_kernel,
        grid=(m // tm,),
        in_specs=[
            pl.BlockSpec((tm, d), lambda i: (i, 0)),
            _resident((1, d)),
            _row_vec_spec(d, row_fn),
            _row_vec_spec(d, row_fn),
            _row_vec_spec(d, row_fn),
            pl.BlockSpec(wg.shape, lambda i: (0, 0), pipeline_mode=single),
            pl.BlockSpec(wu.shape, lambda i: (0, 0), pipeline_mode=single),
            pl.BlockSpec(wd.shape, lambda i: (0, 0), pipeline_mode=single),
        ],
        out_specs=pl.BlockSpec((tm, d), lambda i: (i, 0)),
        out_shape=jax.ShapeDtypeStruct((m, d), F32),
        scratch_shapes=[pltpu.VMEM((tm, wg.shape[1]), BF16)],
        compiler_params=_params("parallel"),
        name="ffn",
    )(x, gain, sh, sc, g, wg, wu, wd)


def _axial_tables(n_tokens):
    rows = n_tokens // GRID_W
    row = np.repeat(np.arange(rows, dtype=np.float32), GRID_W)
    col = np.tile(np.arange(GRID_W, dtype=np.float32), rows)
    n_freq = HEAD_DIM // 4
    inv = jnp.asarray(ROPE_BASE, F32) ** (-jnp.arange(n_freq, dtype=F32) / n_freq)
    ang = jnp.concatenate([jnp.asarray(row)[:, None] * inv, jnp.asarray(col)[:, None] * inv], axis=-1)
    cos, sin = jnp.cos(ang), jnp.sin(ang)
    reps = LANES // HEAD_DIM
    cos_t = jnp.tile(jnp.concatenate([cos, cos], axis=-1), (1, reps))
    sin_t = jnp.tile(jnp.concatenate([-sin, sin], axis=-1), (1, reps))
    return cos_t, sin_t


def _retention_rot_tables(n_tokens):
    inv = jnp.asarray(ROPE_BASE, F32) ** (-jnp.linspace(0.0, 1.0, RET_DK // 2, dtype=F32))
    ang = jnp.arange(n_tokens, dtype=F32)[:, None] * inv
    return jnp.cos(ang), jnp.sin(ang)


def kernel(x, c, ctx, c_ctx, ada_w, ada_b, norm_mix, norm_ffn, attn_w_qkv, attn_w_o, attn_q_norm, attn_k_norm,
           attn_sink, pool_w, pool_scale, ret_w_in, ret_w_o, ffn_w_gate, ffn_w_up, ffn_w_down):
    batch, seq, d = x.shape
    ctx_len = ctx.shape[1]
    mod_rows = 16
    assert batch < mod_rows and seq % 1024 == 0 and ctx_len % 256 == 0 and seq % GRID_W == 0

    cvec = jnp.zeros((mod_rows, d), F32).at[:batch].set(c).at[batch].set(c_ctx)
    mods = _ada_mods(cvec, ada_w, ada_b)
    mods = mods.reshape(DEPTH, mod_rows, 6, 1, d)

    x_lat = x.reshape(batch * seq, d)
    x_ctx = ctx.reshape(batch * ctx_len, d)

    tm_lat, tm_ctx = 512, 256
    lat_row = lambda i: i // (seq // tm_lat)
    ctx_row = lambda i: batch

    attn_cos, attn_sin = _axial_tables(seq)
    ret_cos, ret_sin = _retention_rot_tables(seq)
    bd = jnp.asarray(np.kron(np.eye(LANES // HEAD_DIM), np.ones((HEAD_DIM, HEAD_DIM))), BF16)
    ret_tables = _ret_tables()

    for i in range(DEPTH):
        kind, slot = i % N_MIXERS, i // N_MIXERS
        need_ctx_out = i < DEPTH - 1
        mod = [mods[i, :, k] for k in range(6)]
        sh_m, sc_m, g_m, sh_f, sc_f, g_f = mod
        gain_m = norm_mix[i].reshape(1, d)
        gain_f = norm_ffn[i].reshape(1, d)

        if kind == 0:
            w_qkv = attn_w_qkv[slot].astype(BF16)
            w_o = attn_w_o[slot].astype(BF16)
            qg = jnp.tile(attn_q_norm[slot], LANES // HEAD_DIM).reshape(1, LANES)
            kg = jnp.tile(attn_k_norm[slot], LANES // HEAD_DIM).reshape(1, LANES)
            sink = attn_sink[slot].astype(F32)
            q_c, k_c, v_c = _attn_qkv(x_ctx, gain_m, (sh_m, sc_m), ctx_row, w_qkv, qg, kg, attn_cos, attn_sin, bd,
                                      rotate=False, tm=tm_ctx, tiles_per_seq=1)
            q_l, k_l, v_l = _attn_qkv(x_lat, gain_m, (sh_m, sc_m), lat_row, w_qkv, qg, kg, attn_cos, attn_sin, bd,
                                      rotate=True, tm=tm_lat, tiles_per_seq=seq // tm_lat)
            a_l = _attention(sink, q_l, k_l, v_l, k_c, v_c, batch=batch, seq=seq, ctx_len=ctx_len, local=True)
            x_lat = _resid_matmul(x_lat, a_l, w_o, g_m, lat_row, tm=tm_lat, name="attn_out")
            if need_ctx_out:
                a_c = _attention(sink, q_c, None, None, k_c, v_c, batch=batch, seq=ctx_len, ctx_len=ctx_len,
                                 local=False)
                x_ctx = _resid_matmul(x_ctx, a_c, w_o, g_m, ctx_row, tm=tm_ctx, name="attn_out_ctx")
        elif kind == 1:
            w_p = pool_w[slot].astype(BF16)
            ls = pool_scale[slot].reshape(1, d)
            x_lat = _pool_mixer(x_lat, gain_m, (sh_m, sc_m, g_m), lat_row, w_p, ls, tm=tm_lat, seq=seq)
            if need_ctx_out:
                x_ctx = _pool_mixer(x_ctx, gain_m, (sh_m, sc_m, g_m), ctx_row, w_p, ls, tm=tm_ctx, seq=ctx_len)
        else:
            w_qkv = ret_w_in[slot, :, :RET_QKV_COLS].astype(BF16)
            w_gates = ret_w_in[slot, :, RET_QKV_COLS:].astype(BF16)
            w_o = ret_w_o[slot].astype(BF16)
            zeros = jnp.zeros((batch, RET_HEADS, 2, RET_DK, RET_DV), F32)
            p_c = _ret_proj(x_ctx, gain_m, (sh_m, sc_m), ctx_row, w_qkv, ret_cos, ret_sin,
                            rotate=False, tm=tm_ctx, tiles_per_seq=1)
            yc_f, yc_b, s_c = _retention(p_c, zeros, ret_tables, batch=batch, seq=ctx_len)
            p_l = _ret_proj(x_lat, gain_m, (sh_m, sc_m), lat_row, w_qkv, ret_cos, ret_sin,
                            rotate=True, tm=tm_lat, tiles_per_seq=seq // tm_lat)
            yl_f, yl_b, _ = _retention(p_l, s_c, ret_tables, batch=batch, seq=seq)
            x_lat = _ret_out(x_lat, yl_f, yl_b, gain_m, (sh_m, sc_m, g_m), lat_row, w_gates, w_o, tm=tm_lat)
            if need_ctx_out:
                x_ctx = _ret_out(x_ctx, yc_f, yc_b, gain_m, (sh_m, sc_m, g_m), ctx_row, w_gates, w_o, tm=tm_ctx)

        wg = ffn_w_gate[i].astype(BF16)
        wu = ffn_w_up[i].astype(BF16)
        wd = ffn_w_down[i].astype(BF16)
        x_lat = _ffn(x_lat, gain_f, (sh_f, sc_f, g_f), lat_row, wg, wu, wd, tm=tm_lat)
        if need_ctx_out:
            x_ctx = _ffn(x_ctx, gain_f, (sh_f, sc_f, g_f), ctx_row, wg, wu, wd, tm=tm_ctx)

    return x_lat.reshape(batch, seq, d)
```

```python
import functools

import jax
import jax.numpy as jnp
import numpy as np
from jax import lax
from jax.experimental import pallas as pl
from jax.experimental.pallas import tpu as pltpu

F32 = jnp.float32
BF16 = jnp.bfloat16

DEPTH = 4
GRID_W = 64
N_MIXERS = 3
ATTN_HEADS = 16
ATTN_KV_HEADS = 4
ATTN_GROUP = ATTN_HEADS // ATTN_KV_HEADS
HEAD_DIM = 64
WINDOW = 128
ATTN_BLOCK = 128
ROPE_BASE = 10000.0
NEG_INF = -1e30
POOL_WINDOWS = (2, 4, 8, 16)
POOL_HALO = 8
RET_HEADS = 4
RET_DK = 256
RET_DV = 512
RET_SUB = 256
RET_BLOCK = 1024
RET_BWD_OFFSET = 0.5
NORM_EPS = 1e-6

VMEM_LIMIT_BYTES = 56 * 1024 * 1024
LANES = 128


def _params(*sem):
    return pltpu.CompilerParams(dimension_semantics=sem, vmem_limit_bytes=VMEM_LIMIT_BYTES)


def _silu(x):
    return x * jax.nn.sigmoid(x)


def _prenorm(x, gain, shift, scale):
    y = x * lax.rsqrt(jnp.mean(x * x, axis=-1, keepdims=True) + NORM_EPS)
    return (y * gain) * (1.0 + scale) + shift


def _resident(shape):
    nd = len(shape)
    return pl.BlockSpec(shape, lambda *_: (0,) * nd)


def _ada_kernel(c_ref, w_ref, b_ref, o_ref):
    cond = _silu(c_ref[...])
    o_ref[...] = jnp.dot(cond.astype(BF16), w_ref[...].astype(BF16), preferred_element_type=F32) + b_ref[...]


def _ada_mods(cvec, ada_w, ada_b):
    depth, d, n = ada_w.shape
    rows = cvec.shape[0]
    tn = 1536
    return pl.pallas_call(
        _ada_kernel,
        grid=(depth, n // tn),
        in_specs=[
            pl.BlockSpec((rows, d), lambda l, j: (0, 0)),
            pl.BlockSpec((None, d, tn), lambda l, j: (l, 0, j)),
            pl.BlockSpec((None, 1, tn), lambda l, j: (l, 0, j)),
        ],
        out_specs=pl.BlockSpec((None, rows, tn), lambda l, j: (l, 0, j)),
        out_shape=jax.ShapeDtypeStruct((depth, rows, n), F32),
        compiler_params=_params("parallel", "parallel"),
        name="ada_mods",
    )(cvec, ada_w, ada_b.reshape(depth, 1, n))


def _row_vec_spec(d, row_fn):
    return pl.BlockSpec((None, 1, d), lambda i, *_: (row_fn(i), 0, 0))


def _qkv_kernel(x_ref, gain_ref, sh_ref, sc_ref, w_ref, qg_ref, kg_ref, cos_ref, sin_ref, bd_ref,
                q_ref, k_ref, v_ref, *, rotate):
    h = _prenorm(x_ref[...], gain_ref[...], sh_ref[...], sc_ref[...]).astype(BF16)
    bd2 = bd_ref[...]
    if rotate:
        cos = cos_ref[...]
        sin = sin_ref[...]

    def head_norm(xg, g):
        x2 = xg * xg
        hi = x2.astype(BF16)
        lo = (x2 - hi.astype(F32)).astype(BF16)
        ssq = jnp.dot(jnp.concatenate([hi, lo], axis=1), bd2, preferred_element_type=F32)
        return (xg * lax.rsqrt(ssq * (1.0 / HEAD_DIM) + NORM_EPS)) * g

    def rot(y):
        return y * cos + pltpu.roll(y, LANES // 2, 1) * sin

    qg = qg_ref[...]
    kg = kg_ref[...]
    nq = ATTN_HEADS * HEAD_DIM
    nk = ATTN_KV_HEADS * HEAD_DIM
    width = 2 * LANES

    def project(c0):
        return jnp.dot(h, w_ref[:, c0:c0 + width], preferred_element_type=F32)

    nxt = project(0)
    for j in range((nq + nk) // LANES):
        if j % 2 == 0:
            wide = nxt
            nxt = project((j + 2) * LANES)
        acc = wide[:, (j % 2) * LANES:(j % 2 + 1) * LANES]
        is_q = j < nq // LANES
        y = head_norm(acc, qg) * (HEAD_DIM ** -0.5) if is_q else head_norm(acc, kg)
        if rotate:
            y = rot(y)
        if is_q:
            q_ref[:, j * LANES:(j + 1) * LANES] = y.astype(BF16)
        else:
            k_ref[:, j * LANES - nq:(j + 1) * LANES - nq] = y.astype(BF16)
    v_ref[...] = nxt.astype(BF16)


def _pair_layout():
    lane = np.arange(LANES)
    head_of_pair = (lane // (HEAD_DIM // 2)) % 2
    dim = lane % (HEAD_DIM // 2) + (HEAD_DIM // 2) * (lane // HEAD_DIM)
    return head_of_pair, dim


def _pair_cols(n_heads):
    head_of_pair, dim = _pair_layout()
    tiles = np.arange(n_heads // 2)
    return ((2 * tiles[:, None] + head_of_pair[None, :]) * HEAD_DIM + dim[None, :]).ravel()


def _attn_qkv(x, gain, mods, row_fn, w, qg, kg, cos_t, sin_t, bd, *, rotate, tm, tiles_per_seq):
    m, d = x.shape
    n = w.shape[1]
    nq = ATTN_HEADS * HEAD_DIM
    nk = ATTN_KV_HEADS * HEAD_DIM
    sh, sc = mods
    tab_spec = pl.BlockSpec((tm, LANES), lambda i: (i % tiles_per_seq, 0))
    return pl.pallas_call(
        functools.partial(_qkv_kernel, rotate=rotate),
        grid=(m // tm,),
        in_specs=[
            pl.BlockSpec((tm, d), lambda i: (i, 0)),
            _resident((1, d)),
            _row_vec_spec(d, row_fn),
            _row_vec_spec(d, row_fn),
            _resident((d, n)),
            _resident((1, LANES)),
            _resident((1, LANES)),
            tab_spec,
            tab_spec,
            _resident((2 * LANES, LANES)),
        ],
        out_specs=[
            pl.BlockSpec((tm, nq), lambda i: (i, 0)),
            pl.BlockSpec((tm, nk), lambda i: (i, 0)),
            pl.BlockSpec((tm, nk), lambda i: (i, 0)),
        ],
        out_shape=[
            jax.ShapeDtypeStruct((m, nq), BF16),
            jax.ShapeDtypeStruct((m, nk), BF16),
            jax.ShapeDtypeStruct((m, nk), BF16),
        ],
        compiler_params=_params("parallel"),
        name="attn_qkv_rot" if rotate else "attn_qkv",
    )(x, gain, sh, sc, w, qg, kg, cos_t, sin_t, bd)


ATTN_SHIFT_LIMIT = 30.0


def _attn_heads(bound_ref, sink_ref, q_ref, k_refs, v_refs, mask_ref, o_ref, *, use_bound):
    blk = q_ref.shape[0]
    lane = lax.broadcasted_iota(jnp.int32, (1, LANES), 1)
    low = lane < HEAD_DIM
    pair_bit = (lane // (HEAD_DIM // 2)) % 2
    one = jnp.ones((), BF16)
    zero = jnp.zeros((), BF16)

    def rows(refs, cols):
        return refs[0][:, cols] if len(refs) == 1 else jnp.concatenate([r[:, cols] for r in refs], axis=0)

    def scores(h):
        par = h % 2
        cols = slice((h // 2) * LANES, (h // 2 + 1) * LANES)
        own_qk = pair_bit == par
        k2 = rows(k_refs, cols)
        q_parts = []
        for g in range(ATTN_GROUP):
            hd = h * ATTN_GROUP + g
            qt = q_ref[:, (hd // 2) * LANES:(hd // 2 + 1) * LANES]
            if hd % 2 != par:
                qt = pltpu.roll(qt, HEAD_DIM // 2 if par == 1 else LANES - HEAD_DIM // 2, 1)
            q_parts.append(jnp.where(own_qk, qt, zero))
        q2 = jnp.concatenate(q_parts, axis=0)
        return lax.dot_general(q2, k2, (((1,), (1,)), ((), ())), preferred_element_type=F32)

    def finish(h, s):
        par = h % 2
        cols = slice((h // 2) * LANES, (h // 2 + 1) * LANES)
        own_v = low if par == 0 else jnp.logical_not(low)
        v2 = jnp.where(own_v, rows(v_refs, cols), one)
        heads = [h * ATTN_GROUP + g for g in range(ATTN_GROUP)]
        if mask_ref is not None:
            s = s + mask_ref[...]
        if use_bound:
            shifts = [jnp.maximum(bound_ref[0], sink_ref[hd]) for hd in heads]
            shift = jnp.concatenate([jnp.full((blk, 1), sh, F32) for sh in shifts], axis=0)
            es = [jnp.exp(jnp.full((1, LANES), sink_ref[hd] - sh, F32)) for hd, sh in zip(heads, shifts)]
        else:
            sink = jnp.concatenate([jnp.full((blk, 1), sink_ref[hd], F32) for hd in heads], axis=0)
            shift = jnp.maximum(jnp.max(s, axis=1, keepdims=True), sink)
            es_all = jnp.exp(sink - shift)
            es = [es_all[g * blk:(g + 1) * blk] for g in range(ATTN_GROUP)]
        p = jnp.exp(s - shift).astype(BF16)
        oa = jnp.dot(p, v2, preferred_element_type=F32)
        res = []
        for g, hd in enumerate(heads):
            a = oa[g * blk:(g + 1) * blk]
            r = pltpu.roll(a, HEAD_DIM, 1)
            res.append(a / (r + es[g]) if hd % 2 == par else r / (a + es[g]))
        for t in range(ATTN_GROUP // 2):
            tile = jnp.where(low, res[2 * t], res[2 * t + 1])
            c0 = (heads[2 * t] // 2) * LANES
            o_ref[:, c0:c0 + LANES] = tile.astype(BF16)

    s_next = scores(0)
    for h in range(ATTN_KV_HEADS):
        s = s_next
        if h + 1 < ATTN_KV_HEADS:
            s_next = scores(h + 1)
        finish(h, s)


def _attn_kernel(bound_ref, sink_ref, q_ref, *refs, local):
    if local:
        kp_ref, kc_ref, kn_ref, vp_ref, vc_ref, vn_ref, kx_ref, vx_ref, mask_ref, o_ref = refs
        k_refs = (kp_ref, kc_ref, kn_ref, kx_ref)
        v_refs = (vp_ref, vc_ref, vn_ref, vx_ref)
    else:
        kx_ref, vx_ref, o_ref = refs
        k_refs, v_refs, mask_ref = (kx_ref,), (vx_ref,), None
    use_bound = bound_ref[0] <= ATTN_SHIFT_LIMIT

    @pl.when(use_bound)
    def _():
        _attn_heads(bound_ref, sink_ref, q_ref, k_refs, v_refs, mask_ref, o_ref, use_bound=True)

    @pl.when(jnp.logical_not(use_bound))
    def _():
        _attn_heads(bound_ref, sink_ref, q_ref, k_refs, v_refs, mask_ref, o_ref, use_bound=False)


def _attn_mask_table():
    span = 3 * ATTN_BLOCK
    offs = np.arange(span) - ATTN_BLOCK
    rel = offs[None, :] - np.arange(ATTN_BLOCK)[:, None]
    near = np.abs(rel) <= WINDOW
    tabs = []
    for variant in range(3):
        ok = near.copy()
        if variant == 0:
            ok[:, :ATTN_BLOCK] = False
        if variant == 2:
            ok[:, 2 * ATTN_BLOCK:] = False
        tabs.append(np.where(ok, 0.0, NEG_INF).astype(np.float32))
    return np.stack(tabs)


def _attention(bound, sink, q, k, v, kx, vx, *, batch, seq, ctx_len, local):
    nblk = seq // ATTN_BLOCK
    nq = ATTN_HEADS * HEAD_DIM
    nk = ATTN_KV_HEADS * HEAD_DIM
    q_spec = pl.BlockSpec((ATTN_BLOCK, nq), lambda b, i: (b * nblk + i, 0))
    x_spec = pl.BlockSpec((ctx_len, nk), lambda b, i: (b, 0))
    smem = pl.BlockSpec(memory_space=pltpu.SMEM)
    if local:
        local_mask = _attn_mask_table()
        mask = np.concatenate([local_mask, np.zeros((3, ATTN_BLOCK, ctx_len), np.float32)], axis=2)
        mask = jnp.asarray(np.tile(mask, (1, ATTN_GROUP, 1)))
        prev = pl.BlockSpec((ATTN_BLOCK, nk), lambda b, i: (b * nblk + jnp.maximum(i - 1, 0), 0))
        cur = pl.BlockSpec((ATTN_BLOCK, nk), lambda b, i: (b * nblk + i, 0))
        nxt = pl.BlockSpec((ATTN_BLOCK, nk), lambda b, i: (b * nblk + jnp.minimum(i + 1, nblk - 1), 0))
        mask_spec = pl.BlockSpec((None,) + mask.shape[1:],
                                 lambda b, i: (jnp.where(i == 0, 0, jnp.where(i == nblk - 1, 2, 1)), 0, 0))
        in_specs = [smem, smem, q_spec, prev, cur, nxt, prev, cur, nxt, x_spec, x_spec, mask_spec]
        args = (bound, sink, q, k, k, k, v, v, v, kx, vx, mask)
    else:
        in_specs = [smem, smem, q_spec, x_spec, x_spec]
        args = (bound, sink, q, kx, vx)
    return pl.pallas_call(
        functools.partial(_attn_kernel, local=local),
        grid=(batch, nblk),
        in_specs=in_specs,
        out_specs=q_spec,
        out_shape=jax.ShapeDtypeStruct(q.shape, BF16),
        compiler_params=_params("parallel", "parallel"),
        name="attn_local" if local else "attn_ctx",
    )(*args)


def _pool_kernel(xp_ref, x_ref, xn_ref, gain_ref, sh_ref, sc_ref, g_ref, w_ref, ls_ref, o_ref, h_ref,
                 *, tiles_per_seq, seq):
    tm, d = x_ref.shape
    it = pl.program_id(0) % tiles_per_seq
    gain, sh, sc = gain_ref[...], sh_ref[...], sc_ref[...]
    hp = jnp.where(it > 0, _prenorm(xp_ref[...], gain, sh, sc), 0.0)
    hn = jnp.where(it < tiles_per_seq - 1, _prenorm(xn_ref[...], gain, sh, sc), 0.0)
    x = x_ref[...]
    h_ref[0:POOL_HALO, :] = hp
    h_ref[POOL_HALO:POOL_HALO + tm, :] = _prenorm(x, gain, sh, sc)
    h_ref[POOL_HALO + tm:, :] = hn
    pos = it * tm + lax.broadcasted_iota(jnp.int32, (tm, 1), 0)
    gd = d // len(POOL_WINDOWS)
    for gi, win in enumerate(POOL_WINDOWS):
        cols = slice(gi * gd, (gi + 1) * gd)
        half = win // 2
        tot = h_ref[POOL_HALO - half:POOL_HALO - half + tm, cols]
        for off in range(-half + 1, half):
            tot = tot + h_ref[POOL_HALO + off:POOL_HALO + off + tm, cols]
        cnt = jnp.minimum(pos + half, seq) - jnp.maximum(pos - half, 0)
        pooled = tot * (1.0 / cnt.astype(F32)) - h_ref[POOL_HALO:POOL_HALO + tm, cols]
        y = jnp.dot(pooled.astype(BF16), w_ref[gi], preferred_element_type=F32)
        o_ref[:, cols] = x[:, cols] + g_ref[:, cols] * (y * ls_ref[:, cols])


def _pool_mixer(x, gain, mods, row_fn, w, layer_scale, *, tm, seq):
    m, d = x.shape
    sh, sc, g = mods
    tiles_per_seq = seq // tm
    hb = tm // POOL_HALO
    nhalo = m // POOL_HALO
    return pl.pallas_call(
        functools.partial(_pool_kernel, tiles_per_seq=tiles_per_seq, seq=seq),
        grid=(m // tm,),
        in_specs=[
            pl.BlockSpec((POOL_HALO, d), lambda i: (jnp.maximum(i * hb - 1, 0), 0)),
            pl.BlockSpec((tm, d), lambda i: (i, 0)),
            pl.BlockSpec((POOL_HALO, d), lambda i: (jnp.minimum((i + 1) * hb, nhalo - 1), 0)),
            _resident((1, d)),
            _row_vec_spec(d, row_fn),
            _row_vec_spec(d, row_fn),
            _row_vec_spec(d, row_fn),
            _resident(w.shape),
            _resident((1, d)),
        ],
        out_specs=pl.BlockSpec((tm, d), lambda i: (i, 0)),
        out_shape=jax.ShapeDtypeStruct((m, d), F32),
        scratch_shapes=[pltpu.VMEM((tm + 2 * POOL_HALO, d), F32)],
        compiler_params=_params("parallel"),
        name="pool_mixer",
    )(x, x, x, gain, sh, sc, g, w, layer_scale)


RET_TN = 512
RET_QK_COLS = 2 * RET_HEADS * RET_DK
RET_V_COLS = RET_HEADS * RET_DV
RET_QKV_COLS = RET_QK_COLS + RET_V_COLS


def _retproj_kernel(x_ref, gain_ref, sh_ref, sc_ref, w_ref, cos_ref, sin_ref, o_ref, *, rotate):
    h = _prenorm(x_ref[...], gain_ref[...], sh_ref[...], sc_ref[...]).astype(BF16)
    half = RET_DK // 2
    q_cols = RET_HEADS * RET_DK
    for c0 in range(0, RET_QKV_COLS, RET_TN):
        acc = jnp.dot(h, w_ref[:, c0:c0 + RET_TN], preferred_element_type=F32)
        if c0 >= RET_QK_COLS:
            o_ref[:, c0:c0 + RET_TN] = acc.astype(BF16)
            continue
        scale = 1.0 if c0 < q_cols else RET_DK ** -0.5
        for hd in range(RET_TN // RET_DK):
            x1 = acc[:, hd * RET_DK:hd * RET_DK + half]
            x2 = acc[:, hd * RET_DK + half:(hd + 1) * RET_DK]
            if rotate:
                cos, sin = cos_ref[...], sin_ref[...]
                x1, x2 = x1 * cos - x2 * sin, x2 * cos + x1 * sin
            o_ref[:, c0 + hd * RET_DK:c0 + hd * RET_DK + half] = (x1 * scale).astype(BF16)
            o_ref[:, c0 + hd * RET_DK + half:c0 + (hd + 1) * RET_DK] = (x2 * scale).astype(BF16)


def _ret_proj(x, gain, mods, row_fn, w, cos_t, sin_t, *, rotate, tm, tiles_per_seq):
    m, d = x.shape
    n = w.shape[1]
    sh, sc = mods
    tab_spec = pl.BlockSpec((tm, RET_DK // 2), lambda i: (i % tiles_per_seq, 0))
    return pl.pallas_call(
        functools.partial(_retproj_kernel, rotate=rotate),
        grid=(m // tm,),
        in_specs=[
            pl.BlockSpec((tm, d), lambda i: (i, 0)),
            _resident((1, d)),
            _row_vec_spec(d, row_fn),
            _row_vec_spec(d, row_fn),
            pl.BlockSpec((d, n), lambda i: (0, 0), pipeline_mode=pl.Buffered(1)),
            tab_spec,
            tab_spec,
        ],
        out_specs=pl.BlockSpec((tm, n), lambda i: (i, 0)),
        out_shape=jax.ShapeDtypeStruct((m, n), BF16),
        compiler_params=_params("parallel"),
        name="ret_proj_rot" if rotate else "ret_proj",
    )(x, gain, sh, sc, w, cos_t, sin_t)


def _ret_kernel(qf_ref, kf_ref, vf_ref, qb_ref, kb_ref, vb_ref, intra_ref, qdec_ref, kdec_ref, cdec_ref,
                s0_ref, yf_ref, yb_ref, sout_ref, state_ref, *, sub):
    c = pl.program_id(2)
    hd = pl.program_id(1)

    @pl.when(c == 0)
    def _():
        state_ref[...] = s0_ref[...]

    n_sub = qf_ref.shape[0] // sub
    refs = ((qf_ref, kf_ref, vf_ref, yf_ref), (qb_ref, kb_ref, vb_ref, yb_ref))
    steps = []
    for j in range(n_sub):
        steps.append((0, slice(j * sub, (j + 1) * sub)))
        steps.append((1, slice((n_sub - 1 - j) * sub, (n_sub - j) * sub)))

    decayed = []
    for d, rows in steps:
        q_ref, k_ref, _, _ = refs[d]
        sc = lax.dot_general(q_ref[rows, :], k_ref[rows, :], (((1,), (1,)), ((), ())), preferred_element_type=F32)
        decayed.append((sc * intra_ref[d]).astype(BF16))

    for (d, rows), sc in zip(steps, decayed):
        q_ref, k_ref, v_ref, y_ref = refs[d]
        q, k, v = q_ref[rows, :], k_ref[rows, :], v_ref[rows, :]
        s = state_ref[d]
        y = (jnp.dot(sc, v, preferred_element_type=F32)
             + jnp.dot(q, s.astype(BF16), preferred_element_type=F32) * qdec_ref[d])
        kd = (k.astype(F32) * kdec_ref[d]).astype(BF16)
        state_ref[d] = s * cdec_ref[d * RET_HEADS + hd] + lax.dot_general(
            kd, v, (((0,), (0,)), ((), ())), preferred_element_type=F32)
        y = y * lax.rsqrt(jnp.mean(y * y, axis=-1, keepdims=True) + NORM_EPS)
        y_ref[rows, :] = y.astype(BF16)

    @pl.when(c == pl.num_programs(2) - 1)
    def _():
        sout_ref[...] = state_ref[...]


def _ret_tables(chunk):
    heads = np.arange(RET_HEADS, dtype=np.float64)
    log_g = [np.log1p(-np.exp2(-5.0 - heads)), np.log1p(-np.exp2(-5.0 - RET_BWD_OFFSET - heads))]
    pos = np.arange(chunk, dtype=np.float64)
    diff = pos[:, None] - pos[None, :]
    intra, qdec, kdec, cdec = [], [], [], []
    for d, lg in enumerate(log_g):
        lg3 = lg[:, None, None]
        if d == 0:
            intra.append(np.where(diff >= 0, np.exp(np.maximum(diff, 0.0)[None] * lg3), 0.0))
            qdec.append(np.exp((pos[None, :] + 1.0) * lg[:, None]))
            kdec.append(np.exp((chunk - 1.0 - pos)[None, :] * lg[:, None]))
        else:
            intra.append(np.where(diff <= 0, np.exp(np.maximum(-diff, 0.0)[None] * lg3), 0.0))
            qdec.append(np.exp((chunk - pos)[None, :] * lg[:, None]))
            kdec.append(np.exp(pos[None, :] * lg[:, None]))
        cdec.append(np.exp(chunk * lg))
    intra = np.stack(intra).astype(np.float32)
    qdec = np.stack(qdec).astype(np.float32)[..., None]
    kdec = np.stack(kdec).astype(np.float32)[..., None]
    cdec = np.stack(cdec).astype(np.float32).reshape(-1)
    return jnp.asarray(intra), jnp.asarray(qdec), jnp.asarray(kdec), jnp.asarray(cdec)


def _retention(proj, s0, *, batch, seq, block):
    sub = min(block, RET_SUB)
    intra, qdec, kdec, cdec = _ret_tables(sub)
    n = seq // block
    kq = RET_HEADS
    kv = RET_QK_COLS // RET_DV

    def fwd(off, width):
        return pl.BlockSpec((block, width), lambda b, h, c: (b * n + c, off + h))

    def bwd(off, width):
        return pl.BlockSpec((block, width), lambda b, h, c: (b * n + n - 1 - c, off + h))

    tab = lambda a: pl.BlockSpec((2, None) + a.shape[2:], lambda b, h, c: (0, h) + (0,) * (a.ndim - 2))
    state_spec = pl.BlockSpec((None, None, 2, RET_DK, RET_DV), lambda b, h, c: (b, h, 0, 0, 0))
    y_shape = jax.ShapeDtypeStruct((batch * seq, RET_V_COLS), BF16)
    return pl.pallas_call(
        functools.partial(_ret_kernel, sub=sub),
        grid=(batch, RET_HEADS, n),
        in_specs=[
            fwd(0, RET_DK), fwd(kq, RET_DK), fwd(kv, RET_DV),
            bwd(0, RET_DK), bwd(kq, RET_DK), bwd(kv, RET_DV),
            tab(intra), tab(qdec), tab(kdec),
            pl.BlockSpec(memory_space=pltpu.SMEM),
            state_spec,
        ],
        out_specs=[
            pl.BlockSpec((block, RET_DV), lambda b, h, c: (b * n + c, h)),
            pl.BlockSpec((block, RET_DV), lambda b, h, c: (b * n + n - 1 - c, h)),
            state_spec,
        ],
        out_shape=[y_shape, y_shape, jax.ShapeDtypeStruct(s0.shape, F32)],
        scratch_shapes=[pltpu.VMEM((2, RET_DK, RET_DV), F32)],
        compiler_params=_params("parallel", "parallel", "arbitrary"),
        name="retention",
    )(proj, proj, proj, proj, proj, proj, intra, qdec, kdec, cdec, s0)


def _retout_kernel(x_ref, yf_ref, yb_ref, gain_ref, sh_ref, sc_ref, g_ref, wg_ref, wo_ref, o_ref, a_ref):
    x = x_ref[...]
    h = _prenorm(x, gain_ref[...], sh_ref[...], sc_ref[...]).astype(BF16)
    hv = RET_V_COLS
    for c0 in range(0, hv, RET_DV):
        cols = slice(c0, c0 + RET_DV)
        gf = jnp.dot(h, wg_ref[:, c0:c0 + RET_DV], preferred_element_type=F32)
        gb = jnp.dot(h, wg_ref[:, hv + c0:hv + c0 + RET_DV], preferred_element_type=F32)
        y = _silu(gf) * yf_ref[:, cols].astype(F32) + _silu(gb) * yb_ref[:, cols].astype(F32)
        a_ref[:, cols] = y.astype(BF16)
    out = jnp.dot(a_ref[...], wo_ref[...], preferred_element_type=F32)
    o_ref[...] = x + g_ref[...] * out


def _ret_out(x, yf, yb, gain, mods, row_fn, w_gates, w_o, *, tm):
    m, d = x.shape
    hv = RET_V_COLS
    sh, sc, g = mods
    single = pl.Buffered(1)
    return pl.pallas_call(
        _retout_kernel,
        grid=(m // tm,),
        in_specs=[
            pl.BlockSpec((tm, d), lambda i: (i, 0)),
            pl.BlockSpec((tm, hv), lambda i: (i, 0)),
            pl.BlockSpec((tm, hv), lambda i: (i, 0)),
            _resident((1, d)),
            _row_vec_spec(d, row_fn),
            _row_vec_spec(d, row_fn),
            _row_vec_spec(d, row_fn),
            pl.BlockSpec(w_gates.shape, lambda i: (0, 0), pipeline_mode=single),
            pl.BlockSpec(w_o.shape, lambda i: (0, 0), pipeline_mode=single),
        ],
        out_specs=pl.BlockSpec((tm, d), lambda i: (i, 0)),
        out_shape=jax.ShapeDtypeStruct((m, d), F32),
        scratch_shapes=[pltpu.VMEM((tm, hv), BF16)],
        compiler_params=_params("parallel"),
        name="ret_out",
    )(x, yf, yb, gain, sh, sc, g, w_gates, w_o)


FFN_CHUNK = 256


def _ffn_kernel(x_ref, gain_ref, sh_ref, sc_ref, g_ref, wg_ref, wu_ref, wd_ref, *rest, mixer_proj):
    if mixer_proj:
        y_ref, wo_ref, gm_ref, o_ref, a_ref = rest
        x = x_ref[...] + gm_ref[...] * jnp.dot(y_ref[...], wo_ref[...], preferred_element_type=F32)
    else:
        o_ref, a_ref = rest
        x = x_ref[...]
    h = _prenorm(x, gain_ref[...], sh_ref[...], sc_ref[...]).astype(BF16)
    for c0 in range(0, wg_ref.shape[1], FFN_CHUNK):
        cols = slice(c0, c0 + FFN_CHUNK)
        gate = jnp.dot(h, wg_ref[:, cols], preferred_element_type=F32)
        up = jnp.dot(h, wu_ref[:, cols], preferred_element_type=F32)
        a_ref[:, cols] = (_silu(gate) * up).astype(BF16)
    out = jnp.dot(a_ref[...], wd_ref[...], preferred_element_type=F32)
    o_ref[...] = x + g_ref[...] * out


def _ffn(x, gain, mods, row_fn, wg, wu, wd, *, tm, mixer_proj=None):
    m, d = x.shape
    sh, sc, g = mods
    single = pl.Buffered(1)
    in_specs = [
        pl.BlockSpec((tm, d), lambda i: (i, 0)),
        _resident((1, d)),
        _row_vec_spec(d, row_fn),
        _row_vec_spec(d, row_fn),
        _row_vec_spec(d, row_fn),
        pl.BlockSpec(wg.shape, lambda i: (0, 0), pipeline_mode=single),
        pl.BlockSpec(wu.shape, lambda i: (0, 0), pipeline_mode=single),
        pl.BlockSpec(wd.shape, lambda i: (0, 0), pipeline_mode=single),
    ]
    args = [x, gain, sh, sc, g, wg, wu, wd]
    if mixer_proj is not None:
        y, w_o, gm = mixer_proj
        in_specs += [
            pl.BlockSpec((tm, y.shape[1]), lambda i: (i, 0)),
            pl.BlockSpec(w_o.shape, lambda i: (0, 0), pipeline_mode=single),
            _row_vec_spec(d, row_fn),
        ]
        args += [y, w_o, gm]
    return pl.pallas_call(
        functools.partial(_ffn_kernel, mixer_proj=mixer_proj is not None),
        grid=(m // tm,),
        in_specs=in_specs,
        out_specs=pl.BlockSpec((tm, d), lambda i: (i, 0)),
        out_shape=jax.ShapeDtypeStruct((m, d), F32),
        scratch_shapes=[pltpu.VMEM((tm, wg.shape[1]), BF16)],
        compiler_params=_params("parallel"),
        name="ffn_proj" if mixer_proj is not None else "ffn",
    )(*args)


def _axial_tables(n_tokens):
    rows = n_tokens // GRID_W
    row = np.repeat(np.arange(rows, dtype=np.float32), GRID_W)
    col = np.tile(np.arange(GRID_W, dtype=np.float32), rows)
    n_freq = HEAD_DIM // 4
    inv = jnp.asarray(ROPE_BASE, F32) ** (-jnp.arange(n_freq, dtype=F32) / n_freq)
    ang = jnp.concatenate([jnp.asarray(row)[:, None] * inv, jnp.asarray(col)[:, None] * inv], axis=-1)
    cos, sin = jnp.cos(ang), jnp.sin(ang)
    _, dim = _pair_layout()
    half = HEAD_DIM // 2
    sign = np.where(dim < half, -1.0, 1.0).astype(np.float32)
    return cos[:, dim % half], sin[:, dim % half] * sign


def _retention_rot_tables(n_tokens):
    inv = jnp.asarray(ROPE_BASE, F32) ** (-jnp.linspace(0.0, 1.0, RET_DK // 2, dtype=F32))
    ang = jnp.arange(n_tokens, dtype=F32)[:, None] * inv
    return jnp.cos(ang), jnp.sin(ang)


def kernel(x, c, ctx, c_ctx, ada_w, ada_b, norm_mix, norm_ffn, attn_w_qkv, attn_w_o, attn_q_norm, attn_k_norm,
           attn_sink, pool_w, pool_scale, ret_w_in, ret_w_o, ffn_w_gate, ffn_w_up, ffn_w_down):
    batch, seq, d = x.shape
    ctx_len = ctx.shape[1]
    mod_rows = 16
    assert batch < mod_rows and seq % 1024 == 0 and ctx_len % 256 == 0 and seq % GRID_W == 0

    cvec = jnp.zeros((mod_rows, d), F32).at[:batch].set(c).at[batch].set(c_ctx)
    mods = _ada_mods(cvec, ada_w, ada_b)
    mods = mods.reshape(DEPTH, mod_rows, 6, 1, d)

    x_lat = x.reshape(batch * seq, d)
    x_ctx = ctx.reshape(batch * ctx_len, d)

    tm_lat, tm_ctx = 512, 256
    lat_row = lambda i: i // (seq // tm_lat)
    ctx_row = lambda i: batch

    attn_cos, attn_sin = _axial_tables(seq)
    ret_cos, ret_sin = _retention_rot_tables(seq)
    head_of_pair, pair_dim = _pair_layout()
    same_head = (head_of_pair[:, None] == head_of_pair[None, :]).astype(np.float32)
    bd = jnp.asarray(np.concatenate([same_head, same_head], axis=0), BF16)
    nq, nk = ATTN_HEADS * HEAD_DIM, ATTN_KV_HEADS * HEAD_DIM
    qkv_cols = np.concatenate([_pair_cols(ATTN_HEADS), nq + _pair_cols(ATTN_KV_HEADS), np.arange(nq + nk, nq + 2 * nk)])

    for i in range(DEPTH):
        kind, slot = i % N_MIXERS, i // N_MIXERS
        need_ctx_out = i < DEPTH - 1
        mod = [mods[i, :, k] for k in range(6)]
        sh_m, sc_m, g_m, sh_f, sc_f, g_f = mod
        gain_m = norm_mix[i].reshape(1, d)
        gain_f = norm_ffn[i].reshape(1, d)
        proj_lat = proj_ctx = None

        if kind == 0:
            w_qkv = attn_w_qkv[slot][:, qkv_cols].astype(BF16)
            w_o = attn_w_o[slot].astype(BF16)
            qg = attn_q_norm[slot][pair_dim].reshape(1, LANES)
            kg = attn_k_norm[slot][pair_dim].reshape(1, LANES)
            sink = attn_sink[slot].astype(F32)
            bound = (1.02 * HEAD_DIM ** 0.5 * jnp.max(jnp.abs(attn_q_norm[slot]))
                     * jnp.max(jnp.abs(attn_k_norm[slot]))).astype(F32).reshape(1)
            q_c, k_c, v_c = _attn_qkv(x_ctx, gain_m, (sh_m, sc_m), ctx_row, w_qkv, qg, kg, attn_cos, attn_sin, bd,
                                      rotate=False, tm=tm_ctx, tiles_per_seq=1)
            q_l, k_l, v_l = _attn_qkv(x_lat, gain_m, (sh_m, sc_m), lat_row, w_qkv, qg, kg, attn_cos, attn_sin, bd,
                                      rotate=True, tm=tm_lat, tiles_per_seq=seq // tm_lat)
            a_l = _attention(bound, sink, q_l, k_l, v_l, k_c, v_c, batch=batch, seq=seq, ctx_len=ctx_len, local=True)
            proj_lat = (a_l, w_o, g_m)
            if need_ctx_out:
                a_c = _attention(bound, sink, q_c, None, None, k_c, v_c, batch=batch, seq=ctx_len, ctx_len=ctx_len,
                                 local=False)
                proj_ctx = (a_c, w_o, g_m)
        elif kind == 1:
            w_p = pool_w[slot].astype(BF16)
            ls = pool_scale[slot].reshape(1, d)
            x_lat = _pool_mixer(x_lat, gain_m, (sh_m, sc_m, g_m), lat_row, w_p, ls, tm=tm_lat, seq=seq)
            if need_ctx_out:
                x_ctx = _pool_mixer(x_ctx, gain_m, (sh_m, sc_m, g_m), ctx_row, w_p, ls, tm=tm_ctx, seq=ctx_len)
        else:
            w_qkv = ret_w_in[slot, :, :RET_QKV_COLS].astype(BF16)
            w_gates = ret_w_in[slot, :, RET_QKV_COLS:].astype(BF16)
            w_o = ret_w_o[slot].astype(BF16)
            zeros = jnp.zeros((batch, RET_HEADS, 2, RET_DK, RET_DV), F32)
            p_c = _ret_proj(x_ctx, gain_m, (sh_m, sc_m), ctx_row, w_qkv, ret_cos, ret_sin,
                            rotate=False, tm=tm_ctx, tiles_per_seq=1)
            yc_f, yc_b, s_c = _retention(p_c, zeros, batch=batch, seq=ctx_len, block=min(ctx_len, RET_BLOCK))
            p_l = _ret_proj(x_lat, gain_m, (sh_m, sc_m), lat_row, w_qkv, ret_cos, ret_sin,
                            rotate=True, tm=tm_lat, tiles_per_seq=seq // tm_lat)
            yl_f, yl_b, _ = _retention(p_l, s_c, batch=batch, seq=seq, block=RET_BLOCK)
            x_lat = _ret_out(x_lat, yl_f, yl_b, gain_m, (sh_m, sc_m, g_m), lat_row, w_gates, w_o, tm=tm_lat)
            if need_ctx_out:
                x_ctx = _ret_out(x_ctx, yc_f, yc_b, gain_m, (sh_m, sc_m, g_m), ctx_row, w_gates, w_o, tm=tm_ctx)

        wg = ffn_w_gate[i].astype(BF16)
        wu = ffn_w_up[i].astype(BF16)
        wd = ffn_w_down[i].astype(BF16)
        x_lat = _ffn(x_lat, gain_f, (sh_f, sc_f, g_f), lat_row, wg, wu, wd, tm=tm_lat, mixer_proj=proj_lat)
        if need_ctx_out:
            x_ctx = _ffn(x_ctx, gain_f, (sh_f, sc_f, g_f), ctx_row, wg, wu, wd, tm=tm_ctx, mixer_proj=proj_ctx)

    return x_lat.reshape(batch, seq, d)
```

```python
import functools

import jax
import jax.numpy as jnp
import numpy as np
from jax import lax
from jax.experimental import pallas as pl
from jax.experimental.pallas import tpu as pltpu

F32 = jnp.float32
BF16 = jnp.bfloat16

DEPTH = 4
GRID_W = 64
N_MIXERS = 3
ATTN_HEADS = 16
ATTN_KV_HEADS = 4
ATTN_GROUP = ATTN_HEADS // ATTN_KV_HEADS
HEAD_DIM = 64
WINDOW = 128
ATTN_BLOCK = 128
ROPE_BASE = 10000.0
NEG_INF = -1e30
POOL_WINDOWS = (2, 4, 8, 16)
POOL_HALO = 8
POOL_PAD = 16
RET_HEADS = 4
RET_DK = 256
RET_DV = 512
RET_SUB = 256
RET_BLOCK = 1024
RET_BWD_OFFSET = 0.5
NORM_EPS = 1e-6

VMEM_LIMIT_BYTES = 56 * 1024 * 1024
LANES = 128
ROW_SUB = 512


def _params(*sem):
    return pltpu.CompilerParams(dimension_semantics=sem, vmem_limit_bytes=VMEM_LIMIT_BYTES)


def _silu(x):
    return x * jax.nn.sigmoid(x)


def _prenorm(x, gain, shift, scale):
    y = x * lax.rsqrt(jnp.mean(x * x, axis=-1, keepdims=True) + NORM_EPS)
    return (y * gain) * (1.0 + scale) + shift


def _resident(shape):
    nd = len(shape)
    return pl.BlockSpec(shape, lambda *_: (0,) * nd)


def _ada_kernel(c_ref, w_ref, b_ref, o_ref):
    cond = _silu(c_ref[...])
    o_ref[...] = jnp.dot(cond.astype(BF16), w_ref[...].astype(BF16), preferred_element_type=F32) + b_ref[...]


def _ada_mods(cvec, ada_w, ada_b):
    depth, d, n = ada_w.shape
    rows = cvec.shape[0]
    tn = 1536
    return pl.pallas_call(
        _ada_kernel,
        grid=(depth, n // tn),
        in_specs=[
            pl.BlockSpec((rows, d), lambda l, j: (0, 0)),
            pl.BlockSpec((None, d, tn), lambda l, j: (l, 0, j)),
            pl.BlockSpec((None, 1, tn), lambda l, j: (l, 0, j)),
        ],
        out_specs=pl.BlockSpec((None, rows, tn), lambda l, j: (l, 0, j)),
        out_shape=jax.ShapeDtypeStruct((depth, rows, n), F32),
        compiler_params=_params("parallel", "parallel"),
        name="ada_mods",
    )(cvec, ada_w, ada_b.reshape(depth, 1, n))


def _row_vec_spec(d, row_fn):
    return pl.BlockSpec((None, 1, d), lambda i, *_: (row_fn(i), 0, 0))


def _qkv_kernel(x_ref, gain_ref, sh_ref, sc_ref, w_ref, qg_ref, kg_ref, cos_ref, sin_ref, bd_ref,
                q_ref, k_ref, v_ref, *, rotate):
    bd2 = bd_ref[...]
    qg = qg_ref[...]
    kg = kg_ref[...]
    nq = ATTN_HEADS * HEAD_DIM
    nk = ATTN_KV_HEADS * HEAD_DIM
    width = 2 * LANES

    def head_norm(xg, g):
        x2 = xg * xg
        hi = x2.astype(BF16)
        lo = (x2 - hi.astype(F32)).astype(BF16)
        ssq = jnp.dot(jnp.concatenate([hi, lo], axis=1), bd2, preferred_element_type=F32)
        return (xg * lax.rsqrt(ssq * (1.0 / HEAD_DIM) + NORM_EPS)) * g

    for r0 in range(0, x_ref.shape[0], ROW_SUB):
        rows = slice(r0, min(r0 + ROW_SUB, x_ref.shape[0]))
        h = _prenorm(x_ref[rows, :], gain_ref[...], sh_ref[...], sc_ref[...]).astype(BF16)

        def project(c0):
            return jnp.dot(h, w_ref[:, c0:c0 + width], preferred_element_type=F32)

        nxt = project(0)
        for j in range((nq + nk) // LANES):
            if j % 2 == 0:
                wide = nxt
                nxt = project((j + 2) * LANES)
            acc = wide[:, (j % 2) * LANES:(j % 2 + 1) * LANES]
            is_q = j < nq // LANES
            y = head_norm(acc, qg) * (HEAD_DIM ** -0.5) if is_q else head_norm(acc, kg)
            if rotate:
                y = y * cos_ref[rows, :] + pltpu.roll(y, LANES // 2, 1) * sin_ref[rows, :]
            if is_q:
                q_ref[rows, j * LANES:(j + 1) * LANES] = y.astype(BF16)
            else:
                k_ref[rows, j * LANES - nq:(j + 1) * LANES - nq] = y.astype(BF16)
        v_ref[rows, :] = nxt.astype(BF16)


def _pair_layout():
    lane = np.arange(LANES)
    head_of_pair = (lane // (HEAD_DIM // 2)) % 2
    dim = lane % (HEAD_DIM // 2) + (HEAD_DIM // 2) * (lane // HEAD_DIM)
    return head_of_pair, dim


def _pair_cols(n_heads):
    head_of_pair, dim = _pair_layout()
    tiles = np.arange(n_heads // 2)
    return ((2 * tiles[:, None] + head_of_pair[None, :]) * HEAD_DIM + dim[None, :]).ravel()


def _attn_qkv(x, gain, mods, row_fn, w, qg, kg, cos_t, sin_t, bd, *, rotate, tm, tiles_per_seq):
    m, d = x.shape
    n = w.shape[1]
    nq = ATTN_HEADS * HEAD_DIM
    nk = ATTN_KV_HEADS * HEAD_DIM
    sh, sc = mods
    tab_spec = pl.BlockSpec((tm, LANES), lambda i: (i % tiles_per_seq, 0))
    return pl.pallas_call(
        functools.partial(_qkv_kernel, rotate=rotate),
        grid=(m // tm,),
        in_specs=[
            pl.BlockSpec((tm, d), lambda i: (i, 0)),
            _resident((1, d)),
            _row_vec_spec(d, row_fn),
            _row_vec_spec(d, row_fn),
            _resident((d, n)),
            _resident((1, LANES)),
            _resident((1, LANES)),
            tab_spec,
            tab_spec,
            _resident((2 * LANES, LANES)),
        ],
        out_specs=[
            pl.BlockSpec((tm, nq), lambda i: (i, 0)),
            pl.BlockSpec((tm, nk), lambda i: (i, 0)),
            pl.BlockSpec((tm, nk), lambda i: (i, 0)),
        ],
        out_shape=[
            jax.ShapeDtypeStruct((m, nq), BF16),
            jax.ShapeDtypeStruct((m, nk), BF16),
            jax.ShapeDtypeStruct((m, nk), BF16),
        ],
        compiler_params=_params("parallel"),
        name="attn_qkv_rot" if rotate else "attn_qkv",
    )(x, gain, sh, sc, w, qg, kg, cos_t, sin_t, bd)


ATTN_SHIFT_LIMIT = 30.0
ATTN_QB = 4


def _attn_heads(bound_ref, sink_ref, q_ref, q_rows, k_parts, v_parts, mask_ref, o_ref, *, use_bound):
    blk = q_rows.stop - q_rows.start
    lane = lax.broadcasted_iota(jnp.int32, (1, LANES), 1)
    low = lane < HEAD_DIM
    pair_bit = (lane // (HEAD_DIM // 2)) % 2
    one = jnp.ones((), BF16)
    zero = jnp.zeros((), BF16)

    def rows(parts, cols):
        pieces = [r[rs, cols] for r, rs in parts]
        return pieces[0] if len(pieces) == 1 else jnp.concatenate(pieces, axis=0)

    def scores(h):
        par = h % 2
        cols = slice((h // 2) * LANES, (h // 2 + 1) * LANES)
        own_qk = pair_bit == par
        k2 = rows(k_parts, cols)
        q_parts = []
        for g in range(ATTN_GROUP):
            hd = h * ATTN_GROUP + g
            qt = q_ref[q_rows, (hd // 2) * LANES:(hd // 2 + 1) * LANES]
            if hd % 2 != par:
                qt = pltpu.roll(qt, HEAD_DIM // 2 if par == 1 else LANES - HEAD_DIM // 2, 1)
            q_parts.append(jnp.where(own_qk, qt, zero))
        q2 = jnp.concatenate(q_parts, axis=0)
        return lax.dot_general(q2, k2, (((1,), (1,)), ((), ())), preferred_element_type=F32)

    def finish(h, s):
        par = h % 2
        cols = slice((h // 2) * LANES, (h // 2 + 1) * LANES)
        own_v = low if par == 0 else jnp.logical_not(low)
        v2 = jnp.where(own_v, rows(v_parts, cols), one)
        heads = [h * ATTN_GROUP + g for g in range(ATTN_GROUP)]
        if mask_ref is not None:
            s = s + mask_ref[...]
        if use_bound:
            shifts = [jnp.maximum(bound_ref[0], sink_ref[hd]) for hd in heads]
            shift = jnp.concatenate([jnp.full((blk, 1), sh, F32) for sh in shifts], axis=0)
            es = [jnp.exp(jnp.full((1, LANES), sink_ref[hd] - sh, F32)) for hd, sh in zip(heads, shifts)]
        else:
            sink = jnp.concatenate([jnp.full((blk, 1), sink_ref[hd], F32) for hd in heads], axis=0)
            shift = jnp.maximum(jnp.max(s, axis=1, keepdims=True), sink)
            es_all = jnp.exp(sink - shift)
            es = [es_all[g * blk:(g + 1) * blk] for g in range(ATTN_GROUP)]
        p = jnp.exp(s - shift).astype(BF16)
        oa = jnp.dot(p, v2, preferred_element_type=F32)
        res = []
        for g, hd in enumerate(heads):
            a = oa[g * blk:(g + 1) * blk]
            r = pltpu.roll(a, HEAD_DIM, 1)
            res.append(a / (r + es[g]) if hd % 2 == par else r / (a + es[g]))
        for t in range(ATTN_GROUP // 2):
            tile = jnp.where(low, res[2 * t], res[2 * t + 1])
            c0 = (heads[2 * t] // 2) * LANES
            o_ref[q_rows, c0:c0 + LANES] = tile.astype(BF16)

    s_next = scores(0)
    for h in range(ATTN_KV_HEADS):
        s = s_next
        if h + 1 < ATTN_KV_HEADS:
            s_next = scores(h + 1)
        finish(h, s)


def _attn_kernel(bound_ref, sink_ref, q_ref, *refs, local):
    full = slice(None)
    if local:
        kp_ref, kc_ref, kn_ref, vp_ref, vc_ref, vn_ref, kx_ref, vx_ref = refs[:8]
        mask_refs = refs[8:8 + ATTN_QB]
        o_ref = refs[8 + ATTN_QB]
        blocks = [slice(j * ATTN_BLOCK, (j + 1) * ATTN_BLOCK) for j in range(ATTN_QB)]
        k_blocks = [(kp_ref, full)] + [(kc_ref, rs) for rs in blocks] + [(kn_ref, full)]
        v_blocks = [(vp_ref, full)] + [(vc_ref, rs) for rs in blocks] + [(vn_ref, full)]
        work = [(blocks[j], k_blocks[j:j + 3] + [(kx_ref, full)], v_blocks[j:j + 3] + [(vx_ref, full)], mask_refs[j])
                for j in range(ATTN_QB)]
    else:
        kx_ref, vx_ref, o_ref = refs
        work = [(slice(0, q_ref.shape[0]), [(kx_ref, full)], [(vx_ref, full)], None)]
    use_bound = bound_ref[0] <= ATTN_SHIFT_LIMIT

    def run(flag):
        for q_rows, k_parts, v_parts, mask_ref in work:
            _attn_heads(bound_ref, sink_ref, q_ref, q_rows, k_parts, v_parts, mask_ref, o_ref, use_bound=flag)

    @pl.when(use_bound)
    def _():
        run(True)

    @pl.when(jnp.logical_not(use_bound))
    def _():
        run(False)


def _attn_mask_table():
    span = 3 * ATTN_BLOCK
    offs = np.arange(span) - ATTN_BLOCK
    rel = offs[None, :] - np.arange(ATTN_BLOCK)[:, None]
    near = np.abs(rel) <= WINDOW
    tabs = []
    for variant in range(3):
        ok = near.copy()
        if variant == 0:
            ok[:, :ATTN_BLOCK] = False
        if variant == 2:
            ok[:, 2 * ATTN_BLOCK:] = False
        tabs.append(np.where(ok, 0.0, NEG_INF).astype(np.float32))
    return np.stack(tabs)


def _attention(bound, sink, q, k, v, kx, vx, *, batch, seq, ctx_len, local):
    nblk = seq // ATTN_BLOCK
    nq = ATTN_HEADS * HEAD_DIM
    nk = ATTN_KV_HEADS * HEAD_DIM
    qb = ATTN_QB if local else 1
    nstep = nblk // qb
    q_spec = pl.BlockSpec((qb * ATTN_BLOCK, nq), lambda b, i: (b * nstep + i, 0))
    x_spec = pl.BlockSpec((ctx_len, nk), lambda b, i: (b, 0))
    smem = pl.BlockSpec(memory_space=pltpu.SMEM)
    if local:
        local_mask = _attn_mask_table()
        mask = np.concatenate([local_mask, np.zeros((3, ATTN_BLOCK, ctx_len), np.float32)], axis=2)
        mask = jnp.asarray(np.tile(mask, (1, ATTN_GROUP, 1)))
        prev = pl.BlockSpec((ATTN_BLOCK, nk), lambda b, i: (b * nblk + jnp.maximum(qb * i - 1, 0), 0))
        cur = pl.BlockSpec((qb * ATTN_BLOCK, nk), lambda b, i: (b * nstep + i, 0))
        nxt = pl.BlockSpec((ATTN_BLOCK, nk), lambda b, i: (b * nblk + jnp.minimum(qb * (i + 1), nblk - 1), 0))

        def mask_spec(j):
            def index(b, i):
                g = qb * i + j
                return (jnp.where(g == 0, 0, jnp.where(g == nblk - 1, 2, 1)), 0, 0)
            return pl.BlockSpec((None,) + mask.shape[1:], index)

        in_specs = ([smem, smem, q_spec, prev, cur, nxt, prev, cur, nxt, x_spec, x_spec]
                    + [mask_spec(j) for j in range(qb)])
        args = (bound, sink, q, k, k, k, v, v, v, kx, vx) + (mask,) * qb
    else:
        in_specs = [smem, smem, q_spec, x_spec, x_spec]
        args = (bound, sink, q, kx, vx)
    return pl.pallas_call(
        functools.partial(_attn_kernel, local=local),
        grid=(batch, nstep),
        in_specs=in_specs,
        out_specs=q_spec,
        out_shape=jax.ShapeDtypeStruct(q.shape, BF16),
        compiler_params=_params("parallel", "parallel"),
        name="attn_local" if local else "attn_ctx",
    )(*args)


def _pool_kernel(xp_ref, x_ref, xn_ref, gain_ref, sh_ref, sc_ref, g_ref, w_ref, ls_ref, o_ref,
                 h_ref, pair_ref, quad_ref, oct_ref, *, tiles_per_seq, seq):
    tm, d = x_ref.shape
    it = pl.program_id(0) % tiles_per_seq
    gain, sh, sc = gain_ref[...], sh_ref[...], sc_ref[...]
    hp = jnp.where(it > 0, _prenorm(xp_ref[...], gain, sh, sc), 0.0)
    hn = jnp.where(it < tiles_per_seq - 1, _prenorm(xn_ref[...], gain, sh, sc), 0.0)
    x = x_ref[...]
    base = POOL_HALO
    h_ref[0:base, :] = hp
    h_ref[base:base + tm, :] = _prenorm(x, gain, sh, sc)
    h_ref[base + tm:base + tm + POOL_HALO, :] = hn
    h_ref[base + tm + POOL_HALO:, :] = jnp.zeros((POOL_PAD, d), F32)
    pos = it * tm + lax.broadcasted_iota(jnp.int32, (tm, 1), 0)
    gd = d // len(POOL_WINDOWS)
    n_pair, n_quad, n_oct = pair_ref.shape[0], quad_ref.shape[0], oct_ref.shape[0]
    for gi, win in enumerate(POOL_WINDOWS):
        cols = slice(gi * gd, (gi + 1) * gd)
        half = win // 2
        if win >= 4:
            pair_ref[:, cols] = h_ref[0:n_pair, cols] + h_ref[1:n_pair + 1, cols]
        if win >= 8:
            quad_ref[:, cols] = pair_ref[0:n_quad, cols] + pair_ref[2:n_quad + 2, cols]
        if win == 2:
            tot = h_ref[base - 1:base - 1 + tm, cols] + h_ref[base:base + tm, cols]
        elif win == 4:
            tot = pair_ref[base - 2:base - 2 + tm, cols] + pair_ref[base:base + tm, cols]
        elif win == 8:
            tot = quad_ref[base - 4:base - 4 + tm, cols] + quad_ref[base:base + tm, cols]
        else:
            oct_ref[:, cols] = quad_ref[0:n_oct, cols] + quad_ref[4:n_oct + 4, cols]
            tot = oct_ref[0:tm, cols] + oct_ref[base:base + tm, cols]
        cnt = jnp.minimum(pos + half, seq) - jnp.maximum(pos - half, 0)
        pooled = tot * (1.0 / cnt.astype(F32)) - h_ref[base:base + tm, cols]
        y = jnp.dot(pooled.astype(BF16), w_ref[gi], preferred_element_type=F32)
        o_ref[:, cols] = x[:, cols] + g_ref[:, cols] * (y * ls_ref[:, cols])


def _pool_mixer(x, gain, mods, row_fn, w, layer_scale, *, tm, seq):
    m, d = x.shape
    sh, sc, g = mods
    tiles_per_seq = seq // tm
    hb = tm // POOL_HALO
    nhalo = m // POOL_HALO
    return pl.pallas_call(
        functools.partial(_pool_kernel, tiles_per_seq=tiles_per_seq, seq=seq),
        grid=(m // tm,),
        in_specs=[
            pl.BlockSpec((POOL_HALO, d), lambda i: (jnp.maximum(i * hb - 1, 0), 0)),
            pl.BlockSpec((tm, d), lambda i: (i, 0)),
            pl.BlockSpec((POOL_HALO, d), lambda i: (jnp.minimum((i + 1) * hb, nhalo - 1), 0)),
            _resident((1, d)),
            _row_vec_spec(d, row_fn),
            _row_vec_spec(d, row_fn),
            _row_vec_spec(d, row_fn),
            _resident(w.shape),
            _resident((1, d)),
        ],
        out_specs=pl.BlockSpec((tm, d), lambda i: (i, 0)),
        out_shape=jax.ShapeDtypeStruct((m, d), F32),
        scratch_shapes=[pltpu.VMEM((tm + 2 * POOL_HALO + POOL_PAD, d), F32),
                        pltpu.VMEM((tm + 3 * POOL_HALO, d), F32),
                        pltpu.VMEM((tm + 2 * POOL_HALO, d), F32),
                        pltpu.VMEM((tm + POOL_HALO, d), F32)],
        compiler_params=_params("parallel"),
        name="pool_mixer",
    )(x, x, x, gain, sh, sc, g, w, layer_scale)


RET_TN = 512
RET_QK_COLS = 2 * RET_HEADS * RET_DK
RET_V_COLS = RET_HEADS * RET_DV
RET_QKV_COLS = RET_QK_COLS + RET_V_COLS


def _retproj_kernel(x_ref, gain_ref, sh_ref, sc_ref, w_ref, cos_ref, sin_ref, o_ref, *, rotate):
    half = RET_DK // 2
    q_cols = RET_HEADS * RET_DK
    for r0 in range(0, x_ref.shape[0], ROW_SUB):
        rows = slice(r0, min(r0 + ROW_SUB, x_ref.shape[0]))
        h = _prenorm(x_ref[rows, :], gain_ref[...], sh_ref[...], sc_ref[...]).astype(BF16)
        for c0 in range(0, RET_QKV_COLS, RET_TN):
            acc = jnp.dot(h, w_ref[:, c0:c0 + RET_TN], preferred_element_type=F32)
            if c0 >= RET_QK_COLS:
                o_ref[rows, c0:c0 + RET_TN] = acc.astype(BF16)
                continue
            scale = 1.0 if c0 < q_cols else RET_DK ** -0.5
            for hd in range(RET_TN // RET_DK):
                x1 = acc[:, hd * RET_DK:hd * RET_DK + half]
                x2 = acc[:, hd * RET_DK + half:(hd + 1) * RET_DK]
                if rotate:
                    cos, sin = cos_ref[rows, :], sin_ref[rows, :]
                    x1, x2 = x1 * cos - x2 * sin, x2 * cos + x1 * sin
                o_ref[rows, c0 + hd * RET_DK:c0 + hd * RET_DK + half] = (x1 * scale).astype(BF16)
                o_ref[rows, c0 + hd * RET_DK + half:c0 + (hd + 1) * RET_DK] = (x2 * scale).astype(BF16)


def _ret_proj(x, gain, mods, row_fn, w, cos_t, sin_t, *, rotate, tm, tiles_per_seq):
    m, d = x.shape
    n = w.shape[1]
    sh, sc = mods
    tab_spec = pl.BlockSpec((tm, RET_DK // 2), lambda i: (i % tiles_per_seq, 0))
    return pl.pallas_call(
        functools.partial(_retproj_kernel, rotate=rotate),
        grid=(m // tm,),
        in_specs=[
            pl.BlockSpec((tm, d), lambda i: (i, 0)),
            _resident((1, d)),
            _row_vec_spec(d, row_fn),
            _row_vec_spec(d, row_fn),
            pl.BlockSpec((d, n), lambda i: (0, 0), pipeline_mode=pl.Buffered(1)),
            tab_spec,
            tab_spec,
        ],
        out_specs=pl.BlockSpec((tm, n), lambda i: (i, 0)),
        out_shape=jax.ShapeDtypeStruct((m, n), BF16),
        compiler_params=_params("parallel"),
        name="ret_proj_rot" if rotate else "ret_proj",
    )(x, gain, sh, sc, w, cos_t, sin_t)


def _ret_kernel(qf_ref, kf_ref, vf_ref, qb_ref, kb_ref, vb_ref, intra_ref, qdec_ref, kdec_ref, cdec_ref,
                s0_ref, yf_ref, yb_ref, sout_ref, state_ref, *, sub):
    c = pl.program_id(2)
    hd = pl.program_id(1)

    @pl.when(c == 0)
    def _():
        state_ref[...] = s0_ref[...]

    n_sub = qf_ref.shape[0] // sub
    refs = ((qf_ref, kf_ref, vf_ref, yf_ref), (qb_ref, kb_ref, vb_ref, yb_ref))
    steps = []
    for j in range(n_sub):
        steps.append((0, slice(j * sub, (j + 1) * sub)))
        steps.append((1, slice((n_sub - 1 - j) * sub, (n_sub - j) * sub)))

    decayed = []
    for d, rows in steps:
        q_ref, k_ref, _, _ = refs[d]
        sc = lax.dot_general(q_ref[rows, :], k_ref[rows, :], (((1,), (1,)), ((), ())), preferred_element_type=F32)
        decayed.append((sc * intra_ref[d]).astype(BF16))

    for (d, rows), sc in zip(steps, decayed):
        q_ref, k_ref, v_ref, y_ref = refs[d]
        q, k, v = q_ref[rows, :], k_ref[rows, :], v_ref[rows, :]
        s = state_ref[d]
        y = (jnp.dot(sc, v, preferred_element_type=F32)
             + jnp.dot(q, s.astype(BF16), preferred_element_type=F32) * qdec_ref[d])
        kd = (k.astype(F32) * kdec_ref[d]).astype(BF16)
        state_ref[d] = s * cdec_ref[d * RET_HEADS + hd] + lax.dot_general(
            kd, v, (((0,), (0,)), ((), ())), preferred_element_type=F32)
        y = y * lax.rsqrt(jnp.mean(y * y, axis=-1, keepdims=True) + NORM_EPS)
        y_ref[rows, :] = y.astype(BF16)

    @pl.when(c == pl.num_programs(2) - 1)
    def _():
        sout_ref[...] = state_ref[...]


def _ret_tables(chunk):
    heads = np.arange(RET_HEADS, dtype=np.float64)
    log_g = [np.log1p(-np.exp2(-5.0 - heads)), np.log1p(-np.exp2(-5.0 - RET_BWD_OFFSET - heads))]
    pos = np.arange(chunk, dtype=np.float64)
    diff = pos[:, None] - pos[None, :]
    intra, qdec, kdec, cdec = [], [], [], []
    for d, lg in enumerate(log_g):
        lg3 = lg[:, None, None]
        if d == 0:
            intra.append(np.where(diff >= 0, np.exp(np.maximum(diff, 0.0)[None] * lg3), 0.0))
            qdec.append(np.exp((pos[None, :] + 1.0) * lg[:, None]))
            kdec.append(np.exp((chunk - 1.0 - pos)[None, :] * lg[:, None]))
        else:
            intra.append(np.where(diff <= 0, np.exp(np.maximum(-diff, 0.0)[None] * lg3), 0.0))
            qdec.append(np.exp((chunk - pos)[None, :] * lg[:, None]))
            kdec.append(np.exp(pos[None, :] * lg[:, None]))
        cdec.append(np.exp(chunk * lg))
    intra = np.stack(intra).astype(np.float32)
    qdec = np.stack(qdec).astype(np.float32)[..., None]
    kdec = np.stack(kdec).astype(np.float32)[..., None]
    cdec = np.stack(cdec).astype(np.float32).reshape(-1)
    return jnp.asarray(intra), jnp.asarray(qdec), jnp.asarray(kdec), jnp.asarray(cdec)


def _retention(proj, s0, *, batch, seq, block):
    sub = min(block, RET_SUB)
    intra, qdec, kdec, cdec = _ret_tables(sub)
    n = seq // block
    kq = RET_HEADS
    kv = RET_QK_COLS // RET_DV

    def fwd(off, width):
        return pl.BlockSpec((block, width), lambda b, h, c: (b * n + c, off + h))

    def bwd(off, width):
        return pl.BlockSpec((block, width), lambda b, h, c: (b * n + n - 1 - c, off + h))

    tab = lambda a: pl.BlockSpec((2, None) + a.shape[2:], lambda b, h, c: (0, h) + (0,) * (a.ndim - 2))
    state_spec = pl.BlockSpec((None, None, 2, RET_DK, RET_DV), lambda b, h, c: (b, h, 0, 0, 0))
    y_shape = jax.ShapeDtypeStruct((batch * seq, RET_V_COLS), BF16)
    return pl.pallas_call(
        functools.partial(_ret_kernel, sub=sub),
        grid=(batch, RET_HEADS, n),
        in_specs=[
            fwd(0, RET_DK), fwd(kq, RET_DK), fwd(kv, RET_DV),
            bwd(0, RET_DK), bwd(kq, RET_DK), bwd(kv, RET_DV),
            tab(intra), tab(qdec), tab(kdec),
            pl.BlockSpec(memory_space=pltpu.SMEM),
            state_spec,
        ],
        out_specs=[
            pl.BlockSpec((block, RET_DV), lambda b, h, c: (b * n + c, h)),
            pl.BlockSpec((block, RET_DV), lambda b, h, c: (b * n + n - 1 - c, h)),
            state_spec,
        ],
        out_shape=[y_shape, y_shape, jax.ShapeDtypeStruct(s0.shape, F32)],
        scratch_shapes=[pltpu.VMEM((2, RET_DK, RET_DV), F32)],
        compiler_params=_params("parallel", "parallel", "arbitrary"),
        name="retention",
    )(proj, proj, proj, proj, proj, proj, intra, qdec, kdec, cdec, s0)


def _retout_kernel(x_ref, yf_ref, yb_ref, gain_ref, sh_ref, sc_ref, g_ref, wg_ref, wo_ref, o_ref, a_ref):
    hv = RET_V_COLS
    for r0 in range(0, x_ref.shape[0], ROW_SUB):
        rows = slice(r0, min(r0 + ROW_SUB, x_ref.shape[0]))
        x = x_ref[rows, :]
        h = _prenorm(x, gain_ref[...], sh_ref[...], sc_ref[...]).astype(BF16)
        for c0 in range(0, hv, RET_DV):
            cols = slice(c0, c0 + RET_DV)
            gf = jnp.dot(h, wg_ref[:, c0:c0 + RET_DV], preferred_element_type=F32)
            gb = jnp.dot(h, wg_ref[:, hv + c0:hv + c0 + RET_DV], preferred_element_type=F32)
            y = _silu(gf) * yf_ref[rows, cols].astype(F32) + _silu(gb) * yb_ref[rows, cols].astype(F32)
            a_ref[rows, cols] = y.astype(BF16)
        out = jnp.dot(a_ref[rows, :], wo_ref[...], preferred_element_type=F32)
        o_ref[rows, :] = x + g_ref[...] * out


def _ret_out(x, yf, yb, gain, mods, row_fn, w_gates, w_o, *, tm):
    m, d = x.shape
    hv = RET_V_COLS
    sh, sc, g = mods
    single = pl.Buffered(1)
    return pl.pallas_call(
        _retout_kernel,
        grid=(m // tm,),
        in_specs=[
            pl.BlockSpec((tm, d), lambda i: (i, 0)),
            pl.BlockSpec((tm, hv), lambda i: (i, 0)),
            pl.BlockSpec((tm, hv), lambda i: (i, 0)),
            _resident((1, d)),
            _row_vec_spec(d, row_fn),
            _row_vec_spec(d, row_fn),
            _row_vec_spec(d, row_fn),
            pl.BlockSpec(w_gates.shape, lambda i: (0, 0), pipeline_mode=single),
            pl.BlockSpec(w_o.shape, lambda i: (0, 0), pipeline_mode=single),
        ],
        out_specs=pl.BlockSpec((tm, d), lambda i: (i, 0)),
        out_shape=jax.ShapeDtypeStruct((m, d), F32),
        scratch_shapes=[pltpu.VMEM((tm, hv), BF16)],
        compiler_params=_params("parallel"),
        name="ret_out",
    )(x, yf, yb, gain, sh, sc, g, w_gates, w_o)


FFN_CHUNK = 256


def _ffn_kernel(x_ref, gain_ref, sh_ref, sc_ref, g_ref, wg_ref, wu_ref, wd_ref, *rest, mixer_proj):
    if mixer_proj:
        y_ref, wo_ref, gm_ref, o_ref, a_ref = rest
    else:
        o_ref, a_ref = rest
    for r0 in range(0, x_ref.shape[0], ROW_SUB):
        rows = slice(r0, min(r0 + ROW_SUB, x_ref.shape[0]))
        x = x_ref[rows, :]
        if mixer_proj:
            x = x + gm_ref[...] * jnp.dot(y_ref[rows, :], wo_ref[...], preferred_element_type=F32)
        h = _prenorm(x, gain_ref[...], sh_ref[...], sc_ref[...]).astype(BF16)
        for c0 in range(0, wg_ref.shape[1], FFN_CHUNK):
            cols = slice(c0, c0 + FFN_CHUNK)
            gate = jnp.dot(h, wg_ref[:, cols], preferred_element_type=F32)
            up = jnp.dot(h, wu_ref[:, cols], preferred_element_type=F32)
            a_ref[rows, cols] = (_silu(gate) * up).astype(BF16)
        out = jnp.dot(a_ref[rows, :], wd_ref[...], preferred_element_type=F32)
        o_ref[rows, :] = x + g_ref[...] * out


def _ffn(x, gain, mods, row_fn, wg, wu, wd, *, tm, mixer_proj=None):
    m, d = x.shape
    sh, sc, g = mods
    single = pl.Buffered(1)
    in_specs = [
        pl.BlockSpec((tm, d), lambda i: (i, 0)),
        _resident((1, d)),
        _row_vec_spec(d, row_fn),
        _row_vec_spec(d, row_fn),
        _row_vec_spec(d, row_fn),
        pl.BlockSpec(wg.shape, lambda i: (0, 0), pipeline_mode=single),
        pl.BlockSpec(wu.shape, lambda i: (0, 0), pipeline_mode=single),
        pl.BlockSpec(wd.shape, lambda i: (0, 0), pipeline_mode=single),
    ]
    args = [x, gain, sh, sc, g, wg, wu, wd]
    if mixer_proj is not None:
        y, w_o, gm = mixer_proj
        in_specs += [
            pl.BlockSpec((tm, y.shape[1]), lambda i: (i, 0)),
            pl.BlockSpec(w_o.shape, lambda i: (0, 0), pipeline_mode=single),
            _row_vec_spec(d, row_fn),
        ]
        args += [y, w_o, gm]
    return pl.pallas_call(
        functools.partial(_ffn_kernel, mixer_proj=mixer_proj is not None),
        grid=(m // tm,),
        in_specs=in_specs,
        out_specs=pl.BlockSpec((tm, d), lambda i: (i, 0)),
        out_shape=jax.ShapeDtypeStruct((m, d), F32),
        scratch_shapes=[pltpu.VMEM((tm, wg.shape[1]), BF16)],
        compiler_params=_params("parallel"),
        name="ffn_proj" if mixer_proj is not None else "ffn",
    )(*args)


def _axial_tables(n_tokens):
    rows = n_tokens // GRID_W
    row = np.repeat(np.arange(rows, dtype=np.float32), GRID_W)
    col = np.tile(np.arange(GRID_W, dtype=np.float32), rows)
    n_freq = HEAD_DIM // 4
    inv = jnp.asarray(ROPE_BASE, F32) ** (-jnp.arange(n_freq, dtype=F32) / n_freq)
    ang = jnp.concatenate([jnp.asarray(row)[:, None] * inv, jnp.asarray(col)[:, None] * inv], axis=-1)
    cos, sin = jnp.cos(ang), jnp.sin(ang)
    _, dim = _pair_layout()
    half = HEAD_DIM // 2
    sign = np.where(dim < half, -1.0, 1.0).astype(np.float32)
    return cos[:, dim % half], sin[:, dim % half] * sign


def _retention_rot_tables(n_tokens):
    inv = jnp.asarray(ROPE_BASE, F32) ** (-jnp.linspace(0.0, 1.0, RET_DK // 2, dtype=F32))
    ang = jnp.arange(n_tokens, dtype=F32)[:, None] * inv
    return jnp.cos(ang), jnp.sin(ang)


def kernel(x, c, ctx, c_ctx, ada_w, ada_b, norm_mix, norm_ffn, attn_w_qkv, attn_w_o, attn_q_norm, attn_k_norm,
           attn_sink, pool_w, pool_scale, ret_w_in, ret_w_o, ffn_w_gate, ffn_w_up, ffn_w_down):
    batch, seq, d = x.shape
    ctx_len = ctx.shape[1]
    mod_rows = 16
    assert batch < mod_rows and seq % 1024 == 0 and ctx_len % 256 == 0 and seq % GRID_W == 0

    cvec = jnp.zeros((mod_rows, d), F32).at[:batch].set(c).at[batch].set(c_ctx)
    mods = _ada_mods(cvec, ada_w, ada_b)
    mods = mods.reshape(DEPTH, mod_rows, 6, 1, d)

    x_lat = x.reshape(batch * seq, d)
    x_ctx = ctx.reshape(batch * ctx_len, d)

    tm_lat, tm_ctx, tm_pool = 1024, 256, 512
    lat_row = lambda i: i // (seq // tm_lat)
    ctx_row = lambda i: batch

    attn_cos, attn_sin = _axial_tables(seq)
    ret_cos, ret_sin = _retention_rot_tables(seq)
    head_of_pair, pair_dim = _pair_layout()
    same_head = (head_of_pair[:, None] == head_of_pair[None, :]).astype(np.float32)
    bd = jnp.asarray(np.concatenate([same_head, same_head], axis=0), BF16)
    nq, nk = ATTN_HEADS * HEAD_DIM, ATTN_KV_HEADS * HEAD_DIM
    qkv_cols = np.concatenate([_pair_cols(ATTN_HEADS), nq + _pair_cols(ATTN_KV_HEADS), np.arange(nq + nk, nq + 2 * nk)])

    for i in range(DEPTH):
        kind, slot = i % N_MIXERS, i // N_MIXERS
        need_ctx_out = i < DEPTH - 1
        mod = [mods[i, :, k] for k in range(6)]
        sh_m, sc_m, g_m, sh_f, sc_f, g_f = mod
        gain_m = norm_mix[i].reshape(1, d)
        gain_f = norm_ffn[i].reshape(1, d)
        proj_lat = proj_ctx = None

        if kind == 0:
            w_qkv = attn_w_qkv[slot][:, qkv_cols].astype(BF16)
            w_o = attn_w_o[slot].astype(BF16)
            qg = attn_q_norm[slot][pair_dim].reshape(1, LANES)
            kg = attn_k_norm[slot][pair_dim].reshape(1, LANES)
            sink = attn_sink[slot].astype(F32)
            bound = (1.02 * HEAD_DIM ** 0.5 * jnp.max(jnp.abs(attn_q_norm[slot]))
                     * jnp.max(jnp.abs(attn_k_norm[slot]))).astype(F32).reshape(1)
            q_c, k_c, v_c = _attn_qkv(x_ctx, gain_m, (sh_m, sc_m), ctx_row, w_qkv, qg, kg, attn_cos, attn_sin, bd,
                                      rotate=False, tm=tm_ctx, tiles_per_seq=1)
            q_l, k_l, v_l = _attn_qkv(x_lat, gain_m, (sh_m, sc_m), lat_row, w_qkv, qg, kg, attn_cos, attn_sin, bd,
                                      rotate=True, tm=tm_lat, tiles_per_seq=seq // tm_lat)
            a_l = _attention(bound, sink, q_l, k_l, v_l, k_c, v_c, batch=batch, seq=seq, ctx_len=ctx_len, local=True)
            proj_lat = (a_l, w_o, g_m)
            if need_ctx_out:
                a_c = _attention(bound, sink, q_c, None, None, k_c, v_c, batch=batch, seq=ctx_len, ctx_len=ctx_len,
                                 local=False)
                proj_ctx = (a_c, w_o, g_m)
        elif kind == 1:
            w_p = pool_w[slot].astype(BF16)
            ls = pool_scale[slot].reshape(1, d)
            pool_row = lambda i: i // (seq // tm_pool)
            x_lat = _pool_mixer(x_lat, gain_m, (sh_m, sc_m, g_m), pool_row, w_p, ls, tm=tm_pool, seq=seq)
            if need_ctx_out:
                x_ctx = _pool_mixer(x_ctx, gain_m, (sh_m, sc_m, g_m), ctx_row, w_p, ls, tm=tm_ctx, seq=ctx_len)
        else:
            w_qkv = ret_w_in[slot, :, :RET_QKV_COLS].astype(BF16)
            w_gates = ret_w_in[slot, :, RET_QKV_COLS:].astype(BF16)
            w_o = ret_w_o[slot].astype(BF16)
            zeros = jnp.zeros((batch, RET_HEADS, 2, RET_DK, RET_DV), F32)
            p_c = _ret_proj(x_ctx, gain_m, (sh_m, sc_m), ctx_row, w_qkv, ret_cos, ret_sin,
                            rotate=False, tm=tm_ctx, tiles_per_seq=1)
            yc_f, yc_b, s_c = _retention(p_c, zeros, batch=batch, seq=ctx_len, block=min(ctx_len, RET_BLOCK))
            p_l = _ret_proj(x_lat, gain_m, (sh_m, sc_m), lat_row, w_qkv, ret_cos, ret_sin,
                            rotate=True, tm=tm_lat, tiles_per_seq=seq // tm_lat)
            yl_f, yl_b, _ = _retention(p_l, s_c, batch=batch, seq=seq, block=RET_BLOCK)
            x_lat = _ret_out(x_lat, yl_f, yl_b, gain_m, (sh_m, sc_m, g_m), lat_row, w_gates, w_o, tm=tm_lat)
            if need_ctx_out:
                x_ctx = _ret_out(x_ctx, yc_f, yc_b, gain_m, (sh_m, sc_m, g_m), ctx_row, w_gates, w_o, tm=tm_ctx)

        wg = ffn_w_gate[i].astype(BF16)
        wu = ffn_w_up[i].astype(BF16)
        wd = ffn_w_down[i].astype(BF16)
        x_lat = _ffn(x_lat, gain_f, (sh_f, sc_f, g_f), lat_row, wg, wu, wd, tm=tm_lat, mixer_proj=proj_lat)
        if need_ctx_out:
            x_ctx = _ffn(x_ctx, gain_f, (sh_f, sc_f, g_f), ctx_row, wg, wu, wd, tm=tm_ctx, mixer_proj=proj_ctx)

    return x_lat.reshape(batch, seq, d)
```

```python
import functools

import jax
import jax.numpy as jnp
import numpy as np
from jax import lax
from jax.experimental import pallas as pl
from jax.experimental.pallas import tpu as pltpu

F32 = jnp.float32
BF16 = jnp.bfloat16

DEPTH = 4
GRID_W = 64
N_MIXERS = 3
ATTN_HEADS = 16
ATTN_KV_HEADS = 4
ATTN_GROUP = ATTN_HEADS // ATTN_KV_HEADS
HEAD_DIM = 64
WINDOW = 128
ATTN_BLOCK = 128
ROPE_BASE = 10000.0
NEG_INF = -1e30
POOL_WINDOWS = (2, 4, 8, 16)
POOL_HALO = 8
POOL_PAD = 16
RET_HEADS = 4
RET_DK = 256
RET_DV = 512
RET_SUB = 256
RET_BLOCK = 2048
RET_BWD_OFFSET = 0.5
NORM_EPS = 1e-6

VMEM_LIMIT_BYTES = 56 * 1024 * 1024
LANES = 128
ROW_SUB = 512


def _params(*sem):
    return pltpu.CompilerParams(dimension_semantics=sem, vmem_limit_bytes=VMEM_LIMIT_BYTES)


def _silu(x):
    return x * jax.nn.sigmoid(x)


def _prenorm(x, gain, shift, scale):
    y = x * lax.rsqrt(jnp.mean(x * x, axis=-1, keepdims=True) + NORM_EPS)
    return (y * gain) * (1.0 + scale) + shift


def _resident(shape):
    nd = len(shape)
    return pl.BlockSpec(shape, lambda *_: (0,) * nd)


def _stacked(arr, index, col_block=0, cols=None):
    shape = arr.shape[1:] if cols is None else arr.shape[1:-1] + (cols,)
    nd = len(shape)
    return pl.BlockSpec((None,) + shape, lambda *_: (index,) + (0,) * (nd - 1) + (col_block,),
                        pipeline_mode=pl.Buffered(1))


def _ada_kernel(c_ref, w_ref, b_ref, o_ref):
    cond = _silu(c_ref[...])
    o_ref[...] = jnp.dot(cond.astype(BF16), w_ref[...].astype(BF16), preferred_element_type=F32) + b_ref[...]


def _ada_mods(cvec, ada_w, ada_b):
    depth, d, n = ada_w.shape
    rows = cvec.shape[0]
    tn = 1536
    return pl.pallas_call(
        _ada_kernel,
        grid=(depth, n // tn),
        in_specs=[
            pl.BlockSpec((rows, d), lambda l, j: (0, 0)),
            pl.BlockSpec((None, d, tn), lambda l, j: (l, 0, j)),
            pl.BlockSpec((None, 1, tn), lambda l, j: (l, 0, j)),
        ],
        out_specs=pl.BlockSpec((None, rows, tn), lambda l, j: (l, 0, j)),
        out_shape=jax.ShapeDtypeStruct((depth, rows, n), F32),
        compiler_params=_params("parallel", "parallel"),
        name="ada_mods",
    )(cvec, ada_w, ada_b.reshape(depth, 1, n))


def _row_vec_spec(d, row_fn):
    return pl.BlockSpec((None, 1, d), lambda i, *_: (row_fn(i), 0, 0))


def _qkv_kernel(x_ref, gain_ref, sh_ref, sc_ref, w_ref, qg_ref, kg_ref, cos_ref, sin_ref, bd_ref,
                q_ref, k_ref, v_ref, *, rotate):
    bd2 = bd_ref[...]
    qg = qg_ref[...]
    kg = kg_ref[...]
    nq = ATTN_HEADS * HEAD_DIM
    nk = ATTN_KV_HEADS * HEAD_DIM
    width = 2 * LANES

    def head_norm(xg, g):
        x2 = xg * xg
        hi = x2.astype(BF16)
        lo = (x2 - hi.astype(F32)).astype(BF16)
        ssq = jnp.dot(jnp.concatenate([hi, lo], axis=1), bd2, preferred_element_type=F32)
        return (xg * lax.rsqrt(ssq * (1.0 / HEAD_DIM) + NORM_EPS)) * g

    for r0 in range(0, x_ref.shape[0], ROW_SUB):
        rows = slice(r0, min(r0 + ROW_SUB, x_ref.shape[0]))
        h = _prenorm(x_ref[rows, :], gain_ref[...], sh_ref[...], sc_ref[...]).astype(BF16)

        def project(c0):
            return jnp.dot(h, w_ref[:, c0:c0 + width], preferred_element_type=F32)

        nxt = project(0)
        for j in range((nq + nk) // LANES):
            if j % 2 == 0:
                wide = nxt
                nxt = project((j + 2) * LANES)
            acc = wide[:, (j % 2) * LANES:(j % 2 + 1) * LANES]
            is_q = j < nq // LANES
            y = head_norm(acc, qg) * (HEAD_DIM ** -0.5) if is_q else head_norm(acc, kg)
            if rotate:
                y = y * cos_ref[rows, :] + pltpu.roll(y, LANES // 2, 1) * sin_ref[rows, :]
            if is_q:
                q_ref[rows, j * LANES:(j + 1) * LANES] = y.astype(BF16)
            else:
                k_ref[rows, j * LANES - nq:(j + 1) * LANES - nq] = y.astype(BF16)
        v_ref[rows, :] = nxt.astype(BF16)


def _pair_layout():
    lane = np.arange(LANES)
    head_of_pair = (lane // (HEAD_DIM // 2)) % 2
    dim = lane % (HEAD_DIM // 2) + (HEAD_DIM // 2) * (lane // HEAD_DIM)
    return head_of_pair, dim


def _attn_qkv(x, gain, mods, row_fn, w, slot, qg, kg, cos_t, sin_t, bd, *, rotate, tm, tiles_per_seq):
    m, d = x.shape
    nq = ATTN_HEADS * HEAD_DIM
    nk = ATTN_KV_HEADS * HEAD_DIM
    sh, sc = mods
    tab_spec = pl.BlockSpec((tm, LANES), lambda i: (i % tiles_per_seq, 0))
    return pl.pallas_call(
        functools.partial(_qkv_kernel, rotate=rotate),
        grid=(m // tm,),
        in_specs=[
            pl.BlockSpec((tm, d), lambda i: (i, 0)),
            _resident((1, d)),
            _row_vec_spec(d, row_fn),
            _row_vec_spec(d, row_fn),
            _stacked(w, slot),
            _resident((1, LANES)),
            _resident((1, LANES)),
            tab_spec,
            tab_spec,
            _resident((2 * LANES, LANES)),
        ],
        out_specs=[
            pl.BlockSpec((tm, nq), lambda i: (i, 0)),
            pl.BlockSpec((tm, nk), lambda i: (i, 0)),
            pl.BlockSpec((tm, nk), lambda i: (i, 0)),
        ],
        out_shape=[
            jax.ShapeDtypeStruct((m, nq), BF16),
            jax.ShapeDtypeStruct((m, nk), BF16),
            jax.ShapeDtypeStruct((m, nk), BF16),
        ],
        compiler_params=_params("parallel"),
        name="attn_qkv_rot" if rotate else "attn_qkv",
    )(x, gain, sh, sc, w, qg, kg, cos_t, sin_t, bd)


ATTN_SHIFT_LIMIT = 30.0
ATTN_QB = 4


def _attn_heads(bound_ref, sink_ref, q_ref, q_rows, k_parts, v_parts, mask_ref, o_ref, *, use_bound):
    blk = q_rows.stop - q_rows.start
    lane = lax.broadcasted_iota(jnp.int32, (1, LANES), 1)
    low = lane < HEAD_DIM
    pair_bit = (lane // (HEAD_DIM // 2)) % 2
    one = jnp.ones((), BF16)
    zero = jnp.zeros((), BF16)

    def rows(parts, cols):
        pieces = [r[rs, cols] for r, rs in parts]
        return pieces[0] if len(pieces) == 1 else jnp.concatenate(pieces, axis=0)

    def scores(h):
        par = h % 2
        cols = slice((h // 2) * LANES, (h // 2 + 1) * LANES)
        own_qk = pair_bit == par
        k2 = rows(k_parts, cols)
        q_parts = []
        for g in range(ATTN_GROUP):
            hd = h * ATTN_GROUP + g
            qt = q_ref[q_rows, (hd // 2) * LANES:(hd // 2 + 1) * LANES]
            if hd % 2 != par:
                qt = pltpu.roll(qt, HEAD_DIM // 2 if par == 1 else LANES - HEAD_DIM // 2, 1)
            q_parts.append(jnp.where(own_qk, qt, zero))
        q2 = jnp.concatenate(q_parts, axis=0)
        return lax.dot_general(q2, k2, (((1,), (1,)), ((), ())), preferred_element_type=F32)

    def finish(h, s):
        par = h % 2
        cols = slice((h // 2) * LANES, (h // 2 + 1) * LANES)
        own_v = low if par == 0 else jnp.logical_not(low)
        v2 = jnp.where(own_v, rows(v_parts, cols), one)
        heads = [h * ATTN_GROUP + g for g in range(ATTN_GROUP)]
        if mask_ref is not None:
            s = s + mask_ref[...]
        if use_bound:
            shifts = [jnp.maximum(bound_ref[0], sink_ref[hd]) for hd in heads]
            shift = jnp.concatenate([jnp.full((blk, 1), sh, F32) for sh in shifts], axis=0)
            es = [jnp.exp(jnp.full((1, LANES), sink_ref[hd] - sh, F32)) for hd, sh in zip(heads, shifts)]
        else:
            sink = jnp.concatenate([jnp.full((blk, 1), sink_ref[hd], F32) for hd in heads], axis=0)
            shift = jnp.maximum(jnp.max(s, axis=1, keepdims=True), sink)
            es_all = jnp.exp(sink - shift)
            es = [es_all[g * blk:(g + 1) * blk] for g in range(ATTN_GROUP)]
        p = jnp.exp(s - shift).astype(BF16)
        oa = jnp.dot(p, v2, preferred_element_type=F32)
        res = []
        for g, hd in enumerate(heads):
            a = oa[g * blk:(g + 1) * blk]
            r = pltpu.roll(a, HEAD_DIM, 1)
            res.append(a / (r + es[g]) if hd % 2 == par else r / (a + es[g]))
        for t in range(ATTN_GROUP // 2):
            tile = jnp.where(low, res[2 * t], res[2 * t + 1])
            c0 = (heads[2 * t] // 2) * LANES
            o_ref[q_rows, c0:c0 + LANES] = tile.astype(BF16)

    s_next = scores(0)
    for h in range(ATTN_KV_HEADS):
        s = s_next
        if h + 1 < ATTN_KV_HEADS:
            s_next = scores(h + 1)
        finish(h, s)


def _attn_kernel(bound_ref, sink_ref, q_ref, *refs, local):
    full = slice(None)
    if local:
        kp_ref, kc_ref, kn_ref, vp_ref, vc_ref, vn_ref, kx_ref, vx_ref = refs[:8]
        mask_refs = refs[8:8 + ATTN_QB]
        o_ref = refs[8 + ATTN_QB]
        blocks = [slice(j * ATTN_BLOCK, (j + 1) * ATTN_BLOCK) for j in range(ATTN_QB)]
        k_blocks = [(kp_ref, full)] + [(kc_ref, rs) for rs in blocks] + [(kn_ref, full)]
        v_blocks = [(vp_ref, full)] + [(vc_ref, rs) for rs in blocks] + [(vn_ref, full)]
        work = [(blocks[j], k_blocks[j:j + 3] + [(kx_ref, full)], v_blocks[j:j + 3] + [(vx_ref, full)], mask_refs[j])
                for j in range(ATTN_QB)]
    else:
        kx_ref, vx_ref, o_ref = refs
        work = [(slice(0, q_ref.shape[0]), [(kx_ref, full)], [(vx_ref, full)], None)]
    use_bound = bound_ref[0] <= ATTN_SHIFT_LIMIT

    def run(flag):
        for q_rows, k_parts, v_parts, mask_ref in work:
            _attn_heads(bound_ref, sink_ref, q_ref, q_rows, k_parts, v_parts, mask_ref, o_ref, use_bound=flag)

    @pl.when(use_bound)
    def _():
        run(True)

    @pl.when(jnp.logical_not(use_bound))
    def _():
        run(False)


def _attn_mask_table():
    span = 3 * ATTN_BLOCK
    offs = np.arange(span) - ATTN_BLOCK
    rel = offs[None, :] - np.arange(ATTN_BLOCK)[:, None]
    near = np.abs(rel) <= WINDOW
    tabs = []
    for variant in range(3):
        ok = near.copy()
        if variant == 0:
            ok[:, :ATTN_BLOCK] = False
        if variant == 2:
            ok[:, 2 * ATTN_BLOCK:] = False
        tabs.append(np.where(ok, 0.0, NEG_INF).astype(np.float32))
    return np.stack(tabs)


def _attention(bound, sink, q, k, v, kx, vx, *, batch, seq, ctx_len, local):
    nblk = seq // ATTN_BLOCK
    nq = ATTN_HEADS * HEAD_DIM
    nk = ATTN_KV_HEADS * HEAD_DIM
    qb = ATTN_QB if local else 1
    nstep = nblk // qb
    q_spec = pl.BlockSpec((qb * ATTN_BLOCK, nq), lambda b, i: (b * nstep + i, 0))
    x_spec = pl.BlockSpec((ctx_len, nk), lambda b, i: (b, 0))
    smem = pl.BlockSpec(memory_space=pltpu.SMEM)
    if local:
        local_mask = _attn_mask_table()
        mask = np.concatenate([local_mask, np.zeros((3, ATTN_BLOCK, ctx_len), np.float32)], axis=2)
        mask = jnp.asarray(np.tile(mask, (1, ATTN_GROUP, 1)))
        prev = pl.BlockSpec((ATTN_BLOCK, nk), lambda b, i: (b * nblk + jnp.maximum(qb * i - 1, 0), 0))
        cur = pl.BlockSpec((qb * ATTN_BLOCK, nk), lambda b, i: (b * nstep + i, 0))
        nxt = pl.BlockSpec((ATTN_BLOCK, nk), lambda b, i: (b * nblk + jnp.minimum(qb * (i + 1), nblk - 1), 0))

        def mask_spec(j):
            def index(b, i):
                g = qb * i + j
                return (jnp.where(g == 0, 0, jnp.where(g == nblk - 1, 2, 1)), 0, 0)
            return pl.BlockSpec((None,) + mask.shape[1:], index)

        in_specs = ([smem, smem, q_spec, prev, cur, nxt, prev, cur, nxt, x_spec, x_spec]
                    + [mask_spec(j) for j in range(qb)])
        args = (bound, sink, q, k, k, k, v, v, v, kx, vx) + (mask,) * qb
    else:
        in_specs = [smem, smem, q_spec, x_spec, x_spec]
        args = (bound, sink, q, kx, vx)
    return pl.pallas_call(
        functools.partial(_attn_kernel, local=local),
        grid=(batch, nstep),
        in_specs=in_specs,
        out_specs=q_spec,
        out_shape=jax.ShapeDtypeStruct(q.shape, BF16),
        compiler_params=_params("parallel", "parallel"),
        name="attn_local" if local else "attn_ctx",
    )(*args)


def _pool_kernel(xp_ref, x_ref, xn_ref, gain_ref, sh_ref, sc_ref, g_ref, w_ref, ls_ref, o_ref,
                 h_ref, pair_ref, quad_ref, oct_ref, *, tiles_per_seq, seq):
    tm, d = x_ref.shape
    it = pl.program_id(0) % tiles_per_seq
    gain, sh, sc = gain_ref[...], sh_ref[...], sc_ref[...]
    hp = jnp.where(it > 0, _prenorm(xp_ref[...], gain, sh, sc), 0.0)
    hn = jnp.where(it < tiles_per_seq - 1, _prenorm(xn_ref[...], gain, sh, sc), 0.0)
    x = x_ref[...]
    base = POOL_HALO
    h_ref[0:base, :] = hp
    h_ref[base:base + tm, :] = _prenorm(x, gain, sh, sc)
    h_ref[base + tm:base + tm + POOL_HALO, :] = hn
    h_ref[base + tm + POOL_HALO:, :] = jnp.zeros((POOL_PAD, d), F32)
    pos = it * tm + lax.broadcasted_iota(jnp.int32, (tm, 1), 0)
    gd = d // len(POOL_WINDOWS)
    n_pair, n_quad, n_oct = pair_ref.shape[0], quad_ref.shape[0], oct_ref.shape[0]
    for gi, win in enumerate(POOL_WINDOWS):
        cols = slice(gi * gd, (gi + 1) * gd)
        half = win // 2
        if win >= 4:
            pair_ref[:, cols] = h_ref[0:n_pair, cols] + h_ref[1:n_pair + 1, cols]
        if win >= 8:
            quad_ref[:, cols] = pair_ref[0:n_quad, cols] + pair_ref[2:n_quad + 2, cols]
        if win == 2:
            tot = h_ref[base - 1:base - 1 + tm, cols] + h_ref[base:base + tm, cols]
        elif win == 4:
            tot = pair_ref[base - 2:base - 2 + tm, cols] + pair_ref[base:base + tm, cols]
        elif win == 8:
            tot = quad_ref[base - 4:base - 4 + tm, cols] + quad_ref[base:base + tm, cols]
        else:
            oct_ref[:, cols] = quad_ref[0:n_oct, cols] + quad_ref[4:n_oct + 4, cols]
            tot = oct_ref[0:tm, cols] + oct_ref[base:base + tm, cols]
        cnt = jnp.minimum(pos + half, seq) - jnp.maximum(pos - half, 0)
        pooled = tot * (1.0 / cnt.astype(F32)) - h_ref[base:base + tm, cols]
        y = jnp.dot(pooled.astype(BF16), w_ref[gi], preferred_element_type=F32)
        o_ref[:, cols] = x[:, cols] + g_ref[:, cols] * (y * ls_ref[:, cols])


def _pool_mixer(x, gain, mods, row_fn, w, layer_scale, *, tm, seq):
    m, d = x.shape
    sh, sc, g = mods
    tiles_per_seq = seq // tm
    hb = tm // POOL_HALO
    nhalo = m // POOL_HALO
    return pl.pallas_call(
        functools.partial(_pool_kernel, tiles_per_seq=tiles_per_seq, seq=seq),
        grid=(m // tm,),
        in_specs=[
            pl.BlockSpec((POOL_HALO, d), lambda i: (jnp.maximum(i * hb - 1, 0), 0)),
            pl.BlockSpec((tm, d), lambda i: (i, 0)),
            pl.BlockSpec((POOL_HALO, d), lambda i: (jnp.minimum((i + 1) * hb, nhalo - 1), 0)),
            _resident((1, d)),
            _row_vec_spec(d, row_fn),
            _row_vec_spec(d, row_fn),
            _row_vec_spec(d, row_fn),
            _resident(w.shape),
            _resident((1, d)),
        ],
        out_specs=pl.BlockSpec((tm, d), lambda i: (i, 0)),
        out_shape=jax.ShapeDtypeStruct((m, d), F32),
        scratch_shapes=[pltpu.VMEM((tm + 2 * POOL_HALO + POOL_PAD, d), F32),
                        pltpu.VMEM((tm + 3 * POOL_HALO, d), F32),
                        pltpu.VMEM((tm + 2 * POOL_HALO, d), F32),
                        pltpu.VMEM((tm + POOL_HALO, d), F32)],
        compiler_params=_params("parallel"),
        name="pool_mixer",
    )(x, x, x, gain, sh, sc, g, w, layer_scale)


RET_TN = 512
RET_QK_COLS = 2 * RET_HEADS * RET_DK
RET_V_COLS = RET_HEADS * RET_DV
RET_QKV_COLS = RET_QK_COLS + RET_V_COLS


def _retproj_kernel(x_ref, gain_ref, sh_ref, sc_ref, w_ref, cos_ref, sin_ref, o_ref, *, rotate):
    half = RET_DK // 2
    q_cols = RET_HEADS * RET_DK
    for r0 in range(0, x_ref.shape[0], ROW_SUB):
        rows = slice(r0, min(r0 + ROW_SUB, x_ref.shape[0]))
        h = _prenorm(x_ref[rows, :], gain_ref[...], sh_ref[...], sc_ref[...]).astype(BF16)
        for c0 in range(0, RET_QKV_COLS, RET_TN):
            acc = jnp.dot(h, w_ref[:, c0:c0 + RET_TN], preferred_element_type=F32)
            if c0 >= RET_QK_COLS:
                o_ref[rows, c0:c0 + RET_TN] = acc.astype(BF16)
                continue
            scale = 1.0 if c0 < q_cols else RET_DK ** -0.5
            for hd in range(RET_TN // RET_DK):
                x1 = acc[:, hd * RET_DK:hd * RET_DK + half]
                x2 = acc[:, hd * RET_DK + half:(hd + 1) * RET_DK]
                if rotate:
                    cos, sin = cos_ref[rows, :], sin_ref[rows, :]
                    x1, x2 = x1 * cos - x2 * sin, x2 * cos + x1 * sin
                o_ref[rows, c0 + hd * RET_DK:c0 + hd * RET_DK + half] = (x1 * scale).astype(BF16)
                o_ref[rows, c0 + hd * RET_DK + half:c0 + (hd + 1) * RET_DK] = (x2 * scale).astype(BF16)


def _ret_proj(x, gain, mods, row_fn, w_in, slot, cos_t, sin_t, *, rotate, tm, tiles_per_seq):
    m, d = x.shape
    n = RET_QKV_COLS
    sh, sc = mods
    tab_spec = pl.BlockSpec((tm, RET_DK // 2), lambda i: (i % tiles_per_seq, 0))
    return pl.pallas_call(
        functools.partial(_retproj_kernel, rotate=rotate),
        grid=(m // tm,),
        in_specs=[
            pl.BlockSpec((tm, d), lambda i: (i, 0)),
            _resident((1, d)),
            _row_vec_spec(d, row_fn),
            _row_vec_spec(d, row_fn),
            _stacked(w_in, slot, col_block=0, cols=n),
            tab_spec,
            tab_spec,
        ],
        out_specs=pl.BlockSpec((tm, n), lambda i: (i, 0)),
        out_shape=jax.ShapeDtypeStruct((m, n), BF16),
        compiler_params=_params("parallel"),
        name="ret_proj_rot" if rotate else "ret_proj",
    )(x, gain, sh, sc, w_in, cos_t, sin_t)


def _ret_kernel(qf_ref, kf_ref, vf_ref, qb_ref, kb_ref, vb_ref, intra_ref, qdec_ref, kdec_ref, cdec_ref,
                *rest, sub, zero_init):
    if zero_init:
        yf_ref, yb_ref, sout_ref, state_ref = rest
    else:
        s0_ref, yf_ref, yb_ref, sout_ref, state_ref = rest
    c = pl.program_id(2)
    hd = pl.program_id(1)

    @pl.when(c == 0)
    def _():
        state_ref[...] = jnp.zeros_like(state_ref) if zero_init else s0_ref[...]

    n_sub = qf_ref.shape[0] // sub
    refs = ((qf_ref, kf_ref, vf_ref, yf_ref), (qb_ref, kb_ref, vb_ref, yb_ref))
    steps = []
    for j in range(n_sub):
        steps.append((0, slice(j * sub, (j + 1) * sub)))
        steps.append((1, slice((n_sub - 1 - j) * sub, (n_sub - j) * sub)))

    decayed = []
    for d, rows in steps:
        q_ref, k_ref, _, _ = refs[d]
        sc = lax.dot_general(q_ref[rows, :], k_ref[rows, :], (((1,), (1,)), ((), ())), preferred_element_type=F32)
        decayed.append((sc * intra_ref[d]).astype(BF16))

    for (d, rows), sc in zip(steps, decayed):
        q_ref, k_ref, v_ref, y_ref = refs[d]
        q, k, v = q_ref[rows, :], k_ref[rows, :], v_ref[rows, :]
        s = state_ref[d]
        y = (jnp.dot(sc, v, preferred_element_type=F32)
             + jnp.dot(q, s.astype(BF16), preferred_element_type=F32) * qdec_ref[d])
        kd = (k.astype(F32) * kdec_ref[d]).astype(BF16)
        state_ref[d] = s * cdec_ref[d * RET_HEADS + hd] + lax.dot_general(
            kd, v, (((0,), (0,)), ((), ())), preferred_element_type=F32)
        y_ref[rows, :] = y.astype(BF16)

    @pl.when(c == pl.num_programs(2) - 1)
    def _():
        sout_ref[...] = state_ref[...]


def _ret_tables(chunk):
    heads = np.arange(RET_HEADS, dtype=np.float64)
    log_g = [np.log1p(-np.exp2(-5.0 - heads)), np.log1p(-np.exp2(-5.0 - RET_BWD_OFFSET - heads))]
    pos = np.arange(chunk, dtype=np.float64)
    diff = pos[:, None] - pos[None, :]
    intra, qdec, kdec, cdec = [], [], [], []
    for d, lg in enumerate(log_g):
        lg3 = lg[:, None, None]
        if d == 0:
            intra.append(np.where(diff >= 0, np.exp(np.maximum(diff, 0.0)[None] * lg3), 0.0))
            qdec.append(np.exp((pos[None, :] + 1.0) * lg[:, None]))
            kdec.append(np.exp((chunk - 1.0 - pos)[None, :] * lg[:, None]))
        else:
            intra.append(np.where(diff <= 0, np.exp(np.maximum(-diff, 0.0)[None] * lg3), 0.0))
            qdec.append(np.exp((chunk - pos)[None, :] * lg[:, None]))
            kdec.append(np.exp(pos[None, :] * lg[:, None]))
        cdec.append(np.exp(chunk * lg))
    intra = np.stack(intra).astype(np.float32)
    qdec = np.stack(qdec).astype(np.float32)[..., None]
    kdec = np.stack(kdec).astype(np.float32)[..., None]
    cdec = np.stack(cdec).astype(np.float32).reshape(-1)
    return jnp.asarray(intra), jnp.asarray(qdec), jnp.asarray(kdec), jnp.asarray(cdec)


def _retention(proj, s0, *, batch, seq, block):
    sub = min(block, RET_SUB)
    intra, qdec, kdec, cdec = _ret_tables(sub)
    n = seq // block
    kq = RET_HEADS
    kv = RET_QK_COLS // RET_DV

    def fwd(off, width):
        return pl.BlockSpec((block, width), lambda b, h, c: (b * n + c, off + h))

    def bwd(off, width):
        return pl.BlockSpec((block, width), lambda b, h, c: (b * n + n - 1 - c, off + h))

    tab = lambda a: pl.BlockSpec((2, None) + a.shape[2:], lambda b, h, c: (0, h) + (0,) * (a.ndim - 2))
    state_spec = pl.BlockSpec((None, None, 2, RET_DK, RET_DV), lambda b, h, c: (b, h, 0, 0, 0))
    y_shape = jax.ShapeDtypeStruct((batch * seq, RET_V_COLS), BF16)
    return pl.pallas_call(
        functools.partial(_ret_kernel, sub=sub, zero_init=s0 is None),
        grid=(batch, RET_HEADS, n),
        in_specs=[
            fwd(0, RET_DK), fwd(kq, RET_DK), fwd(kv, RET_DV),
            bwd(0, RET_DK), bwd(kq, RET_DK), bwd(kv, RET_DV),
            tab(intra), tab(qdec), tab(kdec),
            pl.BlockSpec(memory_space=pltpu.SMEM),
        ] + ([] if s0 is None else [state_spec]),
        out_specs=[
            pl.BlockSpec((block, RET_DV), lambda b, h, c: (b * n + c, h)),
            pl.BlockSpec((block, RET_DV), lambda b, h, c: (b * n + n - 1 - c, h)),
            state_spec,
        ],
        out_shape=[y_shape, y_shape, jax.ShapeDtypeStruct((batch, RET_HEADS, 2, RET_DK, RET_DV), F32)],
        scratch_shapes=[pltpu.VMEM((2, RET_DK, RET_DV), F32)],
        compiler_params=_params("parallel", "parallel", "arbitrary"),
        name="retention",
    )(*((proj,) * 6 + (intra, qdec, kdec, cdec) + (() if s0 is None else (s0,))))


def _rms_unit(y):
    y = y.astype(F32)
    return y * lax.rsqrt(jnp.mean(y * y, axis=-1, keepdims=True) + NORM_EPS)


def _retout_kernel(x_ref, yf_ref, yb_ref, gain_ref, sh_ref, sc_ref, g_ref, wg_ref, wo_ref, o_ref, a_ref):
    hv = RET_V_COLS
    for r0 in range(0, x_ref.shape[0], ROW_SUB):
        rows = slice(r0, min(r0 + ROW_SUB, x_ref.shape[0]))
        x = x_ref[rows, :]
        h = _prenorm(x, gain_ref[...], sh_ref[...], sc_ref[...]).astype(BF16)
        for c0 in range(0, hv, RET_DV):
            cols = slice(c0, c0 + RET_DV)
            gf = jnp.dot(h, wg_ref[:, c0:c0 + RET_DV], preferred_element_type=F32)
            gb = jnp.dot(h, wg_ref[:, hv + c0:hv + c0 + RET_DV], preferred_element_type=F32)
            y = _silu(gf) * _rms_unit(yf_ref[rows, cols]) + _silu(gb) * _rms_unit(yb_ref[rows, cols])
            a_ref[rows, cols] = y.astype(BF16)
        out = jnp.dot(a_ref[rows, :], wo_ref[...], preferred_element_type=F32)
        o_ref[rows, :] = x + g_ref[...] * out


def _ret_out(x, yf, yb, gain, mods, row_fn, w_in, w_o, slot, *, tm):
    m, d = x.shape
    hv = RET_V_COLS
    sh, sc, g = mods
    return pl.pallas_call(
        _retout_kernel,
        grid=(m // tm,),
        in_specs=[
            pl.BlockSpec((tm, d), lambda i: (i, 0)),
            pl.BlockSpec((tm, hv), lambda i: (i, 0)),
            pl.BlockSpec((tm, hv), lambda i: (i, 0)),
            _resident((1, d)),
            _row_vec_spec(d, row_fn),
            _row_vec_spec(d, row_fn),
            _row_vec_spec(d, row_fn),
            _stacked(w_in, slot, col_block=1, cols=2 * hv),
            _stacked(w_o, slot),
        ],
        out_specs=pl.BlockSpec((tm, d), lambda i: (i, 0)),
        out_shape=jax.ShapeDtypeStruct((m, d), F32),
        scratch_shapes=[pltpu.VMEM((tm, hv), BF16)],
        compiler_params=_params("parallel"),
        name="ret_out",
    )(x, yf, yb, gain, sh, sc, g, w_in, w_o)


FFN_CHUNK = 256


def _ffn_kernel(x_ref, gain_ref, sh_ref, sc_ref, g_ref, wg_ref, wu_ref, wd_ref, *rest, mixer_proj):
    if mixer_proj:
        y_ref, wo_ref, gm_ref, o_ref, a_ref = rest
    else:
        o_ref, a_ref = rest
    for r0 in range(0, x_ref.shape[0], ROW_SUB):
        rows = slice(r0, min(r0 + ROW_SUB, x_ref.shape[0]))
        x = x_ref[rows, :]
        if mixer_proj:
            x = x + gm_ref[...] * jnp.dot(y_ref[rows, :], wo_ref[...], preferred_element_type=F32)
        h = _prenorm(x, gain_ref[...], sh_ref[...], sc_ref[...]).astype(BF16)
        for c0 in range(0, wg_ref.shape[1], FFN_CHUNK):
            cols = slice(c0, c0 + FFN_CHUNK)
            gate = jnp.dot(h, wg_ref[:, cols], preferred_element_type=F32)
            up = jnp.dot(h, wu_ref[:, cols], preferred_element_type=F32)
            a_ref[rows, cols] = (_silu(gate) * up).astype(BF16)
        out = jnp.dot(a_ref[rows, :], wd_ref[...], preferred_element_type=F32)
        o_ref[rows, :] = x + g_ref[...] * out


def _ffn(x, gain, mods, row_fn, wg, wu, wd, layer, *, tm, mixer_proj=None):
    m, d = x.shape
    sh, sc, g = mods
    in_specs = [
        pl.BlockSpec((tm, d), lambda i: (i, 0)),
        _resident((1, d)),
        _row_vec_spec(d, row_fn),
        _row_vec_spec(d, row_fn),
        _row_vec_spec(d, row_fn),
        _stacked(wg, layer),
        _stacked(wu, layer),
        _stacked(wd, layer),
    ]
    args = [x, gain, sh, sc, g, wg, wu, wd]
    if mixer_proj is not None:
        y, w_o, slot, gm = mixer_proj
        in_specs += [
            pl.BlockSpec((tm, y.shape[1]), lambda i: (i, 0)),
            _stacked(w_o, slot),
            _row_vec_spec(d, row_fn),
        ]
        args += [y, w_o, gm]
    return pl.pallas_call(
        functools.partial(_ffn_kernel, mixer_proj=mixer_proj is not None),
        grid=(m // tm,),
        in_specs=in_specs,
        out_specs=pl.BlockSpec((tm, d), lambda i: (i, 0)),
        out_shape=jax.ShapeDtypeStruct((m, d), F32),
        scratch_shapes=[pltpu.VMEM((tm, wg.shape[2]), BF16)],
        compiler_params=_params("parallel"),
        name="ffn_proj" if mixer_proj is not None else "ffn",
    )(*args)


def _axial_tables(n_tokens):
    rows = n_tokens // GRID_W
    row = np.repeat(np.arange(rows, dtype=np.float32), GRID_W)
    col = np.tile(np.arange(GRID_W, dtype=np.float32), rows)
    n_freq = HEAD_DIM // 4
    inv = jnp.asarray(ROPE_BASE, F32) ** (-jnp.arange(n_freq, dtype=F32) / n_freq)
    ang = jnp.concatenate([jnp.asarray(row)[:, None] * inv, jnp.asarray(col)[:, None] * inv], axis=-1)
    cos, sin = jnp.cos(ang), jnp.sin(ang)
    return jnp.tile(cos, (1, 4)), jnp.concatenate([-sin, -sin, sin, sin], axis=-1)


def _retention_rot_tables(n_tokens):
    inv = jnp.asarray(ROPE_BASE, F32) ** (-jnp.linspace(0.0, 1.0, RET_DK // 2, dtype=F32))
    ang = jnp.arange(n_tokens, dtype=F32)[:, None] * inv
    return jnp.cos(ang), jnp.sin(ang)


def kernel(x, c, ctx, c_ctx, ada_w, ada_b, norm_mix, norm_ffn, attn_w_qkv, attn_w_o, attn_q_norm, attn_k_norm,
           attn_sink, pool_w, pool_scale, ret_w_in, ret_w_o, ffn_w_gate, ffn_w_up, ffn_w_down):
    batch, seq, d = x.shape
    ctx_len = ctx.shape[1]
    mod_rows = 16
    assert batch < mod_rows and seq % 1024 == 0 and ctx_len % 256 == 0 and seq % GRID_W == 0

    cvec = jnp.zeros((mod_rows, d), F32).at[:batch].set(c).at[batch].set(c_ctx)
    mods = _ada_mods(cvec, ada_w, ada_b)
    mods = mods.reshape(DEPTH, mod_rows, 6, 1, d)

    x_lat = x.reshape(batch * seq, d)
    x_ctx = ctx.reshape(batch * ctx_len, d)

    tm_lat, tm_ctx, tm_pool = 1024, 256, 512
    lat_row = lambda i: i // (seq // tm_lat)
    ctx_row = lambda i: batch

    attn_cos, attn_sin = _axial_tables(seq)
    ret_cos, ret_sin = _retention_rot_tables(seq)
    head_of_pair, _ = _pair_layout()
    same_head = (head_of_pair[:, None] == head_of_pair[None, :]).astype(np.float32)
    bd = jnp.asarray(np.concatenate([same_head, same_head], axis=0), BF16)
    half = HEAD_DIM // 2

    def pair_gain(g):
        return jnp.broadcast_to(g.reshape(2, 1, half), (2, 2, half)).reshape(1, LANES)

    nqk = (ATTN_HEADS + ATTN_KV_HEADS) * HEAD_DIM
    slots = attn_w_qkv.shape[0]
    w_qk = attn_w_qkv[:, :, :nqk].reshape(slots, d, nqk // LANES, 2, 2, half)
    w_qk = jnp.swapaxes(w_qk, 3, 4).reshape(slots, d, nqk)
    attn_qkv_w = jnp.concatenate([w_qk, attn_w_qkv[:, :, nqk:]], axis=-1).astype(BF16)
    attn_o_w = attn_w_o.astype(BF16)
    ret_in_w = ret_w_in.astype(BF16)
    ret_o_w = ret_w_o.astype(BF16)
    pool_w_bf = pool_w.astype(BF16)
    wg_all, wu_all, wd_all = ffn_w_gate.astype(BF16), ffn_w_up.astype(BF16), ffn_w_down.astype(BF16)

    for i in range(DEPTH):
        kind, slot = i % N_MIXERS, i // N_MIXERS
        need_ctx_out = i < DEPTH - 1
        mod = [mods[i, :, k] for k in range(6)]
        sh_m, sc_m, g_m, sh_f, sc_f, g_f = mod
        gain_m = norm_mix[i].reshape(1, d)
        gain_f = norm_ffn[i].reshape(1, d)
        proj_lat = proj_ctx = None

        if kind == 0:
            qg = pair_gain(attn_q_norm[slot])
            kg = pair_gain(attn_k_norm[slot])
            sink = attn_sink[slot].astype(F32)
            bound = (1.02 * HEAD_DIM ** 0.5 * jnp.max(jnp.abs(attn_q_norm[slot]))
                     * jnp.max(jnp.abs(attn_k_norm[slot]))).astype(F32).reshape(1)
            q_c, k_c, v_c = _attn_qkv(x_ctx, gain_m, (sh_m, sc_m), ctx_row, attn_qkv_w, slot, qg, kg, attn_cos,
                                      attn_sin, bd, rotate=False, tm=tm_ctx, tiles_per_seq=1)
            q_l, k_l, v_l = _attn_qkv(x_lat, gain_m, (sh_m, sc_m), lat_row, attn_qkv_w, slot, qg, kg, attn_cos,
                                      attn_sin, bd, rotate=True, tm=tm_lat, tiles_per_seq=seq // tm_lat)
            a_l = _attention(bound, sink, q_l, k_l, v_l, k_c, v_c, batch=batch, seq=seq, ctx_len=ctx_len, local=True)
            proj_lat = (a_l, attn_o_w, slot, g_m)
            if need_ctx_out:
                a_c = _attention(bound, sink, q_c, None, None, k_c, v_c, batch=batch, seq=ctx_len, ctx_len=ctx_len,
                                 local=False)
                proj_ctx = (a_c, attn_o_w, slot, g_m)
        elif kind == 1:
            w_p = pool_w_bf[slot]
            ls = pool_scale[slot].reshape(1, d)
            pool_row = lambda i: i // (seq // tm_pool)
            x_lat = _pool_mixer(x_lat, gain_m, (sh_m, sc_m, g_m), pool_row, w_p, ls, tm=tm_pool, seq=seq)
            if need_ctx_out:
                x_ctx = _pool_mixer(x_ctx, gain_m, (sh_m, sc_m, g_m), ctx_row, w_p, ls, tm=tm_ctx, seq=ctx_len)
        else:
            p_c = _ret_proj(x_ctx, gain_m, (sh_m, sc_m), ctx_row, ret_in_w, slot, ret_cos, ret_sin,
                            rotate=False, tm=tm_ctx, tiles_per_seq=1)
            yc_f, yc_b, s_c = _retention(p_c, None, batch=batch, seq=ctx_len, block=min(ctx_len, RET_BLOCK))
            p_l = _ret_proj(x_lat, gain_m, (sh_m, sc_m), lat_row, ret_in_w, slot, ret_cos, ret_sin,
                            rotate=True, tm=tm_lat, tiles_per_seq=seq // tm_lat)
            yl_f, yl_b, _ = _retention(p_l, s_c, batch=batch, seq=seq, block=RET_BLOCK)
            x_lat = _ret_out(x_lat, yl_f, yl_b, gain_m, (sh_m, sc_m, g_m), lat_row, ret_in_w, ret_o_w, slot,
                             tm=tm_lat)
            if need_ctx_out:
                x_ctx = _ret_out(x_ctx, yc_f, yc_b, gain_m, (sh_m, sc_m, g_m), ctx_row, ret_in_w, ret_o_w, slot,
                                 tm=tm_ctx)

        x_lat = _ffn(x_lat, gain_f, (sh_f, sc_f, g_f), lat_row, wg_all, wu_all, wd_all, i, tm=tm_lat,
                     mixer_proj=proj_lat)
        if need_ctx_out:
            x_ctx = _ffn(x_ctx, gain_f, (sh_f, sc_f, g_f), ctx_row, wg_all, wu_all, wd_all, i, tm=tm_ctx,
                         mixer_proj=proj_ctx)

    return x_lat.reshape(batch, seq, d)
```

```python
import functools

import jax
import jax.numpy as jnp
import numpy as np
from jax import lax
from jax.experimental import pallas as pl
from jax.experimental.pallas import tpu as pltpu

F32 = jnp.float32
BF16 = jnp.bfloat16

DEPTH = 4
GRID_W = 64
N_MIXERS = 3
ATTN_HEADS = 16
ATTN_KV_HEADS = 4
ATTN_GROUP = ATTN_HEADS // ATTN_KV_HEADS
HEAD_DIM = 64
WINDOW = 128
ATTN_BLOCK = 128
ROPE_BASE = 10000.0
NEG_INF = -1e30
POOL_WINDOWS = (2, 4, 8, 16)
POOL_HALO = 8
POOL_PAD = 16
RET_HEADS = 4
RET_DK = 256
RET_DV = 512
RET_SUB = 256
RET_BLOCK = 2048
RET_BWD_OFFSET = 0.5
NORM_EPS = 1e-6

VMEM_LIMIT_BYTES = 56 * 1024 * 1024
LANES = 128
ROW_SUB = 512


def _params(*sem):
    return pltpu.CompilerParams(dimension_semantics=sem, vmem_limit_bytes=VMEM_LIMIT_BYTES)


def _silu(x):
    return x * jax.nn.sigmoid(x)


def _prenorm(x, gain, shift, scale):
    y = x * lax.rsqrt(jnp.mean(x * x, axis=-1, keepdims=True) + NORM_EPS)
    return (y * gain) * (1.0 + scale) + shift


def _resident(shape):
    nd = len(shape)
    return pl.BlockSpec(shape, lambda *_: (0,) * nd)


def _stacked(arr, index, col_block=0, cols=None):
    shape = arr.shape[1:] if cols is None else arr.shape[1:-1] + (cols,)
    nd = len(shape)
    return pl.BlockSpec((None,) + shape, lambda *_: (index,) + (0,) * (nd - 1) + (col_block,),
                        pipeline_mode=pl.Buffered(1))


def _ada_kernel(c_ref, w_ref, b_ref, o_ref):
    cond = _silu(c_ref[...])
    o_ref[...] = jnp.dot(cond.astype(BF16), w_ref[...].astype(BF16), preferred_element_type=F32) + b_ref[...]


def _ada_mods(cvec, ada_w, ada_b):
    depth, d, n = ada_w.shape
    rows = cvec.shape[0]
    tn = 1536
    return pl.pallas_call(
        _ada_kernel,
        grid=(depth, n // tn),
        in_specs=[
            pl.BlockSpec((rows, d), lambda l, j: (0, 0)),
            pl.BlockSpec((None, d, tn), lambda l, j: (l, 0, j)),
            pl.BlockSpec((None, 1, tn), lambda l, j: (l, 0, j)),
        ],
        out_specs=pl.BlockSpec((None, rows, tn), lambda l, j: (l, 0, j)),
        out_shape=jax.ShapeDtypeStruct((depth, rows, n), F32),
        compiler_params=_params("parallel", "parallel"),
        name="ada_mods",
    )(cvec, ada_w, ada_b.reshape(depth, 1, n))


def _row_vec_spec(d, row_fn):
    return pl.BlockSpec((None, 1, d), lambda i, *_: (row_fn(i), 0, 0))


def _qkv_kernel(x_ref, gain_ref, sh_ref, sc_ref, w_ref, qg_ref, kg_ref, cos_ref, sin_ref, bd_ref,
                q_ref, k_ref, v_ref, *, rotate):
    bd2 = bd_ref[...]
    qg = qg_ref[...]
    kg = kg_ref[...]
    nq = ATTN_HEADS * HEAD_DIM
    nk = ATTN_KV_HEADS * HEAD_DIM
    width = 2 * LANES

    def head_norm(xg, g):
        x2 = xg * xg
        hi = x2.astype(BF16)
        lo = (x2 - hi.astype(F32)).astype(BF16)
        ssq = jnp.dot(jnp.concatenate([hi, lo], axis=1), bd2, preferred_element_type=F32)
        return (xg * lax.rsqrt(ssq * (1.0 / HEAD_DIM) + NORM_EPS)) * g

    for r0 in range(0, x_ref.shape[0], ROW_SUB):
        rows = slice(r0, min(r0 + ROW_SUB, x_ref.shape[0]))
        h = _prenorm(x_ref[rows, :], gain_ref[...], sh_ref[...], sc_ref[...]).astype(BF16)

        def project(c0):
            return jnp.dot(h, w_ref[:, c0:c0 + width], preferred_element_type=F32)

        nxt = project(0)
        for j in range((nq + nk) // LANES):
            if j % 2 == 0:
                wide = nxt
                nxt = project((j + 2) * LANES)
            acc = wide[:, (j % 2) * LANES:(j % 2 + 1) * LANES]
            is_q = j < nq // LANES
            y = head_norm(acc, qg) * Q_SCALE if is_q else head_norm(acc, kg)
            if rotate:
                y = y * cos_ref[rows, :] + pltpu.roll(y, LANES // 2, 1) * sin_ref[rows, :]
            if is_q:
                q_ref[rows, j * LANES:(j + 1) * LANES] = y.astype(BF16)
            else:
                k_ref[rows, j * LANES - nq:(j + 1) * LANES - nq] = y.astype(BF16)
        v_ref[rows, :] = nxt.astype(BF16)


def _pair_layout():
    lane = np.arange(LANES)
    head_of_pair = (lane // (HEAD_DIM // 2)) % 2
    dim = lane % (HEAD_DIM // 2) + (HEAD_DIM // 2) * (lane // HEAD_DIM)
    return head_of_pair, dim


def _attn_qkv(x, gain, mods, row_fn, w, slot, qg, kg, cos_t, sin_t, bd, *, rotate, tm, tiles_per_seq):
    m, d = x.shape
    nq = ATTN_HEADS * HEAD_DIM
    nk = ATTN_KV_HEADS * HEAD_DIM
    sh, sc = mods
    tab_spec = pl.BlockSpec((tm, LANES), lambda i: (i % tiles_per_seq, 0))
    return pl.pallas_call(
        functools.partial(_qkv_kernel, rotate=rotate),
        grid=(m // tm,),
        in_specs=[
            pl.BlockSpec((tm, d), lambda i: (i, 0)),
            _resident((1, d)),
            _row_vec_spec(d, row_fn),
            _row_vec_spec(d, row_fn),
            _stacked(w, slot),
            _resident((1, LANES)),
            _resident((1, LANES)),
            tab_spec,
            tab_spec,
            _resident((2 * LANES, LANES)),
        ],
        out_specs=[
            pl.BlockSpec((tm, nq), lambda i: (i, 0)),
            pl.BlockSpec((tm, nk), lambda i: (i, 0)),
            pl.BlockSpec((tm, nk), lambda i: (i, 0)),
        ],
        out_shape=[
            jax.ShapeDtypeStruct((m, nq), BF16),
            jax.ShapeDtypeStruct((m, nk), BF16),
            jax.ShapeDtypeStruct((m, nk), BF16),
        ],
        compiler_params=_params("parallel"),
        name="attn_qkv_rot" if rotate else "attn_qkv",
    )(x, gain, sh, sc, w, qg, kg, cos_t, sin_t, bd)


ATTN_SHIFT_LIMIT = 30.0
LOG2E = 1.4426950408889634
Q_SCALE = HEAD_DIM ** -0.5 * LOG2E
ATTN_QB = 4


def _attn_heads(bound_ref, sink_ref, q_ref, q_rows, k_parts, v_parts, mask_ref, o_ref, *, use_bound):
    blk = q_rows.stop - q_rows.start
    lane = lax.broadcasted_iota(jnp.int32, (1, LANES), 1)
    low = lane < HEAD_DIM
    pair_bit = (lane // (HEAD_DIM // 2)) % 2
    one = jnp.ones((), BF16)
    zero = jnp.zeros((), BF16)

    def rows(parts, cols):
        pieces = [r[rs, cols] for r, rs in parts]
        return pieces[0] if len(pieces) == 1 else jnp.concatenate(pieces, axis=0)

    def scores(h):
        par = h % 2
        cols = slice((h // 2) * LANES, (h // 2 + 1) * LANES)
        own_qk = pair_bit == par
        k2 = rows(k_parts, cols)
        q_parts = []
        for g in range(ATTN_GROUP):
            hd = h * ATTN_GROUP + g
            qt = q_ref[q_rows, (hd // 2) * LANES:(hd // 2 + 1) * LANES]
            if hd % 2 != par:
                qt = pltpu.roll(qt, HEAD_DIM // 2 if par == 1 else LANES - HEAD_DIM // 2, 1)
            q_parts.append(jnp.where(own_qk, qt, zero))
        q2 = jnp.concatenate(q_parts, axis=0)
        return lax.dot_general(q2, k2, (((1,), (1,)), ((), ())), preferred_element_type=F32)

    def finish(h, s):
        par = h % 2
        cols = slice((h // 2) * LANES, (h // 2 + 1) * LANES)
        own_v = low if par == 0 else jnp.logical_not(low)
        v2 = jnp.where(own_v, rows(v_parts, cols), one)
        heads = [h * ATTN_GROUP + g for g in range(ATTN_GROUP)]
        if mask_ref is not None:
            b = ATTN_BLOCK
            s = jnp.concatenate([s[:, :b] + mask_ref[:, :b], s[:, b:2 * b],
                                 s[:, 2 * b:3 * b] + mask_ref[:, b:], s[:, 3 * b:]], axis=1)
        if use_bound:
            shifts = [jnp.maximum(bound_ref[0], sink_ref[hd]) for hd in heads]
            shift = jnp.concatenate([jnp.full((blk, 1), sh, F32) for sh in shifts], axis=0)
            es = [jnp.exp2(jnp.full((1, LANES), sink_ref[hd] - sh, F32)) for hd, sh in zip(heads, shifts)]
        else:
            sink = jnp.concatenate([jnp.full((blk, 1), sink_ref[hd], F32) for hd in heads], axis=0)
            shift = jnp.maximum(jnp.max(s, axis=1, keepdims=True), sink)
            es_all = jnp.exp2(sink - shift)
            es = [es_all[g * blk:(g + 1) * blk] for g in range(ATTN_GROUP)]
        p = jnp.exp2(s - shift).astype(BF16)
        oa = jnp.dot(p, v2, preferred_element_type=F32)
        res = []
        for g, hd in enumerate(heads):
            a = oa[g * blk:(g + 1) * blk]
            r = pltpu.roll(a, HEAD_DIM, 1)
            res.append(a / (r + es[g]) if hd % 2 == par else r / (a + es[g]))
        for t in range(ATTN_GROUP // 2):
            tile = jnp.where(low, res[2 * t], res[2 * t + 1])
            c0 = (heads[2 * t] // 2) * LANES
            o_ref[q_rows, c0:c0 + LANES] = tile.astype(BF16)

    s_next = scores(0)
    for h in range(ATTN_KV_HEADS):
        s = s_next
        if h + 1 < ATTN_KV_HEADS:
            s_next = scores(h + 1)
        finish(h, s)


def _attn_kernel(bound_ref, sink_ref, q_ref, *refs, local):
    full = slice(None)
    if local:
        kp_ref, kc_ref, kn_ref, vp_ref, vc_ref, vn_ref, kx_ref, vx_ref = refs[:8]
        mask_refs = refs[8:8 + ATTN_QB]
        o_ref = refs[8 + ATTN_QB]
        blocks = [slice(j * ATTN_BLOCK, (j + 1) * ATTN_BLOCK) for j in range(ATTN_QB)]
        k_blocks = [(kp_ref, full)] + [(kc_ref, rs) for rs in blocks] + [(kn_ref, full)]
        v_blocks = [(vp_ref, full)] + [(vc_ref, rs) for rs in blocks] + [(vn_ref, full)]
        work = [(blocks[j], k_blocks[j:j + 3] + [(kx_ref, full)], v_blocks[j:j + 3] + [(vx_ref, full)], mask_refs[j])
                for j in range(ATTN_QB)]
    else:
        kx_ref, vx_ref, o_ref = refs
        work = [(slice(0, q_ref.shape[0]), [(kx_ref, full)], [(vx_ref, full)], None)]
    use_bound = bound_ref[0] <= ATTN_SHIFT_LIMIT * LOG2E

    def run(flag):
        for q_rows, k_parts, v_parts, mask_ref in work:
            _attn_heads(bound_ref, sink_ref, q_ref, q_rows, k_parts, v_parts, mask_ref, o_ref, use_bound=flag)

    @pl.when(use_bound)
    def _():
        run(True)

    @pl.when(jnp.logical_not(use_bound))
    def _():
        run(False)


def _attn_mask_table():
    span = 3 * ATTN_BLOCK
    offs = np.arange(span) - ATTN_BLOCK
    rel = offs[None, :] - np.arange(ATTN_BLOCK)[:, None]
    near = np.abs(rel) <= WINDOW
    tabs = []
    for variant in range(3):
        ok = near.copy()
        if variant == 0:
            ok[:, :ATTN_BLOCK] = False
        if variant == 2:
            ok[:, 2 * ATTN_BLOCK:] = False
        tabs.append(np.where(ok, 0.0, NEG_INF).astype(np.float32))
    return np.stack(tabs)


def _attention(bound, sink, q, k, v, kx, vx, *, batch, seq, ctx_len, local):
    nblk = seq // ATTN_BLOCK
    nq = ATTN_HEADS * HEAD_DIM
    nk = ATTN_KV_HEADS * HEAD_DIM
    qb = ATTN_QB if local else 1
    nstep = nblk // qb
    q_spec = pl.BlockSpec((qb * ATTN_BLOCK, nq), lambda b, i: (b * nstep + i, 0))
    x_spec = pl.BlockSpec((ctx_len, nk), lambda b, i: (b, 0))
    smem = pl.BlockSpec(memory_space=pltpu.SMEM)
    if local:
        local_mask = _attn_mask_table()
        mask = np.concatenate([local_mask[:, :, :ATTN_BLOCK], local_mask[:, :, 2 * ATTN_BLOCK:]], axis=2)
        mask = jnp.asarray(np.tile(mask, (1, ATTN_GROUP, 1)))
        prev = pl.BlockSpec((ATTN_BLOCK, nk), lambda b, i: (b * nblk + jnp.maximum(qb * i - 1, 0), 0))
        cur = pl.BlockSpec((qb * ATTN_BLOCK, nk), lambda b, i: (b * nstep + i, 0))
        nxt = pl.BlockSpec((ATTN_BLOCK, nk), lambda b, i: (b * nblk + jnp.minimum(qb * (i + 1), nblk - 1), 0))

        def mask_spec(j):
            def index(b, i):
                g = qb * i + j
                return (jnp.where(g == 0, 0, jnp.where(g == nblk - 1, 2, 1)), 0, 0)
            return pl.BlockSpec((None,) + mask.shape[1:], index)

        in_specs = ([smem, smem, q_spec, prev, cur, nxt, prev, cur, nxt, x_spec, x_spec]
                    + [mask_spec(j) for j in range(qb)])
        args = (bound, sink, q, k, k, k, v, v, v, kx, vx) + (mask,) * qb
    else:
        in_specs = [smem, smem, q_spec, x_spec, x_spec]
        args = (bound, sink, q, kx, vx)
    return pl.pallas_call(
        functools.partial(_attn_kernel, local=local),
        grid=(batch, nstep),
        in_specs=in_specs,
        out_specs=q_spec,
        out_shape=jax.ShapeDtypeStruct(q.shape, BF16),
        compiler_params=_params("parallel", "parallel"),
        name="attn_local" if local else "attn_ctx",
    )(*args)


def _pool_kernel(xp_ref, x_ref, xn_ref, gain_ref, sh_ref, sc_ref, g_ref, w_ref, ls_ref, o_ref,
                 h_ref, pair_ref, quad_ref, oct_ref, *, tiles_per_seq, seq):
    tm, d = x_ref.shape
    it = pl.program_id(0) % tiles_per_seq
    gain, sh, sc = gain_ref[...], sh_ref[...], sc_ref[...]
    hp = jnp.where(it > 0, _prenorm(xp_ref[...], gain, sh, sc), 0.0)
    hn = jnp.where(it < tiles_per_seq - 1, _prenorm(xn_ref[...], gain, sh, sc), 0.0)
    x = x_ref[...]
    base = POOL_HALO
    h_ref[0:base, :] = hp
    h_ref[base:base + tm, :] = _prenorm(x, gain, sh, sc)
    h_ref[base + tm:base + tm + POOL_HALO, :] = hn
    h_ref[base + tm + POOL_HALO:, :] = jnp.zeros((POOL_PAD, d), F32)
    pos = it * tm + lax.broadcasted_iota(jnp.int32, (tm, 1), 0)
    gd = d // len(POOL_WINDOWS)
    n_pair, n_quad, n_oct = pair_ref.shape[0], quad_ref.shape[0], oct_ref.shape[0]
    for gi, win in enumerate(POOL_WINDOWS):
        cols = slice(gi * gd, (gi + 1) * gd)
        half = win // 2
        if win >= 4:
            pair_ref[:, cols] = h_ref[0:n_pair, cols] + h_ref[1:n_pair + 1, cols]
        if win >= 8:
            quad_ref[:, cols] = pair_ref[0:n_quad, cols] + pair_ref[2:n_quad + 2, cols]
        if win == 2:
            tot = h_ref[base - 1:base - 1 + tm, cols] + h_ref[base:base + tm, cols]
        elif win == 4:
            tot = pair_ref[base - 2:base - 2 + tm, cols] + pair_ref[base:base + tm, cols]
        elif win == 8:
            tot = quad_ref[base - 4:base - 4 + tm, cols] + quad_ref[base:base + tm, cols]
        else:
            oct_ref[:, cols] = quad_ref[0:n_oct, cols] + quad_ref[4:n_oct + 4, cols]
            tot = oct_ref[0:tm, cols] + oct_ref[base:base + tm, cols]
        cnt = jnp.minimum(pos + half, seq) - jnp.maximum(pos - half, 0)
        pooled = tot * (1.0 / cnt.astype(F32)) - h_ref[base:base + tm, cols]
        y = jnp.dot(pooled.astype(BF16), w_ref[gi], preferred_element_type=F32)
        o_ref[:, cols] = x[:, cols] + g_ref[:, cols] * (y * ls_ref[:, cols])


def _pool_mixer(x, gain, mods, row_fn, w, layer_scale, *, tm, seq):
    m, d = x.shape
    sh, sc, g = mods
    tiles_per_seq = seq // tm
    hb = tm // POOL_HALO
    nhalo = m // POOL_HALO
    return pl.pallas_call(
        functools.partial(_pool_kernel, tiles_per_seq=tiles_per_seq, seq=seq),
        grid=(m // tm,),
        in_specs=[
            pl.BlockSpec((POOL_HALO, d), lambda i: (jnp.maximum(i * hb - 1, 0), 0)),
            pl.BlockSpec((tm, d), lambda i: (i, 0)),
            pl.BlockSpec((POOL_HALO, d), lambda i: (jnp.minimum((i + 1) * hb, nhalo - 1), 0)),
            _resident((1, d)),
            _row_vec_spec(d, row_fn),
            _row_vec_spec(d, row_fn),
            _row_vec_spec(d, row_fn),
            _resident(w.shape),
            _resident((1, d)),
        ],
        out_specs=pl.BlockSpec((tm, d), lambda i: (i, 0)),
        out_shape=jax.ShapeDtypeStruct((m, d), F32),
        scratch_shapes=[pltpu.VMEM((tm + 2 * POOL_HALO + POOL_PAD, d), F32),
                        pltpu.VMEM((tm + 3 * POOL_HALO, d), F32),
                        pltpu.VMEM((tm + 2 * POOL_HALO, d), F32),
                        pltpu.VMEM((tm + POOL_HALO, d), F32)],
        compiler_params=_params("parallel"),
        name="pool_mixer",
    )(x, x, x, gain, sh, sc, g, w, layer_scale)


RET_TN = 512
RET_QK_COLS = 2 * RET_HEADS * RET_DK
RET_V_COLS = RET_HEADS * RET_DV
RET_QKV_COLS = RET_QK_COLS + RET_V_COLS


def _retproj_kernel(x_ref, gain_ref, sh_ref, sc_ref, w_ref, cos_ref, sin_ref, o_ref, *, rotate):
    half = RET_DK // 2
    q_cols = RET_HEADS * RET_DK
    for r0 in range(0, x_ref.shape[0], ROW_SUB):
        rows = slice(r0, min(r0 + ROW_SUB, x_ref.shape[0]))
        h = _prenorm(x_ref[rows, :], gain_ref[...], sh_ref[...], sc_ref[...]).astype(BF16)
        for c0 in range(0, RET_QKV_COLS, RET_TN):
            acc = jnp.dot(h, w_ref[:, c0:c0 + RET_TN], preferred_element_type=F32)
            if c0 >= RET_QK_COLS:
                o_ref[rows, c0:c0 + RET_TN] = acc.astype(BF16)
                continue
            scale = 1.0 if c0 < q_cols else RET_DK ** -0.5
            for hd in range(RET_TN // RET_DK):
                x1 = acc[:, hd * RET_DK:hd * RET_DK + half]
                x2 = acc[:, hd * RET_DK + half:(hd + 1) * RET_DK]
                if rotate:
                    cos, sin = cos_ref[rows, :], sin_ref[rows, :]
                    x1, x2 = x1 * cos - x2 * sin, x2 * cos + x1 * sin
                o_ref[rows, c0 + hd * RET_DK:c0 + hd * RET_DK + half] = (x1 * scale).astype(BF16)
                o_ref[rows, c0 + hd * RET_DK + half:c0 + (hd + 1) * RET_DK] = (x2 * scale).astype(BF16)


def _ret_proj(x, gain, mods, row_fn, w_in, slot, cos_t, sin_t, *, rotate, tm, tiles_per_seq):
    m, d = x.shape
    n = RET_QKV_COLS
    sh, sc = mods
    tab_spec = pl.BlockSpec((tm, RET_DK // 2), lambda i: (i % tiles_per_seq, 0))
    return pl.pallas_call(
        functools.partial(_retproj_kernel, rotate=rotate),
        grid=(m // tm,),
        in_specs=[
            pl.BlockSpec((tm, d), lambda i: (i, 0)),
            _resident((1, d)),
            _row_vec_spec(d, row_fn),
            _row_vec_spec(d, row_fn),
            _stacked(w_in, slot, col_block=0, cols=n),
            tab_spec,
            tab_spec,
        ],
        out_specs=pl.BlockSpec((tm, n), lambda i: (i, 0)),
        out_shape=jax.ShapeDtypeStruct((m, n), BF16),
        compiler_params=_params("parallel"),
        name="ret_proj_rot" if rotate else "ret_proj",
    )(x, gain, sh, sc, w_in, cos_t, sin_t)


def _ret_kernel(qf_ref, kf_ref, vf_ref, qb_ref, kb_ref, vb_ref, intra_ref, qdec_ref, kdec_ref, cdec_ref,
                *rest, sub, zero_init):
    if zero_init:
        yf_ref, yb_ref, sout_ref, state_ref = rest
    else:
        s0_ref, yf_ref, yb_ref, sout_ref, state_ref = rest
    c = pl.program_id(2)
    hd = pl.program_id(1)

    @pl.when(c == 0)
    def _():
        state_ref[...] = jnp.zeros_like(state_ref) if zero_init else s0_ref[...]

    n_sub = qf_ref.shape[0] // sub
    refs = ((qf_ref, kf_ref, vf_ref, yf_ref), (qb_ref, kb_ref, vb_ref, yb_ref))
    steps = []
    for j in range(n_sub):
        steps.append((0, slice(j * sub, (j + 1) * sub)))
        steps.append((1, slice((n_sub - 1 - j) * sub, (n_sub - j) * sub)))

    decayed = []
    for d, rows in steps:
        q_ref, k_ref, _, _ = refs[d]
        sc = lax.dot_general(q_ref[rows, :], k_ref[rows, :], (((1,), (1,)), ((), ())), preferred_element_type=F32)
        decayed.append((sc * intra_ref[d]).astype(BF16))

    for (d, rows), sc in zip(steps, decayed):
        q_ref, k_ref, v_ref, y_ref = refs[d]
        q, k, v = q_ref[rows, :], k_ref[rows, :], v_ref[rows, :]
        s = state_ref[d]
        y = (jnp.dot(sc, v, preferred_element_type=F32)
             + jnp.dot(q, s.astype(BF16), preferred_element_type=F32) * qdec_ref[d])
        kd = (k.astype(F32) * kdec_ref[d]).astype(BF16)
        state_ref[d] = s * cdec_ref[d * RET_HEADS + hd] + lax.dot_general(
            kd, v, (((0,), (0,)), ((), ())), preferred_element_type=F32)
        y_ref[rows, :] = y.astype(BF16)

    @pl.when(c == pl.num_programs(2) - 1)
    def _():
        sout_ref[...] = state_ref[...]


def _ret_tables(chunk):
    heads = np.arange(RET_HEADS, dtype=np.float64)
    log_g = [np.log1p(-np.exp2(-5.0 - heads)), np.log1p(-np.exp2(-5.0 - RET_BWD_OFFSET - heads))]
    pos = np.arange(chunk, dtype=np.float64)
    diff = pos[:, None] - pos[None, :]
    intra, qdec, kdec, cdec = [], [], [], []
    for d, lg in enumerate(log_g):
        lg3 = lg[:, None, None]
        if d == 0:
            intra.append(np.where(diff >= 0, np.exp(np.maximum(diff, 0.0)[None] * lg3), 0.0))
            qdec.append(np.exp((pos[None, :] + 1.0) * lg[:, None]))
            kdec.append(np.exp((chunk - 1.0 - pos)[None, :] * lg[:, None]))
        else:
            intra.append(np.where(diff <= 0, np.exp(np.maximum(-diff, 0.0)[None] * lg3), 0.0))
            qdec.append(np.exp((chunk - pos)[None, :] * lg[:, None]))
            kdec.append(np.exp(pos[None, :] * lg[:, None]))
        cdec.append(np.exp(chunk * lg))
    intra = np.stack(intra).astype(np.float32)
    qdec = np.stack(qdec).astype(np.float32)[..., None]
    kdec = np.stack(kdec).astype(np.float32)[..., None]
    cdec = np.stack(cdec).astype(np.float32).reshape(-1)
    return jnp.asarray(intra), jnp.asarray(qdec), jnp.asarray(kdec), jnp.asarray(cdec)


def _retention(proj, s0, *, batch, seq, block):
    sub = min(block, RET_SUB)
    intra, qdec, kdec, cdec = _ret_tables(sub)
    n = seq // block
    kq = RET_HEADS
    kv = RET_QK_COLS // RET_DV

    def fwd(off, width):
        return pl.BlockSpec((block, width), lambda b, h, c: (b * n + c, off + h))

    def bwd(off, width):
        return pl.BlockSpec((block, width), lambda b, h, c: (b * n + n - 1 - c, off + h))

    tab = lambda a: pl.BlockSpec((2, None) + a.shape[2:], lambda b, h, c: (0, h) + (0,) * (a.ndim - 2))
    state_spec = pl.BlockSpec((None, None, 2, RET_DK, RET_DV), lambda b, h, c: (b, h, 0, 0, 0))
    y_shape = jax.ShapeDtypeStruct((batch * seq, RET_V_COLS), BF16)
    return pl.pallas_call(
        functools.partial(_ret_kernel, sub=sub, zero_init=s0 is None),
        grid=(batch, RET_HEADS, n),
        in_specs=[
            fwd(0, RET_DK), fwd(kq, RET_DK), fwd(kv, RET_DV),
            bwd(0, RET_DK), bwd(kq, RET_DK), bwd(kv, RET_DV),
            tab(intra), tab(qdec), tab(kdec),
            pl.BlockSpec(memory_space=pltpu.SMEM),
        ] + ([] if s0 is None else [state_spec]),
        out_specs=[
            pl.BlockSpec((block, RET_DV), lambda b, h, c: (b * n + c, h)),
            pl.BlockSpec((block, RET_DV), lambda b, h, c: (b * n + n - 1 - c, h)),
            state_spec,
        ],
        out_shape=[y_shape, y_shape, jax.ShapeDtypeStruct((batch, RET_HEADS, 2, RET_DK, RET_DV), F32)],
        scratch_shapes=[pltpu.VMEM((2, RET_DK, RET_DV), F32)],
        compiler_params=_params("parallel", "parallel", "arbitrary"),
        name="retention",
    )(*((proj,) * 6 + (intra, qdec, kdec, cdec) + (() if s0 is None else (s0,))))


def _rms_unit(y):
    y = y.astype(F32)
    return y * lax.rsqrt(jnp.mean(y * y, axis=-1, keepdims=True) + NORM_EPS)


def _retout_kernel(x_ref, yf_ref, yb_ref, gain_ref, sh_ref, sc_ref, g_ref, wg_ref, wo_ref, o_ref, a_ref):
    hv = RET_V_COLS
    for r0 in range(0, x_ref.shape[0], ROW_SUB):
        rows = slice(r0, min(r0 + ROW_SUB, x_ref.shape[0]))
        x = x_ref[rows, :]
        h = _prenorm(x, gain_ref[...], sh_ref[...], sc_ref[...]).astype(BF16)
        for c0 in range(0, hv, RET_DV):
            cols = slice(c0, c0 + RET_DV)
            gf = jnp.dot(h, wg_ref[:, c0:c0 + RET_DV], preferred_element_type=F32)
            gb = jnp.dot(h, wg_ref[:, hv + c0:hv + c0 + RET_DV], preferred_element_type=F32)
            y = _silu(gf) * _rms_unit(yf_ref[rows, cols]) + _silu(gb) * _rms_unit(yb_ref[rows, cols])
            a_ref[rows, cols] = y.astype(BF16)
        out = jnp.dot(a_ref[rows, :], wo_ref[...], preferred_element_type=F32)
        o_ref[rows, :] = x + g_ref[...] * out


def _ret_out(x, yf, yb, gain, mods, row_fn, w_in, w_o, slot, *, tm):
    m, d = x.shape
    hv = RET_V_COLS
    sh, sc, g = mods
    return pl.pallas_call(
        _retout_kernel,
        grid=(m // tm,),
        in_specs=[
            pl.BlockSpec((tm, d), lambda i: (i, 0)),
            pl.BlockSpec((tm, hv), lambda i: (i, 0)),
            pl.BlockSpec((tm, hv), lambda i: (i, 0)),
            _resident((1, d)),
            _row_vec_spec(d, row_fn),
            _row_vec_spec(d, row_fn),
            _row_vec_spec(d, row_fn),
            _stacked(w_in, slot, col_block=1, cols=2 * hv),
            _stacked(w_o, slot),
        ],
        out_specs=pl.BlockSpec((tm, d), lambda i: (i, 0)),
        out_shape=jax.ShapeDtypeStruct((m, d), F32),
        scratch_shapes=[pltpu.VMEM((tm, hv), BF16)],
        compiler_params=_params("parallel"),
        name="ret_out",
    )(x, yf, yb, gain, sh, sc, g, w_in, w_o)


FFN_CHUNK = 256


def _ffn_kernel(x_ref, gain_ref, sh_ref, sc_ref, g_ref, wg_ref, wu_ref, wd_ref, *rest, mixer_proj):
    if mixer_proj:
        y_ref, wo_ref, gm_ref, o_ref, a_ref = rest
    else:
        o_ref, a_ref = rest
    for r0 in range(0, x_ref.shape[0], ROW_SUB):
        rows = slice(r0, min(r0 + ROW_SUB, x_ref.shape[0]))
        x = x_ref[rows, :]
        if mixer_proj:
            x = x + gm_ref[...] * jnp.dot(y_ref[rows, :], wo_ref[...], preferred_element_type=F32)
        h = _prenorm(x, gain_ref[...], sh_ref[...], sc_ref[...]).astype(BF16)
        for c0 in range(0, wg_ref.shape[1], FFN_CHUNK):
            cols = slice(c0, c0 + FFN_CHUNK)
            gate = jnp.dot(h, wg_ref[:, cols], preferred_element_type=F32)
            up = jnp.dot(h, wu_ref[:, cols], preferred_element_type=F32)
            a_ref[rows, cols] = (_silu(gate) * up).astype(BF16)
        out = jnp.dot(a_ref[rows, :], wd_ref[...], preferred_element_type=F32)
        o_ref[rows, :] = x + g_ref[...] * out


def _ffn(x, gain, mods, row_fn, wg, wu, wd, layer, *, tm, mixer_proj=None):
    m, d = x.shape
    sh, sc, g = mods
    in_specs = [
        pl.BlockSpec((tm, d), lambda i: (i, 0)),
        _resident((1, d)),
        _row_vec_spec(d, row_fn),
        _row_vec_spec(d, row_fn),
        _row_vec_spec(d, row_fn),
        _stacked(wg, layer),
        _stacked(wu, layer),
        _stacked(wd, layer),
    ]
    args = [x, gain, sh, sc, g, wg, wu, wd]
    if mixer_proj is not None:
        y, w_o, slot, gm = mixer_proj
        in_specs += [
            pl.BlockSpec((tm, y.shape[1]), lambda i: (i, 0)),
            _stacked(w_o, slot),
            _row_vec_spec(d, row_fn),
        ]
        args += [y, w_o, gm]
    return pl.pallas_call(
        functools.partial(_ffn_kernel, mixer_proj=mixer_proj is not None),
        grid=(m // tm,),
        in_specs=in_specs,
        out_specs=pl.BlockSpec((tm, d), lambda i: (i, 0)),
        out_shape=jax.ShapeDtypeStruct((m, d), F32),
        scratch_shapes=[pltpu.VMEM((tm, wg.shape[2]), BF16)],
        compiler_params=_params("parallel"),
        name="ffn_proj" if mixer_proj is not None else "ffn",
    )(*args)


def _axial_tables(n_tokens):
    rows = n_tokens // GRID_W
    row = np.repeat(np.arange(rows, dtype=np.float32), GRID_W)
    col = np.tile(np.arange(GRID_W, dtype=np.float32), rows)
    n_freq = HEAD_DIM // 4
    inv = jnp.asarray(ROPE_BASE, F32) ** (-jnp.arange(n_freq, dtype=F32) / n_freq)
    ang = jnp.concatenate([jnp.asarray(row)[:, None] * inv, jnp.asarray(col)[:, None] * inv], axis=-1)
    cos, sin = jnp.cos(ang), jnp.sin(ang)
    return jnp.tile(cos, (1, 4)), jnp.concatenate([-sin, -sin, sin, sin], axis=-1)


def _retention_rot_tables(n_tokens):
    inv = jnp.asarray(ROPE_BASE, F32) ** (-jnp.linspace(0.0, 1.0, RET_DK // 2, dtype=F32))
    ang = jnp.arange(n_tokens, dtype=F32)[:, None] * inv
    return jnp.cos(ang), jnp.sin(ang)


def kernel(x, c, ctx, c_ctx, ada_w, ada_b, norm_mix, norm_ffn, attn_w_qkv, attn_w_o, attn_q_norm, attn_k_norm,
           attn_sink, pool_w, pool_scale, ret_w_in, ret_w_o, ffn_w_gate, ffn_w_up, ffn_w_down):
    batch, seq, d = x.shape
    ctx_len = ctx.shape[1]
    mod_rows = 16
    tm_lat, tm_ctx, tm_pool = 1024, 256, 512
    assert batch < mod_rows and seq % GRID_W == 0
    assert seq % max(tm_lat, RET_BLOCK, ATTN_QB * ATTN_BLOCK) == 0 and ctx_len % tm_ctx == 0

    cvec = jnp.zeros((mod_rows, d), F32).at[:batch].set(c).at[batch].set(c_ctx)
    mods = _ada_mods(cvec, ada_w, ada_b)
    mods = mods.reshape(DEPTH, mod_rows, 6, 1, d)

    x_lat = x.reshape(batch * seq, d)
    x_ctx = ctx.reshape(batch * ctx_len, d)

    lat_row = lambda i: i // (seq // tm_lat)
    ctx_row = lambda i: batch

    attn_cos, attn_sin = _axial_tables(seq)
    ret_cos, ret_sin = _retention_rot_tables(seq)
    head_of_pair, _ = _pair_layout()
    same_head = (head_of_pair[:, None] == head_of_pair[None, :]).astype(np.float32)
    bd = jnp.asarray(np.concatenate([same_head, same_head], axis=0), BF16)
    half = HEAD_DIM // 2

    def pair_gain(g):
        return jnp.broadcast_to(g.reshape(2, 1, half), (2, 2, half)).reshape(1, LANES)

    nqk = (ATTN_HEADS + ATTN_KV_HEADS) * HEAD_DIM
    slots = attn_w_qkv.shape[0]
    w_qk = attn_w_qkv[:, :, :nqk].reshape(slots, d, nqk // LANES, 2, 2, half)
    w_qk = jnp.swapaxes(w_qk, 3, 4).reshape(slots, d, nqk)
    attn_qkv_w = jnp.concatenate([w_qk, attn_w_qkv[:, :, nqk:]], axis=-1).astype(BF16)
    attn_o_w = attn_w_o.astype(BF16)
    ret_in_w = ret_w_in.astype(BF16)
    ret_o_w = ret_w_o.astype(BF16)
    pool_w_bf = pool_w.astype(BF16)
    wg_all, wu_all, wd_all = ffn_w_gate.astype(BF16), ffn_w_up.astype(BF16), ffn_w_down.astype(BF16)

    for i in range(DEPTH):
        kind, slot = i % N_MIXERS, i // N_MIXERS
        need_ctx_out = i < DEPTH - 1
        mod = [mods[i, :, k] for k in range(6)]
        sh_m, sc_m, g_m, sh_f, sc_f, g_f = mod
        gain_m = norm_mix[i].reshape(1, d)
        gain_f = norm_ffn[i].reshape(1, d)
        proj_lat = proj_ctx = None

        if kind == 0:
            qg = pair_gain(attn_q_norm[slot])
            kg = pair_gain(attn_k_norm[slot])
            sink = attn_sink[slot].astype(F32) * LOG2E
            bound = (1.02 * LOG2E * HEAD_DIM ** 0.5 * jnp.max(jnp.abs(attn_q_norm[slot]))
                     * jnp.max(jnp.abs(attn_k_norm[slot]))).astype(F32).reshape(1)
            q_c, k_c, v_c = _attn_qkv(x_ctx, gain_m, (sh_m, sc_m), ctx_row, attn_qkv_w, slot, qg, kg, attn_cos,
                                      attn_sin, bd, rotate=False, tm=tm_ctx, tiles_per_seq=1)
            q_l, k_l, v_l = _attn_qkv(x_lat, gain_m, (sh_m, sc_m), lat_row, attn_qkv_w, slot, qg, kg, attn_cos,
                                      attn_sin, bd, rotate=True, tm=tm_lat, tiles_per_seq=seq // tm_lat)
            a_l = _attention(bound, sink, q_l, k_l, v_l, k_c, v_c, batch=batch, seq=seq, ctx_len=ctx_len, local=True)
            proj_lat = (a_l, attn_o_w, slot, g_m)
            if need_ctx_out:
                a_c = _attention(bound, sink, q_c, None, None, k_c, v_c, batch=batch, seq=ctx_len, ctx_len=ctx_len,
                                 local=False)
                proj_ctx = (a_c, attn_o_w, slot, g_m)
        elif kind == 1:
            w_p = pool_w_bf[slot]
            ls = pool_scale[slot].reshape(1, d)
            pool_row = lambda i: i // (seq // tm_pool)
            x_lat = _pool_mixer(x_lat, gain_m, (sh_m, sc_m, g_m), pool_row, w_p, ls, tm=tm_pool, seq=seq)
            if need_ctx_out:
                x_ctx = _pool_mixer(x_ctx, gain_m, (sh_m, sc_m, g_m), ctx_row, w_p, ls, tm=tm_ctx, seq=ctx_len)
        else:
            p_c = _ret_proj(x_ctx, gain_m, (sh_m, sc_m), ctx_row, ret_in_w, slot, ret_cos, ret_sin,
                            rotate=False, tm=tm_ctx, tiles_per_seq=1)
            yc_f, yc_b, s_c = _retention(p_c, None, batch=batch, seq=ctx_len, block=min(ctx_len, RET_BLOCK))
            p_l = _ret_proj(x_lat, gain_m, (sh_m, sc_m), lat_row, ret_in_w, slot, ret_cos, ret_sin,
                            rotate=True, tm=tm_lat, tiles_per_seq=seq // tm_lat)
            yl_f, yl_b, _ = _retention(p_l, s_c, batch=batch, seq=seq, block=RET_BLOCK)
            x_lat = _ret_out(x_lat, yl_f, yl_b, gain_m, (sh_m, sc_m, g_m), lat_row, ret_in_w, ret_o_w, slot,
                             tm=tm_lat)
            if need_ctx_out:
                x_ctx = _ret_out(x_ctx, yc_f, yc_b, gain_m, (sh_m, sc_m, g_m), ctx_row, ret_in_w, ret_o_w, slot,
                                 tm=tm_ctx)

        x_lat = _ffn(x_lat, gain_f, (sh_f, sc_f, g_f), lat_row, wg_all, wu_all, wd_all, i, tm=tm_lat,
                     mixer_proj=proj_lat)
        if need_ctx_out:
            x_ctx = _ffn(x_ctx, gain_f, (sh_f, sc_f, g_f), ctx_row, wg_all, wu_all, wd_all, i, tm=tm_ctx,
                         mixer_proj=proj_ctx)

    return x_lat.reshape(batch, seq, d)
```

```python
import functools

import jax
import jax.numpy as jnp
import numpy as np
from jax import lax
from jax.experimental import pallas as pl
from jax.experimental.pallas import tpu as pltpu

F32 = jnp.float32
BF16 = jnp.bfloat16

DEPTH = 4
GRID_W = 64
N_MIXERS = 3
ATTN_HEADS = 16
ATTN_KV_HEADS = 4
ATTN_GROUP = ATTN_HEADS // ATTN_KV_HEADS
HEAD_DIM = 64
WINDOW = 128
ATTN_BLOCK = 128
ROPE_BASE = 10000.0
NEG_INF = -1e30
POOL_WINDOWS = (2, 4, 8, 16)
POOL_HALO = 8
POOL_PAD = 16
RET_HEADS = 4
RET_DK = 256
RET_DV = 512
RET_SUB = 256
RET_BLOCK = 2048
RET_BWD_OFFSET = 0.5
NORM_EPS = 1e-6

VMEM_LIMIT_BYTES = 56 * 1024 * 1024
LANES = 128
ROW_SUB = 512
RETOUT_SUB = 256


def _params(*sem):
    return pltpu.CompilerParams(dimension_semantics=sem, vmem_limit_bytes=VMEM_LIMIT_BYTES)


def _silu(x):
    return x * jax.nn.sigmoid(x)


def _prenorm(x, gain, shift, scale):
    y = x * lax.rsqrt(jnp.mean(x * x, axis=-1, keepdims=True) + NORM_EPS)
    return y * (gain * (1.0 + scale)) + shift


def _resident(shape):
    nd = len(shape)
    return pl.BlockSpec(shape, lambda *_: (0,) * nd)


def _stacked(arr, index, col_block=0, cols=None):
    shape = arr.shape[1:] if cols is None else arr.shape[1:-1] + (cols,)
    nd = len(shape)
    return pl.BlockSpec((None,) + shape, lambda *_: (index,) + (0,) * (nd - 1) + (col_block,),
                        pipeline_mode=pl.Buffered(1))


def _ada_kernel(c_ref, w_ref, b_ref, o_ref):
    cond = _silu(c_ref[...])
    o_ref[...] = jnp.dot(cond.astype(BF16), w_ref[...].astype(BF16), preferred_element_type=F32) + b_ref[...]


def _ada_mods(cvec, ada_w, ada_b):
    depth, d, n = ada_w.shape
    rows = cvec.shape[0]
    tn = 1536
    return pl.pallas_call(
        _ada_kernel,
        grid=(depth, n // tn),
        in_specs=[
            pl.BlockSpec((rows, d), lambda l, j: (0, 0)),
            pl.BlockSpec((None, d, tn), lambda l, j: (l, 0, j)),
            pl.BlockSpec((None, 1, tn), lambda l, j: (l, 0, j)),
        ],
        out_specs=pl.BlockSpec((None, rows, tn), lambda l, j: (l, 0, j)),
        out_shape=jax.ShapeDtypeStruct((depth, rows, n), F32),
        compiler_params=_params("parallel", "parallel"),
        name="ada_mods",
    )(cvec, ada_w, ada_b.reshape(depth, 1, n))


def _row_vec_spec(d, row_fn):
    return pl.BlockSpec((None, 1, d), lambda i, *_: (row_fn(i), 0, 0))


def _qkv_kernel(x_ref, gain_ref, sh_ref, sc_ref, w_ref, qg_ref, kg_ref, cos_ref, sin_ref, bd_ref,
                q_ref, k_ref, v_ref, *, rotate):
    bd2 = bd_ref[...]
    qg = qg_ref[...]
    kg = kg_ref[...]
    nq = ATTN_HEADS * HEAD_DIM
    nk = ATTN_KV_HEADS * HEAD_DIM
    width = 2 * LANES

    def head_norm(xg, g):
        x2 = xg * xg
        hi = x2.astype(BF16)
        lo = (x2 - hi.astype(F32)).astype(BF16)
        ssq = jnp.dot(jnp.concatenate([hi, lo], axis=1), bd2, preferred_element_type=F32)
        return (xg * lax.rsqrt(ssq * (1.0 / HEAD_DIM) + NORM_EPS)) * g

    for r0 in range(0, x_ref.shape[0], ROW_SUB):
        rows = slice(r0, min(r0 + ROW_SUB, x_ref.shape[0]))
        h = _prenorm(x_ref[rows, :], gain_ref[...], sh_ref[...], sc_ref[...]).astype(BF16)

        def project(c0):
            return jnp.dot(h, w_ref[:, c0:c0 + width], preferred_element_type=F32)

        nxt = project(0)
        for j in range((nq + nk) // LANES):
            if j % 2 == 0:
                wide = nxt
                nxt = project((j + 2) * LANES)
            acc = wide[:, (j % 2) * LANES:(j % 2 + 1) * LANES]
            is_q = j < nq // LANES
            y = head_norm(acc, qg) * Q_SCALE if is_q else head_norm(acc, kg)
            if rotate:
                y = y * cos_ref[rows, :] + pltpu.roll(y, LANES // 2, 1) * sin_ref[rows, :]
            if is_q:
                q_ref[rows, j * LANES:(j + 1) * LANES] = y.astype(BF16)
            else:
                k_ref[rows, j * LANES - nq:(j + 1) * LANES - nq] = y.astype(BF16)
        v_ref[rows, :] = nxt.astype(BF16)


def _pair_layout():
    lane = np.arange(LANES)
    head_of_pair = (lane // (HEAD_DIM // 2)) % 2
    dim = lane % (HEAD_DIM // 2) + (HEAD_DIM // 2) * (lane // HEAD_DIM)
    return head_of_pair, dim


def _attn_qkv(x, gain, mods, row_fn, w, slot, qg, kg, cos_t, sin_t, bd, *, rotate, tm, tiles_per_seq):
    m, d = x.shape
    nq = ATTN_HEADS * HEAD_DIM
    nk = ATTN_KV_HEADS * HEAD_DIM
    sh, sc = mods
    tab_spec = pl.BlockSpec((tm, LANES), lambda i: (i % tiles_per_seq, 0))
    return pl.pallas_call(
        functools.partial(_qkv_kernel, rotate=rotate),
        grid=(m // tm,),
        in_specs=[
            pl.BlockSpec((tm, d), lambda i: (i, 0)),
            _resident((1, d)),
            _row_vec_spec(d, row_fn),
            _row_vec_spec(d, row_fn),
            _stacked(w, slot),
            _resident((1, LANES)),
            _resident((1, LANES)),
            tab_spec,
            tab_spec,
            _resident((2 * LANES, LANES)),
        ],
        out_specs=[
            pl.BlockSpec((tm, nq), lambda i: (i, 0)),
            pl.BlockSpec((tm, nk), lambda i: (i, 0)),
            pl.BlockSpec((tm, nk), lambda i: (i, 0)),
        ],
        out_shape=[
            jax.ShapeDtypeStruct((m, nq), BF16),
            jax.ShapeDtypeStruct((m, nk), BF16),
            jax.ShapeDtypeStruct((m, nk), BF16),
        ],
        compiler_params=_params("parallel"),
        name="attn_qkv_rot" if rotate else "attn_qkv",
    )(x, gain, sh, sc, w, qg, kg, cos_t, sin_t, bd)


ATTN_SHIFT_LIMIT = 30.0
LOG2E = 1.4426950408889634
Q_SCALE = HEAD_DIM ** -0.5 * LOG2E
ATTN_QB = 4


def _attn_heads(bound_ref, sink_ref, q_ref, q_rows, k_parts, v_parts, mask_ref, o_ref, *, use_bound):
    blk = q_rows.stop - q_rows.start
    lane = lax.broadcasted_iota(jnp.int32, (1, LANES), 1)
    low = lane < HEAD_DIM
    pair_bit = (lane // (HEAD_DIM // 2)) % 2
    one = jnp.ones((), BF16)
    zero = jnp.zeros((), BF16)

    def rows(parts, cols):
        pieces = [r[rs, cols] for r, rs in parts]
        return pieces[0] if len(pieces) == 1 else jnp.concatenate(pieces, axis=0)

    def scores(h):
        par = h % 2
        cols = slice((h // 2) * LANES, (h // 2 + 1) * LANES)
        own_qk = pair_bit == par
        k2 = rows(k_parts, cols)
        q_parts = []
        for g in range(ATTN_GROUP):
            hd = h * ATTN_GROUP + g
            qt = q_ref[q_rows, (hd // 2) * LANES:(hd // 2 + 1) * LANES]
            if hd % 2 != par:
                qt = pltpu.roll(qt, HEAD_DIM // 2 if par == 1 else LANES - HEAD_DIM // 2, 1)
            q_parts.append(jnp.where(own_qk, qt, zero))
        q2 = jnp.concatenate(q_parts, axis=0)
        return lax.dot_general(q2, k2, (((1,), (1,)), ((), ())), preferred_element_type=F32)

    def finish(h, s):
        par = h % 2
        cols = slice((h // 2) * LANES, (h // 2 + 1) * LANES)
        own_v = low if par == 0 else jnp.logical_not(low)
        v2 = jnp.where(own_v, rows(v_parts, cols), one)
        heads = [h * ATTN_GROUP + g for g in range(ATTN_GROUP)]
        if mask_ref is not None:
            b = ATTN_BLOCK
            s = jnp.concatenate([s[:, :b] + mask_ref[:, :b], s[:, b:2 * b],
                                 s[:, 2 * b:3 * b] + mask_ref[:, b:], s[:, 3 * b:]], axis=1)
        if use_bound:
            shifts = [jnp.maximum(bound_ref[0], sink_ref[hd]) for hd in heads]
            shift = jnp.concatenate([jnp.full((blk, 1), sh, F32) for sh in shifts], axis=0)
            es = [jnp.exp2(jnp.full((1, LANES), sink_ref[hd] - sh, F32)) for hd, sh in zip(heads, shifts)]
        else:
            sink = jnp.concatenate([jnp.full((blk, 1), sink_ref[hd], F32) for hd in heads], axis=0)
            shift = jnp.maximum(jnp.max(s, axis=1, keepdims=True), sink)
            es_all = jnp.exp2(sink - shift)
            es = [es_all[g * blk:(g + 1) * blk] for g in range(ATTN_GROUP)]
        p = jnp.exp2(s - shift).astype(BF16)
        oa = jnp.dot(p, v2, preferred_element_type=F32)
        res = []
        for g, hd in enumerate(heads):
            a = oa[g * blk:(g + 1) * blk]
            r = pltpu.roll(a, HEAD_DIM, 1)
            res.append(a / (r + es[g]) if hd % 2 == par else r / (a + es[g]))
        for t in range(ATTN_GROUP // 2):
            tile = jnp.where(low, res[2 * t], res[2 * t + 1])
            c0 = (heads[2 * t] // 2) * LANES
            o_ref[q_rows, c0:c0 + LANES] = tile.astype(BF16)

    s_next = scores(0)
    for h in range(ATTN_KV_HEADS):
        s = s_next
        if h + 1 < ATTN_KV_HEADS:
            s_next = scores(h + 1)
        finish(h, s)


def _attn_kernel(bound_ref, sink_ref, q_ref, *refs, local):
    full = slice(None)
    if local:
        kp_ref, kc_ref, kn_ref, vp_ref, vc_ref, vn_ref, kx_ref, vx_ref = refs[:8]
        mask_refs = refs[8:8 + ATTN_QB]
        o_ref = refs[8 + ATTN_QB]
        blocks = [slice(j * ATTN_BLOCK, (j + 1) * ATTN_BLOCK) for j in range(ATTN_QB)]
        k_blocks = [(kp_ref, full)] + [(kc_ref, rs) for rs in blocks] + [(kn_ref, full)]
        v_blocks = [(vp_ref, full)] + [(vc_ref, rs) for rs in blocks] + [(vn_ref, full)]
        work = [(blocks[j], k_blocks[j:j + 3] + [(kx_ref, full)], v_blocks[j:j + 3] + [(vx_ref, full)], mask_refs[j])
                for j in range(ATTN_QB)]
    else:
        kx_ref, vx_ref, o_ref = refs
        work = [(slice(0, q_ref.shape[0]), [(kx_ref, full)], [(vx_ref, full)], None)]
    use_bound = bound_ref[0] <= ATTN_SHIFT_LIMIT * LOG2E

    def run(flag):
        for q_rows, k_parts, v_parts, mask_ref in work:
            _attn_heads(bound_ref, sink_ref, q_ref, q_rows, k_parts, v_parts, mask_ref, o_ref, use_bound=flag)

    @pl.when(use_bound)
    def _():
        run(True)

    @pl.when(jnp.logical_not(use_bound))
    def _():
        run(False)


def _attn_mask_table():
    span = 3 * ATTN_BLOCK
    offs = np.arange(span) - ATTN_BLOCK
    rel = offs[None, :] - np.arange(ATTN_BLOCK)[:, None]
    near = np.abs(rel) <= WINDOW
    tabs = []
    for variant in range(3):
        ok = near.copy()
        if variant == 0:
            ok[:, :ATTN_BLOCK] = False
        if variant == 2:
            ok[:, 2 * ATTN_BLOCK:] = False
        tabs.append(np.where(ok, 0.0, NEG_INF).astype(np.float32))
    return np.stack(tabs)


def _attention(bound, sink, q, k, v, kx, vx, *, batch, seq, ctx_len, local):
    nblk = seq // ATTN_BLOCK
    nq = ATTN_HEADS * HEAD_DIM
    nk = ATTN_KV_HEADS * HEAD_DIM
    qb = ATTN_QB if local else 1
    nstep = nblk // qb
    q_spec = pl.BlockSpec((qb * ATTN_BLOCK, nq), lambda b, i: (b * nstep + i, 0))
    x_spec = pl.BlockSpec((ctx_len, nk), lambda b, i: (b, 0))
    smem = pl.BlockSpec(memory_space=pltpu.SMEM)
    if local:
        local_mask = _attn_mask_table()
        mask = np.concatenate([local_mask[:, :, :ATTN_BLOCK], local_mask[:, :, 2 * ATTN_BLOCK:]], axis=2)
        mask = jnp.asarray(np.tile(mask, (1, ATTN_GROUP, 1)))
        prev = pl.BlockSpec((ATTN_BLOCK, nk), lambda b, i: (b * nblk + jnp.maximum(qb * i - 1, 0), 0))
        cur = pl.BlockSpec((qb * ATTN_BLOCK, nk), lambda b, i: (b * nstep + i, 0))
        nxt = pl.BlockSpec((ATTN_BLOCK, nk), lambda b, i: (b * nblk + jnp.minimum(qb * (i + 1), nblk - 1), 0))

        def mask_spec(j):
            def index(b, i):
                g = qb * i + j
                return (jnp.where(g == 0, 0, jnp.where(g == nblk - 1, 2, 1)), 0, 0)
            return pl.BlockSpec((None,) + mask.shape[1:], index)

        in_specs = ([smem, smem, q_spec, prev, cur, nxt, prev, cur, nxt, x_spec, x_spec]
                    + [mask_spec(j) for j in range(qb)])
        args = (bound, sink, q, k, k, k, v, v, v, kx, vx) + (mask,) * qb
    else:
        in_specs = [smem, smem, q_spec, x_spec, x_spec]
        args = (bound, sink, q, kx, vx)
    return pl.pallas_call(
        functools.partial(_attn_kernel, local=local),
        grid=(batch, nstep),
        in_specs=in_specs,
        out_specs=q_spec,
        out_shape=jax.ShapeDtypeStruct(q.shape, BF16),
        compiler_params=_params("parallel", "parallel"),
        name="attn_local" if local else "attn_ctx",
    )(*args)


def _pool_kernel(xp_ref, x_ref, xn_ref, gain_ref, sh_ref, sc_ref, g_ref, w_ref, ls_ref, inv_ref, o_ref,
                 h_ref, pair_ref, quad_ref, oct_ref, *, tiles_per_seq):
    tm, d = x_ref.shape
    it = pl.program_id(0) % tiles_per_seq
    gain, sh, sc = gain_ref[...], sh_ref[...], sc_ref[...]
    hp = jnp.where(it > 0, _prenorm(xp_ref[...], gain, sh, sc), 0.0)
    hn = jnp.where(it < tiles_per_seq - 1, _prenorm(xn_ref[...], gain, sh, sc), 0.0)
    x = x_ref[...]
    base = POOL_HALO
    h_ref[0:base, :] = hp
    h_ref[base:base + tm, :] = _prenorm(x, gain, sh, sc)
    h_ref[base + tm:base + tm + POOL_HALO, :] = hn
    h_ref[base + tm + POOL_HALO:, :] = jnp.zeros((POOL_PAD, d), F32)
    gate_scale = g_ref[...] * ls_ref[...]
    gd = d // len(POOL_WINDOWS)
    n_pair, n_quad, n_oct = pair_ref.shape[0], quad_ref.shape[0], oct_ref.shape[0]
    for gi, win in enumerate(POOL_WINDOWS):
        cols = slice(gi * gd, (gi + 1) * gd)
        if win >= 4:
            pair_ref[:, cols] = h_ref[0:n_pair, cols] + h_ref[1:n_pair + 1, cols]
        if win >= 8:
            quad_ref[:, cols] = pair_ref[0:n_quad, cols] + pair_ref[2:n_quad + 2, cols]
        if win == 2:
            tot = h_ref[base - 1:base - 1 + tm, cols] + h_ref[base:base + tm, cols]
        elif win == 4:
            tot = pair_ref[base - 2:base - 2 + tm, cols] + pair_ref[base:base + tm, cols]
        elif win == 8:
            tot = quad_ref[base - 4:base - 4 + tm, cols] + quad_ref[base:base + tm, cols]
        else:
            oct_ref[:, cols] = quad_ref[0:n_oct, cols] + quad_ref[4:n_oct + 4, cols]
            tot = oct_ref[0:tm, cols] + oct_ref[base:base + tm, cols]
        pooled = tot * inv_ref[:, gi:gi + 1] - h_ref[base:base + tm, cols]
        y = jnp.dot(pooled.astype(BF16), w_ref[gi], preferred_element_type=F32)
        o_ref[:, cols] = x[:, cols] + gate_scale[:, cols] * y


def _pool_inv_counts(seq):
    t = np.arange(seq)
    cols = [1.0 / (np.minimum(t + w // 2, seq) - np.maximum(t - w // 2, 0)) for w in POOL_WINDOWS]
    return jnp.asarray(np.stack(cols, axis=1).astype(np.float32))


def _pool_mixer(x, gain, mods, row_fn, w, layer_scale, *, tm, seq):
    m, d = x.shape
    sh, sc, g = mods
    tiles_per_seq = seq // tm
    hb = tm // POOL_HALO
    nhalo = m // POOL_HALO
    return pl.pallas_call(
        functools.partial(_pool_kernel, tiles_per_seq=tiles_per_seq),
        grid=(m // tm,),
        in_specs=[
            pl.BlockSpec((POOL_HALO, d), lambda i: (jnp.maximum(i * hb - 1, 0), 0)),
            pl.BlockSpec((tm, d), lambda i: (i, 0)),
            pl.BlockSpec((POOL_HALO, d), lambda i: (jnp.minimum((i + 1) * hb, nhalo - 1), 0)),
            _resident((1, d)),
            _row_vec_spec(d, row_fn),
            _row_vec_spec(d, row_fn),
            _row_vec_spec(d, row_fn),
            _resident(w.shape),
            _resident((1, d)),
            pl.BlockSpec((tm, len(POOL_WINDOWS)), lambda i: (i % tiles_per_seq, 0)),
        ],
        out_specs=pl.BlockSpec((tm, d), lambda i: (i, 0)),
        out_shape=jax.ShapeDtypeStruct((m, d), F32),
        scratch_shapes=[pltpu.VMEM((tm + 2 * POOL_HALO + POOL_PAD, d), F32),
                        pltpu.VMEM((tm + 3 * POOL_HALO, d), F32),
                        pltpu.VMEM((tm + 2 * POOL_HALO, d), F32),
                        pltpu.VMEM((tm + POOL_HALO, d), F32)],
        compiler_params=_params("parallel"),
        name="pool_mixer",
    )(x, x, x, gain, sh, sc, g, w, layer_scale, _pool_inv_counts(seq))


RET_TN = 512
RET_QK_COLS = 2 * RET_HEADS * RET_DK
RET_V_COLS = RET_HEADS * RET_DV
RET_QKV_COLS = RET_QK_COLS + RET_V_COLS


def _retproj_kernel(x_ref, gain_ref, sh_ref, sc_ref, w_ref, cos_ref, sin_ref, o_ref, *, rotate):
    half = RET_DK // 2
    q_cols = RET_HEADS * RET_DK
    for r0 in range(0, x_ref.shape[0], ROW_SUB):
        rows = slice(r0, min(r0 + ROW_SUB, x_ref.shape[0]))
        h = _prenorm(x_ref[rows, :], gain_ref[...], sh_ref[...], sc_ref[...]).astype(BF16)
        for c0 in range(0, RET_QKV_COLS, RET_TN):
            acc = jnp.dot(h, w_ref[:, c0:c0 + RET_TN], preferred_element_type=F32)
            if c0 >= RET_QK_COLS:
                o_ref[rows, c0:c0 + RET_TN] = acc.astype(BF16)
                continue
            scale = 1.0 if c0 < q_cols else RET_DK ** -0.5
            for hd in range(RET_TN // RET_DK):
                x1 = acc[:, hd * RET_DK:hd * RET_DK + half]
                x2 = acc[:, hd * RET_DK + half:(hd + 1) * RET_DK]
                if rotate:
                    cos, sin = cos_ref[rows, :], sin_ref[rows, :]
                    x1, x2 = x1 * cos - x2 * sin, x2 * cos + x1 * sin
                o_ref[rows, c0 + hd * RET_DK:c0 + hd * RET_DK + half] = (x1 * scale).astype(BF16)
                o_ref[rows, c0 + hd * RET_DK + half:c0 + (hd + 1) * RET_DK] = (x2 * scale).astype(BF16)


def _ret_proj(x, gain, mods, row_fn, w_in, slot, cos_t, sin_t, *, rotate, tm, tiles_per_seq):
    m, d = x.shape
    n = RET_QKV_COLS
    sh, sc = mods
    tab_spec = pl.BlockSpec((tm, RET_DK // 2), lambda i: (i % tiles_per_seq, 0))
    return pl.pallas_call(
        functools.partial(_retproj_kernel, rotate=rotate),
        grid=(m // tm,),
        in_specs=[
            pl.BlockSpec((tm, d), lambda i: (i, 0)),
            _resident((1, d)),
            _row_vec_spec(d, row_fn),
            _row_vec_spec(d, row_fn),
            _stacked(w_in, slot, col_block=0, cols=n),
            tab_spec,
            tab_spec,
        ],
        out_specs=pl.BlockSpec((tm, n), lambda i: (i, 0)),
        out_shape=jax.ShapeDtypeStruct((m, n), BF16),
        compiler_params=_params("parallel"),
        name="ret_proj_rot" if rotate else "ret_proj",
    )(x, gain, sh, sc, w_in, cos_t, sin_t)


def _ret_kernel(qf_ref, kf_ref, vf_ref, qb_ref, kb_ref, vb_ref, intra_ref, qdec_ref, kdec_ref, cdec_ref,
                *rest, sub, zero_init):
    if zero_init:
        yf_ref, yb_ref, sout_ref, state_ref = rest
    else:
        s0_ref, yf_ref, yb_ref, sout_ref, state_ref = rest
    c = pl.program_id(2)
    hd = pl.program_id(1)

    @pl.when(c == 0)
    def _():
        state_ref[...] = jnp.zeros_like(state_ref) if zero_init else s0_ref[...]

    n_sub = qf_ref.shape[0] // sub
    refs = ((qf_ref, kf_ref, vf_ref, yf_ref), (qb_ref, kb_ref, vb_ref, yb_ref))
    steps = []
    for j in range(n_sub):
        steps.append((0, slice(j * sub, (j + 1) * sub)))
        steps.append((1, slice((n_sub - 1 - j) * sub, (n_sub - j) * sub)))

    decayed = []
    for d, rows in steps:
        q_ref, k_ref, _, _ = refs[d]
        sc = lax.dot_general(q_ref[rows, :], k_ref[rows, :], (((1,), (1,)), ((), ())), preferred_element_type=F32)
        decayed.append((sc * intra_ref[d]).astype(BF16))

    for (d, rows), sc in zip(steps, decayed):
        q_ref, k_ref, v_ref, y_ref = refs[d]
        q, k, v = q_ref[rows, :], k_ref[rows, :], v_ref[rows, :]
        s = state_ref[d]
        y = (jnp.dot(sc, v, preferred_element_type=F32)
             + jnp.dot(q, s.astype(BF16), preferred_element_type=F32) * qdec_ref[d])
        kd = (k.astype(F32) * kdec_ref[d]).astype(BF16)
        state_ref[d] = s * cdec_ref[d * RET_HEADS + hd] + lax.dot_general(
            kd, v, (((0,), (0,)), ((), ())), preferred_element_type=F32)
        y_ref[rows, :] = y.astype(BF16)

    @pl.when(c == pl.num_programs(2) - 1)
    def _():
        sout_ref[...] = state_ref[...]


def _ret_tables(chunk):
    heads = np.arange(RET_HEADS, dtype=np.float64)
    log_g = [np.log1p(-np.exp2(-5.0 - heads)), np.log1p(-np.exp2(-5.0 - RET_BWD_OFFSET - heads))]
    pos = np.arange(chunk, dtype=np.float64)
    diff = pos[:, None] - pos[None, :]
    intra, qdec, kdec, cdec = [], [], [], []
    for d, lg in enumerate(log_g):
        lg3 = lg[:, None, None]
        if d == 0:
            intra.append(np.where(diff >= 0, np.exp(np.maximum(diff, 0.0)[None] * lg3), 0.0))
            qdec.append(np.exp((pos[None, :] + 1.0) * lg[:, None]))
            kdec.append(np.exp((chunk - 1.0 - pos)[None, :] * lg[:, None]))
        else:
            intra.append(np.where(diff <= 0, np.exp(np.maximum(-diff, 0.0)[None] * lg3), 0.0))
            qdec.append(np.exp((chunk - pos)[None, :] * lg[:, None]))
            kdec.append(np.exp(pos[None, :] * lg[:, None]))
        cdec.append(np.exp(chunk * lg))
    intra = np.stack(intra).astype(np.float32)
    qdec = np.stack(qdec).astype(np.float32)[..., None]
    kdec = np.stack(kdec).astype(np.float32)[..., None]
    cdec = np.stack(cdec).astype(np.float32).reshape(-1)
    return jnp.asarray(intra), jnp.asarray(qdec), jnp.asarray(kdec), jnp.asarray(cdec)


def _retention(proj, s0, *, batch, seq, block):
    sub = min(block, RET_SUB)
    intra, qdec, kdec, cdec = _ret_tables(sub)
    n = seq // block
    kq = RET_HEADS
    kv = RET_QK_COLS // RET_DV

    def fwd(off, width):
        return pl.BlockSpec((block, width), lambda b, h, c: (b * n + c, off + h))

    def bwd(off, width):
        return pl.BlockSpec((block, width), lambda b, h, c: (b * n + n - 1 - c, off + h))

    tab = lambda a: pl.BlockSpec((2, None) + a.shape[2:], lambda b, h, c: (0, h) + (0,) * (a.ndim - 2))
    state_spec = pl.BlockSpec((None, None, 2, RET_DK, RET_DV), lambda b, h, c: (b, h, 0, 0, 0))
    y_shape = jax.ShapeDtypeStruct((batch * seq, RET_V_COLS), BF16)
    return pl.pallas_call(
        functools.partial(_ret_kernel, sub=sub, zero_init=s0 is None),
        grid=(batch, RET_HEADS, n),
        in_specs=[
            fwd(0, RET_DK), fwd(kq, RET_DK), fwd(kv, RET_DV),
            bwd(0, RET_DK), bwd(kq, RET_DK), bwd(kv, RET_DV),
            tab(intra), tab(qdec), tab(kdec),
            pl.BlockSpec(memory_space=pltpu.SMEM),
        ] + ([] if s0 is None else [state_spec]),
        out_specs=[
            pl.BlockSpec((block, RET_DV), lambda b, h, c: (b * n + c, h)),
            pl.BlockSpec((block, RET_DV), lambda b, h, c: (b * n + n - 1 - c, h)),
            state_spec,
        ],
        out_shape=[y_shape, y_shape, jax.ShapeDtypeStruct((batch, RET_HEADS, 2, RET_DK, RET_DV), F32)],
        scratch_shapes=[pltpu.VMEM((2, RET_DK, RET_DV), F32)],
        compiler_params=_params("parallel", "parallel", "arbitrary"),
        name="retention",
    )(*((proj,) * 6 + (intra, qdec, kdec, cdec) + (() if s0 is None else (s0,))))


def _rms_unit(y):
    y = y.astype(F32)
    return y * lax.rsqrt(jnp.mean(y * y, axis=-1, keepdims=True) + NORM_EPS)


def _retout_kernel(x_ref, yf_ref, yb_ref, gain_ref, sh_ref, sc_ref, g_ref, wg_ref, wo_ref, o_ref, a_ref):
    hv = RET_V_COLS
    for r0 in range(0, x_ref.shape[0], RETOUT_SUB):
        rows = slice(r0, min(r0 + RETOUT_SUB, x_ref.shape[0]))
        x = x_ref[rows, :]
        h = _prenorm(x, gain_ref[...], sh_ref[...], sc_ref[...]).astype(BF16)
        for c0 in range(0, hv, RET_DV):
            cols = slice(c0, c0 + RET_DV)
            gf = jnp.dot(h, wg_ref[:, c0:c0 + RET_DV], preferred_element_type=F32)
            gb = jnp.dot(h, wg_ref[:, hv + c0:hv + c0 + RET_DV], preferred_element_type=F32)
            y = _silu(gf) * _rms_unit(yf_ref[rows, cols]) + _silu(gb) * _rms_unit(yb_ref[rows, cols])
            a_ref[rows, cols] = y.astype(BF16)
        out = jnp.dot(a_ref[rows, :], wo_ref[...], preferred_element_type=F32)
        o_ref[rows, :] = x + g_ref[...] * out


def _ret_out(x, yf, yb, gain, mods, row_fn, w_in, w_o, slot, *, tm):
    m, d = x.shape
    hv = RET_V_COLS
    sh, sc, g = mods
    return pl.pallas_call(
        _retout_kernel,
        grid=(m // tm,),
        in_specs=[
            pl.BlockSpec((tm, d), lambda i: (i, 0)),
            pl.BlockSpec((tm, hv), lambda i: (i, 0)),
            pl.BlockSpec((tm, hv), lambda i: (i, 0)),
            _resident((1, d)),
            _row_vec_spec(d, row_fn),
            _row_vec_spec(d, row_fn),
            _row_vec_spec(d, row_fn),
            _stacked(w_in, slot, col_block=1, cols=2 * hv),
            _stacked(w_o, slot),
        ],
        out_specs=pl.BlockSpec((tm, d), lambda i: (i, 0)),
        out_shape=jax.ShapeDtypeStruct((m, d), F32),
        scratch_shapes=[pltpu.VMEM((tm, hv), BF16)],
        compiler_params=_params("parallel"),
        name="ret_out",
    )(x, yf, yb, gain, sh, sc, g, w_in, w_o)


FFN_CHUNK = 256


def _ffn_kernel(x_ref, gain_ref, sh_ref, sc_ref, g_ref, wg_ref, wu_ref, wd_ref, *rest, mixer_proj):
    if mixer_proj:
        y_ref, wo_ref, gm_ref, o_ref, a_ref = rest
    else:
        o_ref, a_ref = rest
    for r0 in range(0, x_ref.shape[0], ROW_SUB):
        rows = slice(r0, min(r0 + ROW_SUB, x_ref.shape[0]))
        x = x_ref[rows, :]
        if mixer_proj:
            x = x + gm_ref[...] * jnp.dot(y_ref[rows, :], wo_ref[...], preferred_element_type=F32)
        h = _prenorm(x, gain_ref[...], sh_ref[...], sc_ref[...]).astype(BF16)
        for c0 in range(0, wg_ref.shape[1], FFN_CHUNK):
            cols = slice(c0, c0 + FFN_CHUNK)
            gate = jnp.dot(h, wg_ref[:, cols], preferred_element_type=F32)
            up = jnp.dot(h, wu_ref[:, cols], preferred_element_type=F32)
            a_ref[rows, cols] = (_silu(gate) * up).astype(BF16)
        out = jnp.dot(a_ref[rows, :], wd_ref[...], preferred_element_type=F32)
        o_ref[rows, :] = x + g_ref[...] * out


def _ffn(x, gain, mods, row_fn, wg, wu, wd, layer, *, tm, mixer_proj=None):
    m, d = x.shape
    sh, sc, g = mods
    in_specs = [
        pl.BlockSpec((tm, d), lambda i: (i, 0)),
        _resident((1, d)),
        _row_vec_spec(d, row_fn),
        _row_vec_spec(d, row_fn),
        _row_vec_spec(d, row_fn),
        _stacked(wg, layer),
        _stacked(wu, layer),
        _stacked(wd, layer),
    ]
    args = [x, gain, sh, sc, g, wg, wu, wd]
    if mixer_proj is not None:
        y, w_o, slot, gm = mixer_proj
        in_specs += [
            pl.BlockSpec((tm, y.shape[1]), lambda i: (i, 0)),
            _stacked(w_o, slot),
            _row_vec_spec(d, row_fn),
        ]
        args += [y, w_o, gm]
    return pl.pallas_call(
        functools.partial(_ffn_kernel, mixer_proj=mixer_proj is not None),
        grid=(m // tm,),
        in_specs=in_specs,
        out_specs=pl.BlockSpec((tm, d), lambda i: (i, 0)),
        out_shape=jax.ShapeDtypeStruct((m, d), F32),
        scratch_shapes=[pltpu.VMEM((tm, wg.shape[2]), BF16)],
        compiler_params=_params("parallel"),
        name="ffn_proj" if mixer_proj is not None else "ffn",
    )(*args)


def _axial_tables(n_tokens):
    rows = n_tokens // GRID_W
    row = np.repeat(np.arange(rows, dtype=np.float32), GRID_W)
    col = np.tile(np.arange(GRID_W, dtype=np.float32), rows)
    n_freq = HEAD_DIM // 4
    inv = jnp.asarray(ROPE_BASE, F32) ** (-jnp.arange(n_freq, dtype=F32) / n_freq)
    ang = jnp.concatenate([jnp.asarray(row)[:, None] * inv, jnp.asarray(col)[:, None] * inv], axis=-1)
    cos, sin = jnp.cos(ang), jnp.sin(ang)
    return jnp.tile(cos, (1, 4)), jnp.concatenate([-sin, -sin, sin, sin], axis=-1)


def _retention_rot_tables(n_tokens):
    inv = jnp.asarray(ROPE_BASE, F32) ** (-jnp.linspace(0.0, 1.0, RET_DK // 2, dtype=F32))
    ang = jnp.arange(n_tokens, dtype=F32)[:, None] * inv
    return jnp.cos(ang), jnp.sin(ang)


def kernel(x, c, ctx, c_ctx, ada_w, ada_b, norm_mix, norm_ffn, attn_w_qkv, attn_w_o, attn_q_norm, attn_k_norm,
           attn_sink, pool_w, pool_scale, ret_w_in, ret_w_o, ffn_w_gate, ffn_w_up, ffn_w_down):
    batch, seq, d = x.shape
    ctx_len = ctx.shape[1]
    mod_rows = 16
    tm_lat, tm_ctx, tm_pool = 1024, 256, 512
    assert batch < mod_rows and seq % GRID_W == 0
    assert seq % max(tm_lat, RET_BLOCK, ATTN_QB * ATTN_BLOCK) == 0 and ctx_len % tm_ctx == 0

    cvec = jnp.zeros((mod_rows, d), F32).at[:batch].set(c).at[batch].set(c_ctx)
    mods = _ada_mods(cvec, ada_w, ada_b)
    mods = mods.reshape(DEPTH, mod_rows, 6, 1, d)

    x_lat = x.reshape(batch * seq, d)
    x_ctx = ctx.reshape(batch * ctx_len, d)

    lat_row = lambda i: i // (seq // tm_lat)
    ctx_row = lambda i: batch

    attn_cos, attn_sin = _axial_tables(seq)
    ret_cos, ret_sin = _retention_rot_tables(seq)
    head_of_pair, _ = _pair_layout()
    same_head = (head_of_pair[:, None] == head_of_pair[None, :]).astype(np.float32)
    bd = jnp.asarray(np.concatenate([same_head, same_head], axis=0), BF16)
    half = HEAD_DIM // 2

    def pair_gain(g):
        return jnp.broadcast_to(g.reshape(2, 1, half), (2, 2, half)).reshape(1, LANES)

    nqk = (ATTN_HEADS + ATTN_KV_HEADS) * HEAD_DIM
    slots = attn_w_qkv.shape[0]
    w_qk = attn_w_qkv[:, :, :nqk].reshape(slots, d, nqk // LANES, 2, 2, half)
    w_qk = jnp.swapaxes(w_qk, 3, 4).reshape(slots, d, nqk)
    attn_qkv_w = jnp.concatenate([w_qk, attn_w_qkv[:, :, nqk:]], axis=-1).astype(BF16)
    attn_o_w = attn_w_o.astype(BF16)
    ret_in_w = ret_w_in.astype(BF16)
    ret_o_w = ret_w_o.astype(BF16)
    pool_w_bf = pool_w.astype(BF16)
    wg_all, wu_all, wd_all = ffn_w_gate.astype(BF16), ffn_w_up.astype(BF16), ffn_w_down.astype(BF16)

    for i in range(DEPTH):
        kind, slot = i % N_MIXERS, i // N_MIXERS
        need_ctx_out = i < DEPTH - 1
        mod = [mods[i, :, k] for k in range(6)]
        sh_m, sc_m, g_m, sh_f, sc_f, g_f = mod
        gain_m = norm_mix[i].reshape(1, d)
        gain_f = norm_ffn[i].reshape(1, d)
        proj_lat = proj_ctx = None

        if kind == 0:
            qg = pair_gain(attn_q_norm[slot])
            kg = pair_gain(attn_k_norm[slot])
            sink = attn_sink[slot].astype(F32) * LOG2E
            bound = (1.02 * LOG2E * HEAD_DIM ** 0.5 * jnp.max(jnp.abs(attn_q_norm[slot]))
                     * jnp.max(jnp.abs(attn_k_norm[slot]))).astype(F32).reshape(1)
            q_c, k_c, v_c = _attn_qkv(x_ctx, gain_m, (sh_m, sc_m), ctx_row, attn_qkv_w, slot, qg, kg, attn_cos,
                                      attn_sin, bd, rotate=False, tm=tm_ctx, tiles_per_seq=1)
            q_l, k_l, v_l = _attn_qkv(x_lat, gain_m, (sh_m, sc_m), lat_row, attn_qkv_w, slot, qg, kg, attn_cos,
                                      attn_sin, bd, rotate=True, tm=tm_lat, tiles_per_seq=seq // tm_lat)
            a_l = _attention(bound, sink, q_l, k_l, v_l, k_c, v_c, batch=batch, seq=seq, ctx_len=ctx_len, local=True)
            proj_lat = (a_l, attn_o_w, slot, g_m)
            if need_ctx_out:
                a_c = _attention(bound, sink, q_c, None, None, k_c, v_c, batch=batch, seq=ctx_len, ctx_len=ctx_len,
                                 local=False)
                proj_ctx = (a_c, attn_o_w, slot, g_m)
        elif kind == 1:
            w_p = pool_w_bf[slot]
            ls = pool_scale[slot].reshape(1, d)
            pool_row = lambda i: i // (seq // tm_pool)
            x_lat = _pool_mixer(x_lat, gain_m, (sh_m, sc_m, g_m), pool_row, w_p, ls, tm=tm_pool, seq=seq)
            if need_ctx_out:
                x_ctx = _pool_mixer(x_ctx, gain_m, (sh_m, sc_m, g_m), ctx_row, w_p, ls, tm=tm_ctx, seq=ctx_len)
        else:
            p_c = _ret_proj(x_ctx, gain_m, (sh_m, sc_m), ctx_row, ret_in_w, slot, ret_cos, ret_sin,
                            rotate=False, tm=tm_ctx, tiles_per_seq=1)
            yc_f, yc_b, s_c = _retention(p_c, None, batch=batch, seq=ctx_len, block=min(ctx_len, RET_BLOCK))
            p_l = _ret_proj(x_lat, gain_m, (sh_m, sc_m), lat_row, ret_in_w, slot, ret_cos, ret_sin,
                            rotate=True, tm=tm_lat, tiles_per_seq=seq // tm_lat)
            yl_f, yl_b, _ = _retention(p_l, s_c, batch=batch, seq=seq, block=RET_BLOCK)
            x_lat = _ret_out(x_lat, yl_f, yl_b, gain_m, (sh_m, sc_m, g_m), lat_row, ret_in_w, ret_o_w, slot,
                             tm=tm_lat)
            if need_ctx_out:
                x_ctx = _ret_out(x_ctx, yc_f, yc_b, gain_m, (sh_m, sc_m, g_m), ctx_row, ret_in_w, ret_o_w, slot,
                                 tm=tm_ctx)

        x_lat = _ffn(x_lat, gain_f, (sh_f, sc_f, g_f), lat_row, wg_all, wu_all, wd_all, i, tm=tm_lat,
                     mixer_proj=proj_lat)
        if need_ctx_out:
            x_ctx = _ffn(x_ctx, gain_f, (sh_f, sc_f, g_f), ctx_row, wg_all, wu_all, wd_all, i, tm=tm_ctx,
                         mixer_proj=proj_ctx)

    return x_lat.reshape(batch, seq, d)
```

```python
import functools

import jax
import jax.numpy as jnp
import numpy as np
from jax import lax
from jax.experimental import pallas as pl
from jax.experimental.pallas import tpu as pltpu

F32 = jnp.float32
BF16 = jnp.bfloat16

DEPTH = 4
GRID_W = 64
N_MIXERS = 3
ATTN_HEADS = 16
ATTN_KV_HEADS = 4
ATTN_GROUP = ATTN_HEADS // ATTN_KV_HEADS
HEAD_DIM = 64
WINDOW = 128
ATTN_BLOCK = 128
ROPE_BASE = 10000.0
NEG_INF = -1e30
POOL_WINDOWS = (2, 4, 8, 16)
POOL_HALO = 8
POOL_PAD = 16
RET_HEADS = 4
RET_DK = 256
RET_DV = 512
RET_SUB = 256
RET_BLOCK = 2048
RET_BWD_OFFSET = 0.5
NORM_EPS = 1e-6

VMEM_LIMIT_BYTES = 56 * 1024 * 1024
LANES = 128
ROW_SUB = 512
RETOUT_SUB = 256


def _params(*sem):
    return pltpu.CompilerParams(dimension_semantics=sem, vmem_limit_bytes=VMEM_LIMIT_BYTES)


def _silu(x):
    return x * jax.nn.sigmoid(x)


def _prenorm(x, gain, shift, scale):
    y = x * lax.rsqrt(jnp.mean(x * x, axis=-1, keepdims=True) + NORM_EPS)
    return y * (gain * (1.0 + scale)) + shift


def _resident(shape):
    nd = len(shape)
    return pl.BlockSpec(shape, lambda *_: (0,) * nd)


def _stacked(arr, index, col_block=0, cols=None):
    shape = arr.shape[1:] if cols is None else arr.shape[1:-1] + (cols,)
    nd = len(shape)
    return pl.BlockSpec((None,) + shape, lambda *_: (index,) + (0,) * (nd - 1) + (col_block,),
                        pipeline_mode=pl.Buffered(1))


def _ada_kernel(c_ref, w_ref, b_ref, o_ref):
    cond = _silu(c_ref[...])
    o_ref[...] = jnp.dot(cond.astype(BF16), w_ref[...].astype(BF16), preferred_element_type=F32) + b_ref[...]


def _ada_mods(cvec, ada_w, ada_b):
    depth, d, n = ada_w.shape
    rows = cvec.shape[0]
    tn = 1536
    return pl.pallas_call(
        _ada_kernel,
        grid=(depth, n // tn),
        in_specs=[
            pl.BlockSpec((rows, d), lambda l, j: (0, 0)),
            pl.BlockSpec((None, d, tn), lambda l, j: (l, 0, j)),
            pl.BlockSpec((None, 1, tn), lambda l, j: (l, 0, j)),
        ],
        out_specs=pl.BlockSpec((None, rows, tn), lambda l, j: (l, 0, j)),
        out_shape=jax.ShapeDtypeStruct((depth, rows, n), F32),
        compiler_params=_params("parallel", "parallel"),
        name="ada_mods",
    )(cvec, ada_w, ada_b.reshape(depth, 1, n))


def _row_vec_spec(d, row_fn):
    return pl.BlockSpec((None, 1, d), lambda i, *_: (row_fn(i), 0, 0))


def _qkv_kernel(x_ref, gain_ref, sh_ref, sc_ref, w_ref, qg_ref, kg_ref, cos_ref, sin_ref, bd_ref,
                q_ref, k_ref, v_ref, *, rotate):
    bd2 = bd_ref[...]
    qg = qg_ref[...]
    kg = kg_ref[...]
    nq = ATTN_HEADS * HEAD_DIM
    nk = ATTN_KV_HEADS * HEAD_DIM
    width = 2 * LANES

    def head_norm(xg, g):
        x2 = xg * xg
        hi = x2.astype(BF16)
        lo = (x2 - hi.astype(F32)).astype(BF16)
        ssq = jnp.dot(jnp.concatenate([hi, lo], axis=1), bd2, preferred_element_type=F32)
        return (xg * lax.rsqrt(ssq * (1.0 / HEAD_DIM) + NORM_EPS)) * g

    for r0 in range(0, x_ref.shape[0], ROW_SUB):
        rows = slice(r0, min(r0 + ROW_SUB, x_ref.shape[0]))
        h = _prenorm(x_ref[rows, :], gain_ref[...], sh_ref[...], sc_ref[...]).astype(BF16)

        def project(c0):
            return jnp.dot(h, w_ref[:, c0:c0 + width], preferred_element_type=F32)

        nxt = project(0)
        for j in range((nq + nk) // LANES):
            if j % 2 == 0:
                wide = nxt
                nxt = project((j + 2) * LANES)
            acc = wide[:, (j % 2) * LANES:(j % 2 + 1) * LANES]
            is_q = j < nq // LANES
            y = head_norm(acc, qg) * Q_SCALE if is_q else head_norm(acc, kg)
            if rotate:
                y = y * cos_ref[rows, :] + pltpu.roll(y, LANES // 2, 1) * sin_ref[rows, :]
            if is_q:
                q_ref[rows, j * LANES:(j + 1) * LANES] = y.astype(BF16)
            else:
                k_ref[rows, j * LANES - nq:(j + 1) * LANES - nq] = y.astype(BF16)
        v_ref[rows, :] = nxt.astype(BF16)


def _pair_layout():
    lane = np.arange(LANES)
    head_of_pair = (lane // (HEAD_DIM // 2)) % 2
    dim = lane % (HEAD_DIM // 2) + (HEAD_DIM // 2) * (lane // HEAD_DIM)
    return head_of_pair, dim


def _attn_qkv(x, gain, mods, row_fn, w, slot, qg, kg, cos_t, sin_t, bd, *, rotate, tm, tiles_per_seq):
    m, d = x.shape
    nq = ATTN_HEADS * HEAD_DIM
    nk = ATTN_KV_HEADS * HEAD_DIM
    sh, sc = mods
    tab_spec = pl.BlockSpec((tm, LANES), lambda i: (i % tiles_per_seq, 0))
    return pl.pallas_call(
        functools.partial(_qkv_kernel, rotate=rotate),
        grid=(m // tm,),
        in_specs=[
            pl.BlockSpec((tm, d), lambda i: (i, 0)),
            _resident((1, d)),
            _row_vec_spec(d, row_fn),
            _row_vec_spec(d, row_fn),
            _stacked(w, slot),
            _resident((1, LANES)),
            _resident((1, LANES)),
            tab_spec,
            tab_spec,
            _resident((2 * LANES, LANES)),
        ],
        out_specs=[
            pl.BlockSpec((tm, nq), lambda i: (i, 0)),
            pl.BlockSpec((tm, nk), lambda i: (i, 0)),
            pl.BlockSpec((tm, nk), lambda i: (i, 0)),
        ],
        out_shape=[
            jax.ShapeDtypeStruct((m, nq), BF16),
            jax.ShapeDtypeStruct((m, nk), BF16),
            jax.ShapeDtypeStruct((m, nk), BF16),
        ],
        compiler_params=_params("parallel"),
        name="attn_qkv_rot" if rotate else "attn_qkv",
    )(x, gain, sh, sc, w, qg, kg, cos_t, sin_t, bd)


ATTN_SHIFT_LIMIT = 30.0
LOG2E = 1.4426950408889634
Q_SCALE = HEAD_DIM ** -0.5 * LOG2E
ATTN_QB = 4


def _attn_heads(bound_ref, sink_ref, q_ref, q_rows, k_parts, v_parts, mask_ref, o_ref, *, use_bound):
    blk = q_rows.stop - q_rows.start
    lane = lax.broadcasted_iota(jnp.int32, (1, LANES), 1)
    low = lane < HEAD_DIM
    pair_bit = (lane // (HEAD_DIM // 2)) % 2
    one = jnp.ones((), BF16)
    zero = jnp.zeros((), BF16)

    def rows(parts, cols):
        pieces = [r[rs, cols] for r, rs in parts]
        return pieces[0] if len(pieces) == 1 else jnp.concatenate(pieces, axis=0)

    def scores(h):
        par = h % 2
        cols = slice((h // 2) * LANES, (h // 2 + 1) * LANES)
        own_qk = pair_bit == par
        k2 = rows(k_parts, cols)
        q_parts = []
        for g in range(ATTN_GROUP):
            hd = h * ATTN_GROUP + g
            qt = q_ref[q_rows, (hd // 2) * LANES:(hd // 2 + 1) * LANES]
            if hd % 2 != par:
                qt = pltpu.roll(qt, HEAD_DIM // 2 if par == 1 else LANES - HEAD_DIM // 2, 1)
            q_parts.append(jnp.where(own_qk, qt, zero))
        q2 = jnp.concatenate(q_parts, axis=0)
        return lax.dot_general(q2, k2, (((1,), (1,)), ((), ())), preferred_element_type=F32)

    def finish(h, s):
        par = h % 2
        cols = slice((h // 2) * LANES, (h // 2 + 1) * LANES)
        own_v = low if par == 0 else jnp.logical_not(low)
        v2 = jnp.where(own_v, rows(v_parts, cols), one)
        heads = [h * ATTN_GROUP + g for g in range(ATTN_GROUP)]
        if mask_ref is not None:
            b = ATTN_BLOCK
            s = jnp.concatenate([s[:, :b] + mask_ref[:, :b], s[:, b:2 * b],
                                 s[:, 2 * b:3 * b] + mask_ref[:, b:], s[:, 3 * b:]], axis=1)
        if use_bound:
            shifts = [jnp.maximum(bound_ref[0], sink_ref[hd]) for hd in heads]
            shift = jnp.concatenate([jnp.full((blk, 1), sh, F32) for sh in shifts], axis=0)
            es = [jnp.exp2(jnp.full((1, LANES), sink_ref[hd] - sh, F32)) for hd, sh in zip(heads, shifts)]
        else:
            sink = jnp.concatenate([jnp.full((blk, 1), sink_ref[hd], F32) for hd in heads], axis=0)
            shift = jnp.maximum(jnp.max(s, axis=1, keepdims=True), sink)
            es_all = jnp.exp2(sink - shift)
            es = [es_all[g * blk:(g + 1) * blk] for g in range(ATTN_GROUP)]
        p = jnp.exp2(s - shift).astype(BF16)
        oa = jnp.dot(p, v2, preferred_element_type=F32)
        res = []
        for g, hd in enumerate(heads):
            a = oa[g * blk:(g + 1) * blk]
            r = pltpu.roll(a, HEAD_DIM, 1)
            res.append(a / (r + es[g]) if hd % 2 == par else r / (a + es[g]))
        for t in range(ATTN_GROUP // 2):
            tile = jnp.where(low, res[2 * t], res[2 * t + 1])
            c0 = (heads[2 * t] // 2) * LANES
            o_ref[q_rows, c0:c0 + LANES] = tile.astype(BF16)

    s_next = scores(0)
    for h in range(ATTN_KV_HEADS):
        s = s_next
        if h + 1 < ATTN_KV_HEADS:
            s_next = scores(h + 1)
        finish(h, s)


def _attn_kernel(bound_ref, sink_ref, q_ref, *refs, local):
    full = slice(None)
    if local:
        kp_ref, kc_ref, kn_ref, vp_ref, vc_ref, vn_ref, kx_ref, vx_ref = refs[:8]
        mask_refs = refs[8:8 + ATTN_QB]
        o_ref = refs[8 + ATTN_QB]
        blocks = [slice(j * ATTN_BLOCK, (j + 1) * ATTN_BLOCK) for j in range(ATTN_QB)]
        k_blocks = [(kp_ref, full)] + [(kc_ref, rs) for rs in blocks] + [(kn_ref, full)]
        v_blocks = [(vp_ref, full)] + [(vc_ref, rs) for rs in blocks] + [(vn_ref, full)]
        work = [(blocks[j], k_blocks[j:j + 3] + [(kx_ref, full)], v_blocks[j:j + 3] + [(vx_ref, full)], mask_refs[j])
                for j in range(ATTN_QB)]
    else:
        kx_ref, vx_ref, o_ref = refs
        work = [(slice(0, q_ref.shape[0]), [(kx_ref, full)], [(vx_ref, full)], None)]
    use_bound = bound_ref[0] <= ATTN_SHIFT_LIMIT * LOG2E

    def run(flag):
        for q_rows, k_parts, v_parts, mask_ref in work:
            _attn_heads(bound_ref, sink_ref, q_ref, q_rows, k_parts, v_parts, mask_ref, o_ref, use_bound=flag)

    @pl.when(use_bound)
    def _():
        run(True)

    @pl.when(jnp.logical_not(use_bound))
    def _():
        run(False)


def _attn_mask_table():
    span = 3 * ATTN_BLOCK
    offs = np.arange(span) - ATTN_BLOCK
    rel = offs[None, :] - np.arange(ATTN_BLOCK)[:, None]
    near = np.abs(rel) <= WINDOW
    tabs = []
    for variant in range(3):
        ok = near.copy()
        if variant == 0:
            ok[:, :ATTN_BLOCK] = False
        if variant == 2:
            ok[:, 2 * ATTN_BLOCK:] = False
        tabs.append(np.where(ok, 0.0, NEG_INF).astype(np.float32))
    return np.stack(tabs)


def _attention(bound, sink, q, k, v, kx, vx, *, batch, seq, ctx_len, local):
    nblk = seq // ATTN_BLOCK
    nq = ATTN_HEADS * HEAD_DIM
    nk = ATTN_KV_HEADS * HEAD_DIM
    qb = ATTN_QB if local else 1
    nstep = nblk // qb
    q_spec = pl.BlockSpec((qb * ATTN_BLOCK, nq), lambda b, i: (b * nstep + i, 0))
    x_spec = pl.BlockSpec((ctx_len, nk), lambda b, i: (b, 0))
    smem = pl.BlockSpec(memory_space=pltpu.SMEM)
    if local:
        local_mask = _attn_mask_table()
        mask = np.concatenate([local_mask[:, :, :ATTN_BLOCK], local_mask[:, :, 2 * ATTN_BLOCK:]], axis=2)
        mask = jnp.asarray(np.tile(mask, (1, ATTN_GROUP, 1)))
        prev = pl.BlockSpec((ATTN_BLOCK, nk), lambda b, i: (b * nblk + jnp.maximum(qb * i - 1, 0), 0))
        cur = pl.BlockSpec((qb * ATTN_BLOCK, nk), lambda b, i: (b * nstep + i, 0))
        nxt = pl.BlockSpec((ATTN_BLOCK, nk), lambda b, i: (b * nblk + jnp.minimum(qb * (i + 1), nblk - 1), 0))

        def mask_spec(j):
            def index(b, i):
                g = qb * i + j
                return (jnp.where(g == 0, 0, jnp.where(g == nblk - 1, 2, 1)), 0, 0)
            return pl.BlockSpec((None,) + mask.shape[1:], index)

        in_specs = ([smem, smem, q_spec, prev, cur, nxt, prev, cur, nxt, x_spec, x_spec]
                    + [mask_spec(j) for j in range(qb)])
        args = (bound, sink, q, k, k, k, v, v, v, kx, vx) + (mask,) * qb
    else:
        in_specs = [smem, smem, q_spec, x_spec, x_spec]
        args = (bound, sink, q, kx, vx)
    return pl.pallas_call(
        functools.partial(_attn_kernel, local=local),
        grid=(batch, nstep),
        in_specs=in_specs,
        out_specs=q_spec,
        out_shape=jax.ShapeDtypeStruct(q.shape, BF16),
        compiler_params=_params("parallel", "parallel"),
        name="attn_local" if local else "attn_ctx",
    )(*args)


def _pool_kernel(xp_ref, x_ref, xn_ref, gain_ref, sh_ref, sc_ref, g_ref, w_ref, ls_ref, inv_ref, o_ref,
                 h_ref, pair_ref, quad_ref, oct_ref, *, tiles_per_seq):
    tm, d = x_ref.shape
    it = pl.program_id(0) % tiles_per_seq
    gain, sh, sc = gain_ref[...], sh_ref[...], sc_ref[...]
    hp = jnp.where(it > 0, _prenorm(xp_ref[...], gain, sh, sc), 0.0)
    hn = jnp.where(it < tiles_per_seq - 1, _prenorm(xn_ref[...], gain, sh, sc), 0.0)
    x = x_ref[...]
    base = POOL_HALO
    h_ref[0:base, :] = hp
    h_ref[base:base + tm, :] = _prenorm(x, gain, sh, sc)
    h_ref[base + tm:base + tm + POOL_HALO, :] = hn
    h_ref[base + tm + POOL_HALO:, :] = jnp.zeros((POOL_PAD, d), F32)
    gate_scale = g_ref[...] * ls_ref[...]
    gd = d // len(POOL_WINDOWS)
    n_pair, n_quad, n_oct = pair_ref.shape[0], quad_ref.shape[0], oct_ref.shape[0]
    for gi, win in enumerate(POOL_WINDOWS):
        cols = slice(gi * gd, (gi + 1) * gd)
        if win >= 4:
            pair_ref[:, cols] = h_ref[0:n_pair, cols] + h_ref[1:n_pair + 1, cols]
        if win >= 8:
            quad_ref[:, cols] = pair_ref[0:n_quad, cols] + pair_ref[2:n_quad + 2, cols]
        if win == 2:
            tot = h_ref[base - 1:base - 1 + tm, cols] + h_ref[base:base + tm, cols]
        elif win == 4:
            tot = pair_ref[base - 2:base - 2 + tm, cols] + pair_ref[base:base + tm, cols]
        elif win == 8:
            tot = quad_ref[base - 4:base - 4 + tm, cols] + quad_ref[base:base + tm, cols]
        else:
            oct_ref[:, cols] = quad_ref[0:n_oct, cols] + quad_ref[4:n_oct + 4, cols]
            tot = oct_ref[0:tm, cols] + oct_ref[base:base + tm, cols]
        pooled = tot * inv_ref[:, gi:gi + 1] - h_ref[base:base + tm, cols]
        y = jnp.dot(pooled.astype(BF16), w_ref[gi], preferred_element_type=F32)
        o_ref[:, cols] = x[:, cols] + gate_scale[:, cols] * y


def _pool_inv_counts(seq):
    t = np.arange(seq)
    cols = [1.0 / (np.minimum(t + w // 2, seq) - np.maximum(t - w // 2, 0)) for w in POOL_WINDOWS]
    return jnp.asarray(np.stack(cols, axis=1).astype(np.float32))


def _pool_mixer(x, gain, mods, row_fn, w, layer_scale, *, tm, seq):
    m, d = x.shape
    sh, sc, g = mods
    tiles_per_seq = seq // tm
    hb = tm // POOL_HALO
    nhalo = m // POOL_HALO
    return pl.pallas_call(
        functools.partial(_pool_kernel, tiles_per_seq=tiles_per_seq),
        grid=(m // tm,),
        in_specs=[
            pl.BlockSpec((POOL_HALO, d), lambda i: (jnp.maximum(i * hb - 1, 0), 0)),
            pl.BlockSpec((tm, d), lambda i: (i, 0)),
            pl.BlockSpec((POOL_HALO, d), lambda i: (jnp.minimum((i + 1) * hb, nhalo - 1), 0)),
            _resident((1, d)),
            _row_vec_spec(d, row_fn),
            _row_vec_spec(d, row_fn),
            _row_vec_spec(d, row_fn),
            _resident(w.shape),
            _resident((1, d)),
            pl.BlockSpec((tm, len(POOL_WINDOWS)), lambda i: (i % tiles_per_seq, 0)),
        ],
        out_specs=pl.BlockSpec((tm, d), lambda i: (i, 0)),
        out_shape=jax.ShapeDtypeStruct((m, d), F32),
        scratch_shapes=[pltpu.VMEM((tm + 2 * POOL_HALO + POOL_PAD, d), F32),
                        pltpu.VMEM((tm + 3 * POOL_HALO, d), F32),
                        pltpu.VMEM((tm + 2 * POOL_HALO, d), F32),
                        pltpu.VMEM((tm + POOL_HALO, d), F32)],
        compiler_params=_params("parallel"),
        name="pool_mixer",
    )(x, x, x, gain, sh, sc, g, w, layer_scale, _pool_inv_counts(seq))


RET_TN = 512
RET_QK_COLS = 2 * RET_HEADS * RET_DK
RET_V_COLS = RET_HEADS * RET_DV
RET_QKV_COLS = RET_QK_COLS + RET_V_COLS


def _retproj_kernel(x_ref, gain_ref, sh_ref, sc_ref, w_ref, cos_ref, sin_ref, o_ref, *, rotate):
    half = RET_DK // 2
    q_cols = RET_HEADS * RET_DK
    for r0 in range(0, x_ref.shape[0], ROW_SUB):
        rows = slice(r0, min(r0 + ROW_SUB, x_ref.shape[0]))
        h = _prenorm(x_ref[rows, :], gain_ref[...], sh_ref[...], sc_ref[...]).astype(BF16)
        for c0 in range(0, RET_QKV_COLS, RET_TN):
            acc = jnp.dot(h, w_ref[:, c0:c0 + RET_TN], preferred_element_type=F32)
            if c0 >= RET_QK_COLS:
                o_ref[rows, c0:c0 + RET_TN] = acc.astype(BF16)
                continue
            scale = 1.0 if c0 < q_cols else RET_DK ** -0.5
            for hd in range(RET_TN // RET_DK):
                x1 = acc[:, hd * RET_DK:hd * RET_DK + half]
                x2 = acc[:, hd * RET_DK + half:(hd + 1) * RET_DK]
                if rotate:
                    cos, sin = cos_ref[rows, :], sin_ref[rows, :]
                    x1, x2 = x1 * cos - x2 * sin, x2 * cos + x1 * sin
                o_ref[rows, c0 + hd * RET_DK:c0 + hd * RET_DK + half] = (x1 * scale).astype(BF16)
                o_ref[rows, c0 + hd * RET_DK + half:c0 + (hd + 1) * RET_DK] = (x2 * scale).astype(BF16)


def _ret_proj(x, gain, mods, row_fn, w_in, slot, cos_t, sin_t, *, rotate, tm, tiles_per_seq):
    m, d = x.shape
    n = RET_QKV_COLS
    sh, sc = mods
    tab_spec = pl.BlockSpec((tm, RET_DK // 2), lambda i: (i % tiles_per_seq, 0))
    return pl.pallas_call(
        functools.partial(_retproj_kernel, rotate=rotate),
        grid=(m // tm,),
        in_specs=[
            pl.BlockSpec((tm, d), lambda i: (i, 0)),
            _resident((1, d)),
            _row_vec_spec(d, row_fn),
            _row_vec_spec(d, row_fn),
            _stacked(w_in, slot, col_block=0, cols=n),
            tab_spec,
            tab_spec,
        ],
        out_specs=pl.BlockSpec((tm, n), lambda i: (i, 0)),
        out_shape=jax.ShapeDtypeStruct((m, n), BF16),
        compiler_params=_params("parallel"),
        name="ret_proj_rot" if rotate else "ret_proj",
    )(x, gain, sh, sc, w_in, cos_t, sin_t)


def _ret_kernel(q_ref, k_ref, v_ref, intra_ref, qdec_ref, kdec_ref, cdec_ref, *rest, sub, zero_init):
    if zero_init:
        yf_ref, yb_ref, sout_ref, state_ref = rest
    else:
        s0_ref, yf_ref, yb_ref, sout_ref, state_ref = rest
    c = pl.program_id(2)
    hd = pl.program_id(1)

    @pl.when(c == 0)
    def _():
        state_ref[...] = jnp.zeros_like(state_ref) if zero_init else s0_ref[...]

    block = yf_ref.shape[0]
    n_sub = block // sub
    y_refs = (yf_ref, yb_ref)
    starts = (c * block, (pl.num_programs(2) - 1 - c) * block)
    steps = []
    for j in range(n_sub):
        steps.append((0, j * sub))
        steps.append((1, (n_sub - 1 - j) * sub))

    def seq_rows(d, r0):
        return pl.ds(pl.multiple_of(starts[d] + r0, sub), sub)

    decayed = []
    for d, r0 in steps:
        rows = seq_rows(d, r0)
        sc = lax.dot_general(q_ref[rows, :], k_ref[rows, :], (((1,), (1,)), ((), ())), preferred_element_type=F32)
        decayed.append((sc * intra_ref[d]).astype(BF16))

    for (d, r0), sc in zip(steps, decayed):
        rows = seq_rows(d, r0)
        y_ref = y_refs[d]
        q, k, v = q_ref[rows, :], k_ref[rows, :], v_ref[rows, :]
        s = state_ref[d]
        y = (jnp.dot(sc, v, preferred_element_type=F32)
             + jnp.dot(q, s.astype(BF16), preferred_element_type=F32) * qdec_ref[d])
        kd = (k.astype(F32) * kdec_ref[d]).astype(BF16)
        state_ref[d] = s * cdec_ref[d * RET_HEADS + hd] + lax.dot_general(
            kd, v, (((0,), (0,)), ((), ())), preferred_element_type=F32)
        y_ref[r0:r0 + sub, :] = y.astype(BF16)

    @pl.when(c == pl.num_programs(2) - 1)
    def _():
        sout_ref[...] = state_ref[...]


def _ret_tables(chunk):
    heads = np.arange(RET_HEADS, dtype=np.float64)
    log_g = [np.log1p(-np.exp2(-5.0 - heads)), np.log1p(-np.exp2(-5.0 - RET_BWD_OFFSET - heads))]
    pos = np.arange(chunk, dtype=np.float64)
    diff = pos[:, None] - pos[None, :]
    intra, qdec, kdec, cdec = [], [], [], []
    for d, lg in enumerate(log_g):
        lg3 = lg[:, None, None]
        if d == 0:
            intra.append(np.where(diff >= 0, np.exp(np.maximum(diff, 0.0)[None] * lg3), 0.0))
            qdec.append(np.exp((pos[None, :] + 1.0) * lg[:, None]))
            kdec.append(np.exp((chunk - 1.0 - pos)[None, :] * lg[:, None]))
        else:
            intra.append(np.where(diff <= 0, np.exp(np.maximum(-diff, 0.0)[None] * lg3), 0.0))
            qdec.append(np.exp((chunk - pos)[None, :] * lg[:, None]))
            kdec.append(np.exp(pos[None, :] * lg[:, None]))
        cdec.append(np.exp(chunk * lg))
    intra = np.stack(intra).astype(np.float32)
    qdec = np.stack(qdec).astype(np.float32)[..., None]
    kdec = np.stack(kdec).astype(np.float32)[..., None]
    cdec = np.stack(cdec).astype(np.float32).reshape(-1)
    return jnp.asarray(intra), jnp.asarray(qdec), jnp.asarray(kdec), jnp.asarray(cdec)


def _retention(proj, s0, *, batch, seq, block):
    sub = min(block, RET_SUB)
    intra, qdec, kdec, cdec = _ret_tables(sub)
    n = seq // block
    kq = RET_HEADS
    kv = RET_QK_COLS // RET_DV

    def whole_seq(off, width):
        return pl.BlockSpec((seq, width), lambda b, h, c: (b, off + h))

    tab = lambda a: pl.BlockSpec((2, None) + a.shape[2:], lambda b, h, c: (0, h) + (0,) * (a.ndim - 2))
    state_spec = pl.BlockSpec((None, None, 2, RET_DK, RET_DV), lambda b, h, c: (b, h, 0, 0, 0))
    y_shape = jax.ShapeDtypeStruct((batch * seq, RET_V_COLS), BF16)
    return pl.pallas_call(
        functools.partial(_ret_kernel, sub=sub, zero_init=s0 is None),
        grid=(batch, RET_HEADS, n),
        in_specs=[
            whole_seq(0, RET_DK), whole_seq(kq, RET_DK), whole_seq(kv, RET_DV),
            tab(intra), tab(qdec), tab(kdec),
            pl.BlockSpec(memory_space=pltpu.SMEM),
        ] + ([] if s0 is None else [state_spec]),
        out_specs=[
            pl.BlockSpec((block, RET_DV), lambda b, h, c: (b * n + c, h)),
            pl.BlockSpec((block, RET_DV), lambda b, h, c: (b * n + n - 1 - c, h)),
            state_spec,
        ],
        out_shape=[y_shape, y_shape, jax.ShapeDtypeStruct((batch, RET_HEADS, 2, RET_DK, RET_DV), F32)],
        scratch_shapes=[pltpu.VMEM((2, RET_DK, RET_DV), F32)],
        compiler_params=_params("parallel", "parallel", "arbitrary"),
        name="retention",
    )(*((proj,) * 3 + (intra, qdec, kdec, cdec) + (() if s0 is None else (s0,))))


def _rms_unit(y):
    y = y.astype(F32)
    return y * lax.rsqrt(jnp.mean(y * y, axis=-1, keepdims=True) + NORM_EPS)


def _retout_kernel(x_ref, yf_ref, yb_ref, gain_ref, sh_ref, sc_ref, g_ref, wg_ref, wo_ref, o_ref, a_ref):
    hv = RET_V_COLS
    for r0 in range(0, x_ref.shape[0], RETOUT_SUB):
        rows = slice(r0, min(r0 + RETOUT_SUB, x_ref.shape[0]))
        x = x_ref[rows, :]
        h = _prenorm(x, gain_ref[...], sh_ref[...], sc_ref[...]).astype(BF16)
        for c0 in range(0, hv, RET_DV):
            cols = slice(c0, c0 + RET_DV)
            gf = jnp.dot(h, wg_ref[:, c0:c0 + RET_DV], preferred_element_type=F32)
            gb = jnp.dot(h, wg_ref[:, hv + c0:hv + c0 + RET_DV], preferred_element_type=F32)
            y = _silu(gf) * _rms_unit(yf_ref[rows, cols]) + _silu(gb) * _rms_unit(yb_ref[rows, cols])
            a_ref[rows, cols] = y.astype(BF16)
        out = jnp.dot(a_ref[rows, :], wo_ref[...], preferred_element_type=F32)
        o_ref[rows, :] = x + g_ref[...] * out


def _ret_out(x, yf, yb, gain, mods, row_fn, w_in, w_o, slot, *, tm):
    m, d = x.shape
    hv = RET_V_COLS
    sh, sc, g = mods
    return pl.pallas_call(
        _retout_kernel,
        grid=(m // tm,),
        in_specs=[
            pl.BlockSpec((tm, d), lambda i: (i, 0)),
            pl.BlockSpec((tm, hv), lambda i: (i, 0)),
            pl.BlockSpec((tm, hv), lambda i: (i, 0)),
            _resident((1, d)),
            _row_vec_spec(d, row_fn),
            _row_vec_spec(d, row_fn),
            _row_vec_spec(d, row_fn),
            _stacked(w_in, slot, col_block=1, cols=2 * hv),
            _stacked(w_o, slot),
        ],
        out_specs=pl.BlockSpec((tm, d), lambda i: (i, 0)),
        out_shape=jax.ShapeDtypeStruct((m, d), F32),
        scratch_shapes=[pltpu.VMEM((tm, hv), BF16)],
        compiler_params=_params("parallel"),
        name="ret_out",
    )(x, yf, yb, gain, sh, sc, g, w_in, w_o)


FFN_CHUNK = 256


def _ffn_kernel(x_ref, gain_ref, sh_ref, sc_ref, g_ref, wg_ref, wu_ref, wd_ref, *rest, mixer_proj):
    if mixer_proj:
        y_ref, wo_ref, gm_ref, o_ref, a_ref = rest
    else:
        o_ref, a_ref = rest
    for r0 in range(0, x_ref.shape[0], ROW_SUB):
        rows = slice(r0, min(r0 + ROW_SUB, x_ref.shape[0]))
        x = x_ref[rows, :]
        if mixer_proj:
            x = x + gm_ref[...] * jnp.dot(y_ref[rows, :], wo_ref[...], preferred_element_type=F32)
        h = _prenorm(x, gain_ref[...], sh_ref[...], sc_ref[...]).astype(BF16)
        for c0 in range(0, wg_ref.shape[1], FFN_CHUNK):
            cols = slice(c0, c0 + FFN_CHUNK)
            gate = jnp.dot(h, wg_ref[:, cols], preferred_element_type=F32)
            up = jnp.dot(h, wu_ref[:, cols], preferred_element_type=F32)
            a_ref[rows, cols] = (_silu(gate) * up).astype(BF16)
        out = jnp.dot(a_ref[rows, :], wd_ref[...], preferred_element_type=F32)
        o_ref[rows, :] = x + g_ref[...] * out


def _ffn(x, gain, mods, row_fn, wg, wu, wd, layer, *, tm, mixer_proj=None):
    m, d = x.shape
    sh, sc, g = mods
    in_specs = [
        pl.BlockSpec((tm, d), lambda i: (i, 0)),
        _resident((1, d)),
        _row_vec_spec(d, row_fn),
        _row_vec_spec(d, row_fn),
        _row_vec_spec(d, row_fn),
        _stacked(wg, layer),
        _stacked(wu, layer),
        _stacked(wd, layer),
    ]
    args = [x, gain, sh, sc, g, wg, wu, wd]
    if mixer_proj is not None:
        y, w_o, slot, gm = mixer_proj
        in_specs += [
            pl.BlockSpec((tm, y.shape[1]), lambda i: (i, 0)),
            _stacked(w_o, slot),
            _row_vec_spec(d, row_fn),
        ]
        args += [y, w_o, gm]
    return pl.pallas_call(
        functools.partial(_ffn_kernel, mixer_proj=mixer_proj is not None),
        grid=(m // tm,),
        in_specs=in_specs,
        out_specs=pl.BlockSpec((tm, d), lambda i: (i, 0)),
        out_shape=jax.ShapeDtypeStruct((m, d), F32),
        scratch_shapes=[pltpu.VMEM((tm, wg.shape[2]), BF16)],
        compiler_params=_params("parallel"),
        name="ffn_proj" if mixer_proj is not None else "ffn",
    )(*args)


def _axial_tables(n_tokens):
    rows = n_tokens // GRID_W
    row = np.repeat(np.arange(rows, dtype=np.float32), GRID_W)
    col = np.tile(np.arange(GRID_W, dtype=np.float32), rows)
    n_freq = HEAD_DIM // 4
    inv = jnp.asarray(ROPE_BASE, F32) ** (-jnp.arange(n_freq, dtype=F32) / n_freq)
    ang = jnp.concatenate([jnp.asarray(row)[:, None] * inv, jnp.asarray(col)[:, None] * inv], axis=-1)
    cos, sin = jnp.cos(ang), jnp.sin(ang)
    return jnp.tile(cos, (1, 4)), jnp.concatenate([-sin, -sin, sin, sin], axis=-1)


def _retention_rot_tables(n_tokens):
    inv = jnp.asarray(ROPE_BASE, F32) ** (-jnp.linspace(0.0, 1.0, RET_DK // 2, dtype=F32))
    ang = jnp.arange(n_tokens, dtype=F32)[:, None] * inv
    return jnp.cos(ang), jnp.sin(ang)


def kernel(x, c, ctx, c_ctx, ada_w, ada_b, norm_mix, norm_ffn, attn_w_qkv, attn_w_o, attn_q_norm, attn_k_norm,
           attn_sink, pool_w, pool_scale, ret_w_in, ret_w_o, ffn_w_gate, ffn_w_up, ffn_w_down):
    batch, seq, d = x.shape
    ctx_len = ctx.shape[1]
    mod_rows = 16
    tm_lat, tm_ctx, tm_pool = 1024, 256, 512
    assert batch < mod_rows and seq % GRID_W == 0
    assert seq % max(tm_lat, RET_BLOCK, ATTN_QB * ATTN_BLOCK) == 0 and ctx_len % tm_ctx == 0

    cvec = jnp.zeros((mod_rows, d), F32).at[:batch].set(c).at[batch].set(c_ctx)
    mods = _ada_mods(cvec, ada_w, ada_b)
    mods = mods.reshape(DEPTH, mod_rows, 6, 1, d)

    x_lat = x.reshape(batch * seq, d)
    x_ctx = ctx.reshape(batch * ctx_len, d)

    lat_row = lambda i: i // (seq // tm_lat)
    ctx_row = lambda i: batch

    attn_cos, attn_sin = _axial_tables(seq)
    ret_cos, ret_sin = _retention_rot_tables(seq)
    head_of_pair, _ = _pair_layout()
    same_head = (head_of_pair[:, None] == head_of_pair[None, :]).astype(np.float32)
    bd = jnp.asarray(np.concatenate([same_head, same_head], axis=0), BF16)
    half = HEAD_DIM // 2

    def pair_gain(g):
        return jnp.broadcast_to(g.reshape(2, 1, half), (2, 2, half)).reshape(1, LANES)

    nqk = (ATTN_HEADS + ATTN_KV_HEADS) * HEAD_DIM
    slots = attn_w_qkv.shape[0]
    w_qk = attn_w_qkv[:, :, :nqk].reshape(slots, d, nqk // LANES, 2, 2, half)
    w_qk = jnp.swapaxes(w_qk, 3, 4).reshape(slots, d, nqk)
    attn_qkv_w = jnp.concatenate([w_qk, attn_w_qkv[:, :, nqk:]], axis=-1).astype(BF16)
    attn_o_w = attn_w_o.astype(BF16)
    ret_in_w = ret_w_in.astype(BF16)
    ret_o_w = ret_w_o.astype(BF16)
    pool_w_bf = pool_w.astype(BF16)
    wg_all, wu_all, wd_all = ffn_w_gate.astype(BF16), ffn_w_up.astype(BF16), ffn_w_down.astype(BF16)

    for i in range(DEPTH):
        kind, slot = i % N_MIXERS, i // N_MIXERS
        need_ctx_out = i < DEPTH - 1
        mod = [mods[i, :, k] for k in range(6)]
        sh_m, sc_m, g_m, sh_f, sc_f, g_f = mod
        gain_m = norm_mix[i].reshape(1, d)
        gain_f = norm_ffn[i].reshape(1, d)
        proj_lat = proj_ctx = None

        if kind == 0:
            qg = pair_gain(attn_q_norm[slot])
            kg = pair_gain(attn_k_norm[slot])
            sink = attn_sink[slot].astype(F32) * LOG2E
            bound = (1.02 * LOG2E * HEAD_DIM ** 0.5 * jnp.max(jnp.abs(attn_q_norm[slot]))
                     * jnp.max(jnp.abs(attn_k_norm[slot]))).astype(F32).reshape(1)
            q_c, k_c, v_c = _attn_qkv(x_ctx, gain_m, (sh_m, sc_m), ctx_row, attn_qkv_w, slot, qg, kg, attn_cos,
                                      attn_sin, bd, rotate=False, tm=tm_ctx, tiles_per_seq=1)
            q_l, k_l, v_l = _attn_qkv(x_lat, gain_m, (sh_m, sc_m), lat_row, attn_qkv_w, slot, qg, kg, attn_cos,
                                      attn_sin, bd, rotate=True, tm=tm_lat, tiles_per_seq=seq // tm_lat)
            a_l = _attention(bound, sink, q_l, k_l, v_l, k_c, v_c, batch=batch, seq=seq, ctx_len=ctx_len, local=True)
            proj_lat = (a_l, attn_o_w, slot, g_m)
            if need_ctx_out:
                a_c = _attention(bound, sink, q_c, None, None, k_c, v_c, batch=batch, seq=ctx_len, ctx_len=ctx_len,
                                 local=False)
                proj_ctx = (a_c, attn_o_w, slot, g_m)
        elif kind == 1:
            w_p = pool_w_bf[slot]
            ls = pool_scale[slot].reshape(1, d)
            pool_row = lambda i: i // (seq // tm_pool)
            x_lat = _pool_mixer(x_lat, gain_m, (sh_m, sc_m, g_m), pool_row, w_p, ls, tm=tm_pool, seq=seq)
            if need_ctx_out:
                x_ctx = _pool_mixer(x_ctx, gain_m, (sh_m, sc_m, g_m), ctx_row, w_p, ls, tm=tm_ctx, seq=ctx_len)
        else:
            p_c = _ret_proj(x_ctx, gain_m, (sh_m, sc_m), ctx_row, ret_in_w, slot, ret_cos, ret_sin,
                            rotate=False, tm=tm_ctx, tiles_per_seq=1)
            yc_f, yc_b, s_c = _retention(p_c, None, batch=batch, seq=ctx_len, block=min(ctx_len, RET_BLOCK))
            p_l = _ret_proj(x_lat, gain_m, (sh_m, sc_m), lat_row, ret_in_w, slot, ret_cos, ret_sin,
                            rotate=True, tm=tm_lat, tiles_per_seq=seq // tm_lat)
            yl_f, yl_b, _ = _retention(p_l, s_c, batch=batch, seq=seq, block=RET_BLOCK)
            x_lat = _ret_out(x_lat, yl_f, yl_b, gain_m, (sh_m, sc_m, g_m), lat_row, ret_in_w, ret_o_w, slot,
                             tm=tm_lat)
            if need_ctx_out:
                x_ctx = _ret_out(x_ctx, yc_f, yc_b, gain_m, (sh_m, sc_m, g_m), ctx_row, ret_in_w, ret_o_w, slot,
                                 tm=tm_ctx)

        x_lat = _ffn(x_lat, gain_f, (sh_f, sc_f, g_f), lat_row, wg_all, wu_all, wd_all, i, tm=tm_lat,
                     mixer_proj=proj_lat)
        if need_ctx_out:
            x_ctx = _ffn(x_ctx, gain_f, (sh_f, sc_f, g_f), ctx_row, wg_all, wu_all, wd_all, i, tm=tm_ctx,
                         mixer_proj=proj_ctx)

    return x_lat.reshape(batch, seq, d)
```

```python
import functools

import jax
import jax.numpy as jnp
import numpy as np
from jax import lax
from jax.experimental import pallas as pl
from jax.experimental.pallas import tpu as pltpu

F32 = jnp.float32
BF16 = jnp.bfloat16

DEPTH = 4
GRID_W = 64
N_MIXERS = 3
ATTN_HEADS = 16
ATTN_KV_HEADS = 4
ATTN_GROUP = ATTN_HEADS // ATTN_KV_HEADS
HEAD_DIM = 64
WINDOW = 128
ATTN_BLOCK = 128
ROPE_BASE = 10000.0
NEG_INF = -1e30
POOL_WINDOWS = (2, 4, 8, 16)
POOL_HALO = 8
POOL_PAD = 16
RET_HEADS = 4
RET_DK = 256
RET_DV = 512
RET_SUB = 256
RET_BLOCK = 2048
RET_BWD_OFFSET = 0.5
NORM_EPS = 1e-6

VMEM_LIMIT_BYTES = 56 * 1024 * 1024
LANES = 128
ROW_SUB = 512
RETOUT_SUB = 256


def _params(*sem):
    return pltpu.CompilerParams(dimension_semantics=sem, vmem_limit_bytes=VMEM_LIMIT_BYTES)


def _silu(x):
    return x * jax.nn.sigmoid(x)


def _prenorm(x, gain, shift, scale):
    y = x * lax.rsqrt(jnp.mean(x * x, axis=-1, keepdims=True) + NORM_EPS)
    return y * (gain * (1.0 + scale)) + shift


def _resident(shape):
    nd = len(shape)
    return pl.BlockSpec(shape, lambda *_: (0,) * nd)


def _stacked(arr, index, col_block=0, cols=None):
    shape = arr.shape[1:] if cols is None else arr.shape[1:-1] + (cols,)
    nd = len(shape)
    return pl.BlockSpec((None,) + shape, lambda *_: (index,) + (0,) * (nd - 1) + (col_block,),
                        pipeline_mode=pl.Buffered(1))


def _ada_kernel(c_ref, w_ref, b_ref, o_ref):
    cond = _silu(c_ref[...])
    o_ref[...] = jnp.dot(cond.astype(BF16), w_ref[...].astype(BF16), preferred_element_type=F32) + b_ref[...]


def _ada_mods(cvec, ada_w, ada_b):
    depth, d, n = ada_w.shape
    rows = cvec.shape[0]
    tn = 1536
    return pl.pallas_call(
        _ada_kernel,
        grid=(depth, n // tn),
        in_specs=[
            pl.BlockSpec((rows, d), lambda l, j: (0, 0)),
            pl.BlockSpec((None, d, tn), lambda l, j: (l, 0, j)),
            pl.BlockSpec((None, 1, tn), lambda l, j: (l, 0, j)),
        ],
        out_specs=pl.BlockSpec((None, rows, tn), lambda l, j: (l, 0, j)),
        out_shape=jax.ShapeDtypeStruct((depth, rows, n), F32),
        compiler_params=_params("parallel", "parallel"),
        name="ada_mods",
    )(cvec, ada_w, ada_b.reshape(depth, 1, n))


def _row_vec_spec(d, row_fn):
    return pl.BlockSpec((None, 1, d), lambda i, *_: (row_fn(i), 0, 0))


def _qkv_kernel(x_ref, gain_ref, sh_ref, sc_ref, w_ref, qg_ref, kg_ref, cos_ref, sin_ref, bd_ref,
                q_ref, k_ref, v_ref, *, rotate):
    bd2 = bd_ref[...]
    qg = qg_ref[...]
    kg = kg_ref[...]
    nq = ATTN_HEADS * HEAD_DIM
    nk = ATTN_KV_HEADS * HEAD_DIM
    width = 2 * LANES

    def head_norm(xg, g):
        x2 = xg * xg
        hi = x2.astype(BF16)
        lo = (x2 - hi.astype(F32)).astype(BF16)
        ssq = jnp.dot(jnp.concatenate([hi, lo], axis=1), bd2, preferred_element_type=F32)
        return (xg * lax.rsqrt(ssq * (1.0 / HEAD_DIM) + NORM_EPS)) * g

    for r0 in range(0, x_ref.shape[0], ROW_SUB):
        rows = slice(r0, min(r0 + ROW_SUB, x_ref.shape[0]))
        h = _prenorm(x_ref[rows, :], gain_ref[...], sh_ref[...], sc_ref[...]).astype(BF16)

        def project(c0):
            return jnp.dot(h, w_ref[:, c0:c0 + width], preferred_element_type=F32)

        nxt = project(0)
        for j in range((nq + nk) // LANES):
            if j % 2 == 0:
                wide = nxt
                nxt = project((j + 2) * LANES)
            acc = wide[:, (j % 2) * LANES:(j % 2 + 1) * LANES]
            is_q = j < nq // LANES
            y = head_norm(acc, qg) * Q_SCALE if is_q else head_norm(acc, kg)
            if rotate:
                y = y * cos_ref[rows, :] + pltpu.roll(y, LANES // 2, 1) * sin_ref[rows, :]
            if is_q:
                q_ref[rows, j * LANES:(j + 1) * LANES] = y.astype(BF16)
            else:
                k_ref[rows, j * LANES - nq:(j + 1) * LANES - nq] = y.astype(BF16)
        v_ref[rows, :] = nxt.astype(BF16)


def _pair_layout():
    lane = np.arange(LANES)
    head_of_pair = (lane // (HEAD_DIM // 2)) % 2
    dim = lane % (HEAD_DIM // 2) + (HEAD_DIM // 2) * (lane // HEAD_DIM)
    return head_of_pair, dim


def _attn_qkv(x, gain, mods, row_fn, w, slot, qg, kg, cos_t, sin_t, bd, *, rotate, tm, tiles_per_seq):
    m, d = x.shape
    nq = ATTN_HEADS * HEAD_DIM
    nk = ATTN_KV_HEADS * HEAD_DIM
    sh, sc = mods
    tab_spec = pl.BlockSpec((tm, LANES), lambda i: (i % tiles_per_seq, 0))
    return pl.pallas_call(
        functools.partial(_qkv_kernel, rotate=rotate),
        grid=(m // tm,),
        in_specs=[
            pl.BlockSpec((tm, d), lambda i: (i, 0)),
            _resident((1, d)),
            _row_vec_spec(d, row_fn),
            _row_vec_spec(d, row_fn),
            _stacked(w, slot),
            _resident((1, LANES)),
            _resident((1, LANES)),
            tab_spec,
            tab_spec,
            _resident((2 * LANES, LANES)),
        ],
        out_specs=[
            pl.BlockSpec((tm, nq), lambda i: (i, 0)),
            pl.BlockSpec((tm, nk), lambda i: (i, 0)),
            pl.BlockSpec((tm, nk), lambda i: (i, 0)),
        ],
        out_shape=[
            jax.ShapeDtypeStruct((m, nq), BF16),
            jax.ShapeDtypeStruct((m, nk), BF16),
            jax.ShapeDtypeStruct((m, nk), BF16),
        ],
        compiler_params=_params("parallel"),
        name="attn_qkv_rot" if rotate else "attn_qkv",
    )(x, gain, sh, sc, w, qg, kg, cos_t, sin_t, bd)


ATTN_SHIFT_LIMIT = 30.0
LOG2E = 1.4426950408889634
Q_SCALE = HEAD_DIM ** -0.5 * LOG2E
ATTN_QB = 8


def _attn_heads(bound_ref, sink_ref, q_ref, q_rows, k_parts, v_parts, mask_ref, o_ref, *, use_bound):
    blk = q_rows.stop - q_rows.start
    lane = lax.broadcasted_iota(jnp.int32, (1, LANES), 1)
    low = lane < HEAD_DIM
    pair_bit = (lane // (HEAD_DIM // 2)) % 2
    one = jnp.ones((), BF16)
    zero = jnp.zeros((), BF16)

    def rows(parts, cols):
        pieces = [r[rs, cols] for r, rs in parts]
        return pieces[0] if len(pieces) == 1 else jnp.concatenate(pieces, axis=0)

    def scores(h):
        par = h % 2
        cols = slice((h // 2) * LANES, (h // 2 + 1) * LANES)
        own_qk = pair_bit == par
        k2 = rows(k_parts, cols)
        q_parts = []
        for g in range(ATTN_GROUP):
            hd = h * ATTN_GROUP + g
            qt = q_ref[q_rows, (hd // 2) * LANES:(hd // 2 + 1) * LANES]
            if hd % 2 != par:
                qt = pltpu.roll(qt, HEAD_DIM // 2 if par == 1 else LANES - HEAD_DIM // 2, 1)
            q_parts.append(jnp.where(own_qk, qt, zero))
        q2 = jnp.concatenate(q_parts, axis=0)
        return lax.dot_general(q2, k2, (((1,), (1,)), ((), ())), preferred_element_type=F32)

    def finish(h, s):
        par = h % 2
        cols = slice((h // 2) * LANES, (h // 2 + 1) * LANES)
        own_v = low if par == 0 else jnp.logical_not(low)
        v2 = jnp.where(own_v, rows(v_parts, cols), one)
        heads = [h * ATTN_GROUP + g for g in range(ATTN_GROUP)]
        if mask_ref is not None:
            b = ATTN_BLOCK
            s = jnp.concatenate([s[:, :b] + mask_ref[:, :b], s[:, b:2 * b],
                                 s[:, 2 * b:3 * b] + mask_ref[:, b:], s[:, 3 * b:]], axis=1)
        if use_bound:
            shifts = [jnp.maximum(bound_ref[0], sink_ref[hd]) for hd in heads]
            shift = jnp.concatenate([jnp.full((blk, 1), sh, F32) for sh in shifts], axis=0)
            es = [jnp.exp2(jnp.full((1, LANES), sink_ref[hd] - sh, F32)) for hd, sh in zip(heads, shifts)]
        else:
            sink = jnp.concatenate([jnp.full((blk, 1), sink_ref[hd], F32) for hd in heads], axis=0)
            shift = jnp.maximum(jnp.max(s, axis=1, keepdims=True), sink)
            es_all = jnp.exp2(sink - shift)
            es = [es_all[g * blk:(g + 1) * blk] for g in range(ATTN_GROUP)]
        p = jnp.exp2(s - shift).astype(BF16)
        oa = jnp.dot(p, v2, preferred_element_type=F32)
        res = []
        for g, hd in enumerate(heads):
            a = oa[g * blk:(g + 1) * blk]
            r = pltpu.roll(a, HEAD_DIM, 1)
            res.append(a / (r + es[g]) if hd % 2 == par else r / (a + es[g]))
        for t in range(ATTN_GROUP // 2):
            tile = jnp.where(low, res[2 * t], res[2 * t + 1])
            c0 = (heads[2 * t] // 2) * LANES
            o_ref[q_rows, c0:c0 + LANES] = tile.astype(BF16)

    s_next = scores(0)
    for h in range(ATTN_KV_HEADS):
        s = s_next
        if h + 1 < ATTN_KV_HEADS:
            s_next = scores(h + 1)
        finish(h, s)


def _attn_kernel(bound_ref, sink_ref, q_ref, *refs, local):
    full = slice(None)
    if local:
        kp_ref, kc_ref, kn_ref, vp_ref, vc_ref, vn_ref, kx_ref, vx_ref = refs[:8]
        mask_refs = refs[8:8 + ATTN_QB]
        o_ref = refs[8 + ATTN_QB]
        blocks = [slice(j * ATTN_BLOCK, (j + 1) * ATTN_BLOCK) for j in range(ATTN_QB)]
        k_blocks = [(kp_ref, full)] + [(kc_ref, rs) for rs in blocks] + [(kn_ref, full)]
        v_blocks = [(vp_ref, full)] + [(vc_ref, rs) for rs in blocks] + [(vn_ref, full)]
        work = [(blocks[j], k_blocks[j:j + 3] + [(kx_ref, full)], v_blocks[j:j + 3] + [(vx_ref, full)], mask_refs[j])
                for j in range(ATTN_QB)]
    else:
        kx_ref, vx_ref, o_ref = refs
        work = [(slice(0, q_ref.shape[0]), [(kx_ref, full)], [(vx_ref, full)], None)]
    use_bound = bound_ref[0] <= ATTN_SHIFT_LIMIT * LOG2E

    def run(flag):
        for q_rows, k_parts, v_parts, mask_ref in work:
            _attn_heads(bound_ref, sink_ref, q_ref, q_rows, k_parts, v_parts, mask_ref, o_ref, use_bound=flag)

    @pl.when(use_bound)
    def _():
        run(True)

    @pl.when(jnp.logical_not(use_bound))
    def _():
        run(False)


def _attn_mask_table():
    span = 3 * ATTN_BLOCK
    offs = np.arange(span) - ATTN_BLOCK
    rel = offs[None, :] - np.arange(ATTN_BLOCK)[:, None]
    near = np.abs(rel) <= WINDOW
    tabs = []
    for variant in range(3):
        ok = near.copy()
        if variant == 0:
            ok[:, :ATTN_BLOCK] = False
        if variant == 2:
            ok[:, 2 * ATTN_BLOCK:] = False
        tabs.append(np.where(ok, 0.0, NEG_INF).astype(np.float32))
    return np.stack(tabs)


def _attention(bound, sink, q, k, v, kx, vx, *, batch, seq, ctx_len, local):
    nblk = seq // ATTN_BLOCK
    nq = ATTN_HEADS * HEAD_DIM
    nk = ATTN_KV_HEADS * HEAD_DIM
    qb = ATTN_QB if local else 1
    nstep = nblk // qb
    q_spec = pl.BlockSpec((qb * ATTN_BLOCK, nq), lambda b, i: (b * nstep + i, 0))
    x_spec = pl.BlockSpec((ctx_len, nk), lambda b, i: (b, 0))
    smem = pl.BlockSpec(memory_space=pltpu.SMEM)
    if local:
        local_mask = _attn_mask_table()
        mask = np.concatenate([local_mask[:, :, :ATTN_BLOCK], local_mask[:, :, 2 * ATTN_BLOCK:]], axis=2)
        mask = jnp.asarray(np.tile(mask, (1, ATTN_GROUP, 1)))
        prev = pl.BlockSpec((ATTN_BLOCK, nk), lambda b, i: (b * nblk + jnp.maximum(qb * i - 1, 0), 0))
        cur = pl.BlockSpec((qb * ATTN_BLOCK, nk), lambda b, i: (b * nstep + i, 0))
        nxt = pl.BlockSpec((ATTN_BLOCK, nk), lambda b, i: (b * nblk + jnp.minimum(qb * (i + 1), nblk - 1), 0))

        def mask_spec(j):
            def index(b, i):
                g = qb * i + j
                return (jnp.where(g == 0, 0, jnp.where(g == nblk - 1, 2, 1)), 0, 0)
            return pl.BlockSpec((None,) + mask.shape[1:], index)

        in_specs = ([smem, smem, q_spec, prev, cur, nxt, prev, cur, nxt, x_spec, x_spec]
                    + [mask_spec(j) for j in range(qb)])
        args = (bound, sink, q, k, k, k, v, v, v, kx, vx) + (mask,) * qb
    else:
        in_specs = [smem, smem, q_spec, x_spec, x_spec]
        args = (bound, sink, q, kx, vx)
    return pl.pallas_call(
        functools.partial(_attn_kernel, local=local),
        grid=(batch, nstep),
        in_specs=in_specs,
        out_specs=q_spec,
        out_shape=jax.ShapeDtypeStruct(q.shape, BF16),
        compiler_params=_params("parallel", "parallel"),
        name="attn_local" if local else "attn_ctx",
    )(*args)


def _pool_kernel(xp_ref, x_ref, xn_ref, gain_ref, sh_ref, sc_ref, g_ref, w_ref, ls_ref, inv_ref, o_ref,
                 h_ref, pair_ref, quad_ref, oct_ref, *, tiles_per_seq):
    tm, d = x_ref.shape
    it = pl.program_id(0) % tiles_per_seq
    gain, sh, sc = gain_ref[...], sh_ref[...], sc_ref[...]
    hp = jnp.where(it > 0, _prenorm(xp_ref[...], gain, sh, sc), 0.0)
    hn = jnp.where(it < tiles_per_seq - 1, _prenorm(xn_ref[...], gain, sh, sc), 0.0)
    x = x_ref[...]
    base = POOL_HALO
    h_ref[0:base, :] = hp
    h_ref[base:base + tm, :] = _prenorm(x, gain, sh, sc)
    h_ref[base + tm:base + tm + POOL_HALO, :] = hn
    h_ref[base + tm + POOL_HALO:, :] = jnp.zeros((POOL_PAD, d), F32)
    gate_scale = g_ref[...] * ls_ref[...]
    gd = d // len(POOL_WINDOWS)
    n_pair, n_quad, n_oct = pair_ref.shape[0], quad_ref.shape[0], oct_ref.shape[0]
    for gi, win in enumerate(POOL_WINDOWS):
        cols = slice(gi * gd, (gi + 1) * gd)
        if win >= 4:
            pair_ref[:, cols] = h_ref[0:n_pair, cols] + h_ref[1:n_pair + 1, cols]
        if win >= 8:
            quad_ref[:, cols] = pair_ref[0:n_quad, cols] + pair_ref[2:n_quad + 2, cols]
        if win == 2:
            tot = h_ref[base - 1:base - 1 + tm, cols] + h_ref[base:base + tm, cols]
        elif win == 4:
            tot = pair_ref[base - 2:base - 2 + tm, cols] + pair_ref[base:base + tm, cols]
        elif win == 8:
            tot = quad_ref[base - 4:base - 4 + tm, cols] + quad_ref[base:base + tm, cols]
        else:
            oct_ref[:, cols] = quad_ref[0:n_oct, cols] + quad_ref[4:n_oct + 4, cols]
            tot = oct_ref[0:tm, cols] + oct_ref[base:base + tm, cols]
        pooled = tot * inv_ref[:, gi:gi + 1] - h_ref[base:base + tm, cols]
        y = jnp.dot(pooled.astype(BF16), w_ref[gi], preferred_element_type=F32)
        o_ref[:, cols] = x[:, cols] + gate_scale[:, cols] * y


def _pool_inv_counts(seq):
    t = np.arange(seq)
    cols = [1.0 / (np.minimum(t + w // 2, seq) - np.maximum(t - w // 2, 0)) for w in POOL_WINDOWS]
    return jnp.asarray(np.stack(cols, axis=1).astype(np.float32))


def _pool_mixer(x, gain, mods, row_fn, w, layer_scale, *, tm, seq):
    m, d = x.shape
    sh, sc, g = mods
    tiles_per_seq = seq // tm
    hb = tm // POOL_HALO
    nhalo = m // POOL_HALO
    return pl.pallas_call(
        functools.partial(_pool_kernel, tiles_per_seq=tiles_per_seq),
        grid=(m // tm,),
        in_specs=[
            pl.BlockSpec((POOL_HALO, d), lambda i: (jnp.maximum(i * hb - 1, 0), 0)),
            pl.BlockSpec((tm, d), lambda i: (i, 0)),
            pl.BlockSpec((POOL_HALO, d), lambda i: (jnp.minimum((i + 1) * hb, nhalo - 1), 0)),
            _resident((1, d)),
            _row_vec_spec(d, row_fn),
            _row_vec_spec(d, row_fn),
            _row_vec_spec(d, row_fn),
            _resident(w.shape),
            _resident((1, d)),
            pl.BlockSpec((tm, len(POOL_WINDOWS)), lambda i: (i % tiles_per_seq, 0)),
        ],
        out_specs=pl.BlockSpec((tm, d), lambda i: (i, 0)),
        out_shape=jax.ShapeDtypeStruct((m, d), F32),
        scratch_shapes=[pltpu.VMEM((tm + 2 * POOL_HALO + POOL_PAD, d), F32),
                        pltpu.VMEM((tm + 3 * POOL_HALO, d), F32),
                        pltpu.VMEM((tm + 2 * POOL_HALO, d), F32),
                        pltpu.VMEM((tm + POOL_HALO, d), F32)],
        compiler_params=_params("parallel"),
        name="pool_mixer",
    )(x, x, x, gain, sh, sc, g, w, layer_scale, _pool_inv_counts(seq))


RET_TN = 512
RET_QK_COLS = 2 * RET_HEADS * RET_DK
RET_V_COLS = RET_HEADS * RET_DV
RET_QKV_COLS = RET_QK_COLS + RET_V_COLS


def _retproj_kernel(x_ref, gain_ref, sh_ref, sc_ref, w_ref, cos_ref, sin_ref, o_ref, *, rotate):
    half = RET_DK // 2
    q_cols = RET_HEADS * RET_DK
    for r0 in range(0, x_ref.shape[0], ROW_SUB):
        rows = slice(r0, min(r0 + ROW_SUB, x_ref.shape[0]))
        h = _prenorm(x_ref[rows, :], gain_ref[...], sh_ref[...], sc_ref[...]).astype(BF16)
        for c0 in range(0, RET_QKV_COLS, RET_TN):
            acc = jnp.dot(h, w_ref[:, c0:c0 + RET_TN], preferred_element_type=F32)
            if c0 >= RET_QK_COLS:
                o_ref[rows, c0:c0 + RET_TN] = acc.astype(BF16)
                continue
            scale = 1.0 if c0 < q_cols else RET_DK ** -0.5
            for hd in range(RET_TN // RET_DK):
                x1 = acc[:, hd * RET_DK:hd * RET_DK + half]
                x2 = acc[:, hd * RET_DK + half:(hd + 1) * RET_DK]
                if rotate:
                    cos, sin = cos_ref[rows, :], sin_ref[rows, :]
                    x1, x2 = x1 * cos - x2 * sin, x2 * cos + x1 * sin
                o_ref[rows, c0 + hd * RET_DK:c0 + hd * RET_DK + half] = (x1 * scale).astype(BF16)
                o_ref[rows, c0 + hd * RET_DK + half:c0 + (hd + 1) * RET_DK] = (x2 * scale).astype(BF16)


def _ret_proj(x, gain, mods, row_fn, w_in, slot, cos_t, sin_t, *, rotate, tm, tiles_per_seq):
    m, d = x.shape
    n = RET_QKV_COLS
    sh, sc = mods
    tab_spec = pl.BlockSpec((tm, RET_DK // 2), lambda i: (i % tiles_per_seq, 0))
    return pl.pallas_call(
        functools.partial(_retproj_kernel, rotate=rotate),
        grid=(m // tm,),
        in_specs=[
            pl.BlockSpec((tm, d), lambda i: (i, 0)),
            _resident((1, d)),
            _row_vec_spec(d, row_fn),
            _row_vec_spec(d, row_fn),
            _stacked(w_in, slot, col_block=0, cols=n),
            tab_spec,
            tab_spec,
        ],
        out_specs=pl.BlockSpec((tm, n), lambda i: (i, 0)),
        out_shape=jax.ShapeDtypeStruct((m, n), BF16),
        compiler_params=_params("parallel"),
        name="ret_proj_rot" if rotate else "ret_proj",
    )(x, gain, sh, sc, w_in, cos_t, sin_t)


def _ret_kernel(qf_ref, kf_ref, vf_ref, qb_ref, kb_ref, vb_ref, intra_ref, qdec_ref, kdec_ref, cdec_ref,
                *rest, sub, zero_init):
    if zero_init:
        yf_ref, yb_ref, sout_ref, state_ref = rest
    else:
        s0_ref, yf_ref, yb_ref, sout_ref, state_ref = rest
    c = pl.program_id(2)
    hd = pl.program_id(1)

    @pl.when(c == 0)
    def _():
        state_ref[...] = jnp.zeros_like(state_ref) if zero_init else s0_ref[...]

    n_sub = qf_ref.shape[0] // sub
    refs = ((qf_ref, kf_ref, vf_ref, yf_ref), (qb_ref, kb_ref, vb_ref, yb_ref))
    steps = []
    for j in range(n_sub):
        steps.append((0, slice(j * sub, (j + 1) * sub)))
        steps.append((1, slice((n_sub - 1 - j) * sub, (n_sub - j) * sub)))

    decayed = []
    for d, rows in steps:
        q_ref, k_ref, _, _ = refs[d]
        sc = lax.dot_general(q_ref[rows, :], k_ref[rows, :], (((1,), (1,)), ((), ())), preferred_element_type=F32)
        decayed.append((sc * intra_ref[d]).astype(BF16))

    for (d, rows), sc in zip(steps, decayed):
        q_ref, k_ref, v_ref, y_ref = refs[d]
        q, k, v = q_ref[rows, :], k_ref[rows, :], v_ref[rows, :]
        s = state_ref[d]
        y = (jnp.dot(sc, v, preferred_element_type=F32)
             + jnp.dot(q, s.astype(BF16), preferred_element_type=F32) * qdec_ref[d])
        kd = (k.astype(F32) * kdec_ref[d]).astype(BF16)
        state_ref[d] = s * cdec_ref[d * RET_HEADS + hd] + lax.dot_general(
            kd, v, (((0,), (0,)), ((), ())), preferred_element_type=F32)
        y_ref[rows, :] = y.astype(BF16)

    @pl.when(c == pl.num_programs(2) - 1)
    def _():
        sout_ref[...] = state_ref[...]


def _ret_tables(chunk):
    heads = np.arange(RET_HEADS, dtype=np.float64)
    log_g = [np.log1p(-np.exp2(-5.0 - heads)), np.log1p(-np.exp2(-5.0 - RET_BWD_OFFSET - heads))]
    pos = np.arange(chunk, dtype=np.float64)
    diff = pos[:, None] - pos[None, :]
    intra, qdec, kdec, cdec = [], [], [], []
    for d, lg in enumerate(log_g):
        lg3 = lg[:, None, None]
        if d == 0:
            intra.append(np.where(diff >= 0, np.exp(np.maximum(diff, 0.0)[None] * lg3), 0.0))
            qdec.append(np.exp((pos[None, :] + 1.0) * lg[:, None]))
            kdec.append(np.exp((chunk - 1.0 - pos)[None, :] * lg[:, None]))
        else:
            intra.append(np.where(diff <= 0, np.exp(np.maximum(-diff, 0.0)[None] * lg3), 0.0))
            qdec.append(np.exp((chunk - pos)[None, :] * lg[:, None]))
            kdec.append(np.exp(pos[None, :] * lg[:, None]))
        cdec.append(np.exp(chunk * lg))
    intra = np.stack(intra).astype(np.float32)
    qdec = np.stack(qdec).astype(np.float32)[..., None]
    kdec = np.stack(kdec).astype(np.float32)[..., None]
    cdec = np.stack(cdec).astype(np.float32).reshape(-1)
    return jnp.asarray(intra), jnp.asarray(qdec), jnp.asarray(kdec), jnp.asarray(cdec)


def _retention(proj, s0, *, batch, seq, block):
    sub = min(block, RET_SUB)
    intra, qdec, kdec, cdec = _ret_tables(sub)
    n = seq // block
    kq = RET_HEADS
    kv = RET_QK_COLS // RET_DV

    def fwd(off, width):
        return pl.BlockSpec((block, width), lambda b, h, c: (b * n + c, off + h))

    def bwd(off, width):
        return pl.BlockSpec((block, width), lambda b, h, c: (b * n + n - 1 - c, off + h))

    tab = lambda a: pl.BlockSpec((2, None) + a.shape[2:], lambda b, h, c: (0, h) + (0,) * (a.ndim - 2))
    state_spec = pl.BlockSpec((None, None, 2, RET_DK, RET_DV), lambda b, h, c: (b, h, 0, 0, 0))
    y_shape = jax.ShapeDtypeStruct((batch * seq, RET_V_COLS), BF16)
    return pl.pallas_call(
        functools.partial(_ret_kernel, sub=sub, zero_init=s0 is None),
        grid=(batch, RET_HEADS, n),
        in_specs=[
            fwd(0, RET_DK), fwd(kq, RET_DK), fwd(kv, RET_DV),
            bwd(0, RET_DK), bwd(kq, RET_DK), bwd(kv, RET_DV),
            tab(intra), tab(qdec), tab(kdec),
            pl.BlockSpec(memory_space=pltpu.SMEM),
        ] + ([] if s0 is None else [state_spec]),
        out_specs=[
            pl.BlockSpec((block, RET_DV), lambda b, h, c: (b * n + c, h)),
            pl.BlockSpec((block, RET_DV), lambda b, h, c: (b * n + n - 1 - c, h)),
            state_spec,
        ],
        out_shape=[y_shape, y_shape, jax.ShapeDtypeStruct((batch, RET_HEADS, 2, RET_DK, RET_DV), F32)],
        scratch_shapes=[pltpu.VMEM((2, RET_DK, RET_DV), F32)],
        compiler_params=_params("parallel", "parallel", "arbitrary"),
        name="retention",
    )(*((proj,) * 6 + (intra, qdec, kdec, cdec) + (() if s0 is None else (s0,))))


def _rms_unit(y):
    y = y.astype(F32)
    return y * lax.rsqrt(jnp.mean(y * y, axis=-1, keepdims=True) + NORM_EPS)


def _retout_kernel(x_ref, yf_ref, yb_ref, gain_ref, sh_ref, sc_ref, g_ref, wg_ref, wo_ref, o_ref, a_ref):
    hv = RET_V_COLS
    for r0 in range(0, x_ref.shape[0], RETOUT_SUB):
        rows = slice(r0, min(r0 + RETOUT_SUB, x_ref.shape[0]))
        x = x_ref[rows, :]
        h = _prenorm(x, gain_ref[...], sh_ref[...], sc_ref[...]).astype(BF16)
        for c0 in range(0, hv, RET_DV):
            cols = slice(c0, c0 + RET_DV)
            gf = jnp.dot(h, wg_ref[:, c0:c0 + RET_DV], preferred_element_type=F32)
            gb = jnp.dot(h, wg_ref[:, hv + c0:hv + c0 + RET_DV], preferred_element_type=F32)
            y = _silu(gf) * _rms_unit(yf_ref[rows, cols]) + _silu(gb) * _rms_unit(yb_ref[rows, cols])
            a_ref[rows, cols] = y.astype(BF16)
        out = jnp.dot(a_ref[rows, :], wo_ref[...], preferred_element_type=F32)
        o_ref[rows, :] = x + g_ref[...] * out


def _ret_out(x, yf, yb, gain, mods, row_fn, w_in, w_o, slot, *, tm):
    m, d = x.shape
    hv = RET_V_COLS
    sh, sc, g = mods
    return pl.pallas_call(
        _retout_kernel,
        grid=(m // tm,),
        in_specs=[
            pl.BlockSpec((tm, d), lambda i: (i, 0)),
            pl.BlockSpec((tm, hv), lambda i: (i, 0)),
            pl.BlockSpec((tm, hv), lambda i: (i, 0)),
            _resident((1, d)),
            _row_vec_spec(d, row_fn),
            _row_vec_spec(d, row_fn),
            _row_vec_spec(d, row_fn),
            _stacked(w_in, slot, col_block=1, cols=2 * hv),
            _stacked(w_o, slot),
        ],
        out_specs=pl.BlockSpec((tm, d), lambda i: (i, 0)),
        out_shape=jax.ShapeDtypeStruct((m, d), F32),
        scratch_shapes=[pltpu.VMEM((tm, hv), BF16)],
        compiler_params=_params("parallel"),
        name="ret_out",
    )(x, yf, yb, gain, sh, sc, g, w_in, w_o)


FFN_CHUNK = 256
FFN_SUB = 256


def _ffn_kernel(x_ref, gain_ref, sh_ref, sc_ref, g_ref, wg_ref, wu_ref, wd_ref, *rest, mixer_proj):
    if mixer_proj:
        y_ref, wo_ref, gm_ref, o_ref, a_ref = rest
    else:
        o_ref, a_ref = rest
    sub = ROW_SUB if mixer_proj else FFN_SUB
    for r0 in range(0, x_ref.shape[0], sub):
        rows = slice(r0, min(r0 + sub, x_ref.shape[0]))
        x = x_ref[rows, :]
        if mixer_proj:
            x = x + gm_ref[...] * jnp.dot(y_ref[rows, :], wo_ref[...], preferred_element_type=F32)
        h = _prenorm(x, gain_ref[...], sh_ref[...], sc_ref[...]).astype(BF16)
        for c0 in range(0, wg_ref.shape[1], FFN_CHUNK):
            cols = slice(c0, c0 + FFN_CHUNK)
            gate = jnp.dot(h, wg_ref[:, cols], preferred_element_type=F32)
            up = jnp.dot(h, wu_ref[:, cols], preferred_element_type=F32)
            a_ref[rows, cols] = (_silu(gate) * up).astype(BF16)
        out = jnp.dot(a_ref[rows, :], wd_ref[...], preferred_element_type=F32)
        o_ref[rows, :] = x + g_ref[...] * out


def _ffn(x, gain, mods, row_fn, wg, wu, wd, layer, *, tm, mixer_proj=None):
    m, d = x.shape
    sh, sc, g = mods
    in_specs = [
        pl.BlockSpec((tm, d), lambda i: (i, 0)),
        _resident((1, d)),
        _row_vec_spec(d, row_fn),
        _row_vec_spec(d, row_fn),
        _row_vec_spec(d, row_fn),
        _stacked(wg, layer),
        _stacked(wu, layer),
        _stacked(wd, layer),
    ]
    args = [x, gain, sh, sc, g, wg, wu, wd]
    if mixer_proj is not None:
        y, w_o, slot, gm = mixer_proj
        in_specs += [
            pl.BlockSpec((tm, y.shape[1]), lambda i: (i, 0)),
            _stacked(w_o, slot),
            _row_vec_spec(d, row_fn),
        ]
        args += [y, w_o, gm]
    return pl.pallas_call(
        functools.partial(_ffn_kernel, mixer_proj=mixer_proj is not None),
        grid=(m // tm,),
        in_specs=in_specs,
        out_specs=pl.BlockSpec((tm, d), lambda i: (i, 0)),
        out_shape=jax.ShapeDtypeStruct((m, d), F32),
        scratch_shapes=[pltpu.VMEM((tm, wg.shape[2]), BF16)],
        compiler_params=_params("parallel"),
        name="ffn_proj" if mixer_proj is not None else "ffn",
    )(*args)


def _axial_tables(n_tokens):
    rows = n_tokens // GRID_W
    row = np.repeat(np.arange(rows, dtype=np.float32), GRID_W)
    col = np.tile(np.arange(GRID_W, dtype=np.float32), rows)
    n_freq = HEAD_DIM // 4
    inv = jnp.asarray(ROPE_BASE, F32) ** (-jnp.arange(n_freq, dtype=F32) / n_freq)
    ang = jnp.concatenate([jnp.asarray(row)[:, None] * inv, jnp.asarray(col)[:, None] * inv], axis=-1)
    cos, sin = jnp.cos(ang), jnp.sin(ang)
    return jnp.tile(cos, (1, 4)), jnp.concatenate([-sin, -sin, sin, sin], axis=-1)


def _retention_rot_tables(n_tokens):
    inv = jnp.asarray(ROPE_BASE, F32) ** (-jnp.linspace(0.0, 1.0, RET_DK // 2, dtype=F32))
    ang = jnp.arange(n_tokens, dtype=F32)[:, None] * inv
    return jnp.cos(ang), jnp.sin(ang)


def kernel(x, c, ctx, c_ctx, ada_w, ada_b, norm_mix, norm_ffn, attn_w_qkv, attn_w_o, attn_q_norm, attn_k_norm,
           attn_sink, pool_w, pool_scale, ret_w_in, ret_w_o, ffn_w_gate, ffn_w_up, ffn_w_down):
    batch, seq, d = x.shape
    ctx_len = ctx.shape[1]
    mod_rows = 16
    tm_lat, tm_ctx, tm_pool = 1024, 256, 512
    assert batch < mod_rows and seq % GRID_W == 0
    assert seq % max(tm_lat, RET_BLOCK, ATTN_QB * ATTN_BLOCK) == 0 and ctx_len % tm_ctx == 0

    cvec = jnp.zeros((mod_rows, d), F32).at[:batch].set(c).at[batch].set(c_ctx)
    mods = _ada_mods(cvec, ada_w, ada_b)
    mods = mods.reshape(DEPTH, mod_rows, 6, 1, d)

    x_lat = x.reshape(batch * seq, d)
    x_ctx = ctx.reshape(batch * ctx_len, d)

    lat_row = lambda i: i // (seq // tm_lat)
    ctx_row = lambda i: batch

    attn_cos, attn_sin = _axial_tables(seq)
    ret_cos, ret_sin = _retention_rot_tables(seq)
    head_of_pair, _ = _pair_layout()
    same_head = (head_of_pair[:, None] == head_of_pair[None, :]).astype(np.float32)
    bd = jnp.asarray(np.concatenate([same_head, same_head], axis=0), BF16)
    half = HEAD_DIM // 2

    def pair_gain(g):
        return jnp.broadcast_to(g.reshape(2, 1, half), (2, 2, half)).reshape(1, LANES)

    nqk = (ATTN_HEADS + ATTN_KV_HEADS) * HEAD_DIM
    slots = attn_w_qkv.shape[0]
    w_qk = attn_w_qkv[:, :, :nqk].reshape(slots, d, nqk // LANES, 2, 2, half)
    w_qk = jnp.swapaxes(w_qk, 3, 4).reshape(slots, d, nqk)
    attn_qkv_w = jnp.concatenate([w_qk, attn_w_qkv[:, :, nqk:]], axis=-1).astype(BF16)
    attn_o_w = attn_w_o.astype(BF16)
    ret_in_w = ret_w_in.astype(BF16)
    ret_o_w = ret_w_o.astype(BF16)
    pool_w_bf = pool_w.astype(BF16)
    wg_all, wu_all, wd_all = ffn_w_gate.astype(BF16), ffn_w_up.astype(BF16), ffn_w_down.astype(BF16)

    for i in range(DEPTH):
        kind, slot = i % N_MIXERS, i // N_MIXERS
        need_ctx_out = i < DEPTH - 1
        mod = [mods[i, :, k] for k in range(6)]
        sh_m, sc_m, g_m, sh_f, sc_f, g_f = mod
        gain_m = norm_mix[i].reshape(1, d)
        gain_f = norm_ffn[i].reshape(1, d)
        proj_lat = proj_ctx = None

        if kind == 0:
            qg = pair_gain(attn_q_norm[slot])
            kg = pair_gain(attn_k_norm[slot])
            sink = attn_sink[slot].astype(F32) * LOG2E
            bound = (1.02 * LOG2E * HEAD_DIM ** 0.5 * jnp.max(jnp.abs(attn_q_norm[slot]))
                     * jnp.max(jnp.abs(attn_k_norm[slot]))).astype(F32).reshape(1)
            q_c, k_c, v_c = _attn_qkv(x_ctx, gain_m, (sh_m, sc_m), ctx_row, attn_qkv_w, slot, qg, kg, attn_cos,
                                      attn_sin, bd, rotate=False, tm=tm_ctx, tiles_per_seq=1)
            q_l, k_l, v_l = _attn_qkv(x_lat, gain_m, (sh_m, sc_m), lat_row, attn_qkv_w, slot, qg, kg, attn_cos,
                                      attn_sin, bd, rotate=True, tm=tm_lat, tiles_per_seq=seq // tm_lat)
            a_l = _attention(bound, sink, q_l, k_l, v_l, k_c, v_c, batch=batch, seq=seq, ctx_len=ctx_len, local=True)
            proj_lat = (a_l, attn_o_w, slot, g_m)
            if need_ctx_out:
                a_c = _attention(bound, sink, q_c, None, None, k_c, v_c, batch=batch, seq=ctx_len, ctx_len=ctx_len,
                                 local=False)
                proj_ctx = (a_c, attn_o_w, slot, g_m)
        elif kind == 1:
            w_p = pool_w_bf[slot]
            ls = pool_scale[slot].reshape(1, d)
            pool_row = lambda i: i // (seq // tm_pool)
            x_lat = _pool_mixer(x_lat, gain_m, (sh_m, sc_m, g_m), pool_row, w_p, ls, tm=tm_pool, seq=seq)
            if need_ctx_out:
                x_ctx = _pool_mixer(x_ctx, gain_m, (sh_m, sc_m, g_m), ctx_row, w_p, ls, tm=tm_ctx, seq=ctx_len)
        else:
            p_c = _ret_proj(x_ctx, gain_m, (sh_m, sc_m), ctx_row, ret_in_w, slot, ret_cos, ret_sin,
                            rotate=False, tm=tm_ctx, tiles_per_seq=1)
            yc_f, yc_b, s_c = _retention(p_c, None, batch=batch, seq=ctx_len, block=min(ctx_len, RET_BLOCK))
            p_l = _ret_proj(x_lat, gain_m, (sh_m, sc_m), lat_row, ret_in_w, slot, ret_cos, ret_sin,
                            rotate=True, tm=tm_lat, tiles_per_seq=seq // tm_lat)
            yl_f, yl_b, _ = _retention(p_l, s_c, batch=batch, seq=seq, block=RET_BLOCK)
            x_lat = _ret_out(x_lat, yl_f, yl_b, gain_m, (sh_m, sc_m, g_m), lat_row, ret_in_w, ret_o_w, slot,
                             tm=tm_lat)
            if need_ctx_out:
                x_ctx = _ret_out(x_ctx, yc_f, yc_b, gain_m, (sh_m, sc_m, g_m), ctx_row, ret_in_w, ret_o_w, slot,
                                 tm=tm_ctx)

        x_lat = _ffn(x_lat, gain_f, (sh_f, sc_f, g_f), lat_row, wg_all, wu_all, wd_all, i, tm=tm_lat,
                     mixer_proj=proj_lat)
        if need_ctx_out:
            x_ctx = _ffn(x_ctx, gain_f, (sh_f, sc_f, g_f), ctx_row, wg_all, wu_all, wd_all, i, tm=tm_ctx,
                         mixer_proj=proj_ctx)

    return x_lat.reshape(batch, seq, d)
```

```python
import functools

import jax
import jax.numpy as jnp
import numpy as np
from jax import lax
from jax.experimental import pallas as pl
from jax.experimental.pallas import tpu as pltpu

F32 = jnp.float32
BF16 = jnp.bfloat16

DEPTH = 4
GRID_W = 64
N_MIXERS = 3
ATTN_HEADS = 16
ATTN_KV_HEADS = 4
ATTN_GROUP = ATTN_HEADS // ATTN_KV_HEADS
HEAD_DIM = 64
WINDOW = 128
ATTN_BLOCK = 128
ROPE_BASE = 10000.0
NEG_INF = -1e30
POOL_WINDOWS = (2, 4, 8, 16)
POOL_HALO = 8
POOL_PAD = 16
RET_HEADS = 4
RET_DK = 256
RET_DV = 512
RET_SUB = 256
RET_BLOCK = 2048
RET_BWD_OFFSET = 0.5
NORM_EPS = 1e-6

VMEM_LIMIT_BYTES = 56 * 1024 * 1024
LANES = 128
ROW_SUB = 512
RETOUT_SUB = 256


def _params(*sem):
    return pltpu.CompilerParams(dimension_semantics=sem, vmem_limit_bytes=VMEM_LIMIT_BYTES)


def _silu(x):
    return x * jax.nn.sigmoid(x)


def _prenorm(x, gain, shift, scale):
    y = x * lax.rsqrt(jnp.mean(x * x, axis=-1, keepdims=True) + NORM_EPS)
    return y * (gain * (1.0 + scale)) + shift


def _resident(shape):
    nd = len(shape)
    return pl.BlockSpec(shape, lambda *_: (0,) * nd)


def _stacked(arr, index, col_block=0, cols=None):
    shape = arr.shape[1:] if cols is None else arr.shape[1:-1] + (cols,)
    nd = len(shape)
    return pl.BlockSpec((None,) + shape, lambda *_: (index,) + (0,) * (nd - 1) + (col_block,),
                        pipeline_mode=pl.Buffered(1))


def _ada_kernel(c_ref, w_ref, b_ref, o_ref):
    cond = _silu(c_ref[...])
    o_ref[...] = jnp.dot(cond.astype(BF16), w_ref[...].astype(BF16), preferred_element_type=F32) + b_ref[...]


def _ada_mods(cvec, ada_w, ada_b):
    depth, d, n = ada_w.shape
    rows = cvec.shape[0]
    tn = 1536
    return pl.pallas_call(
        _ada_kernel,
        grid=(depth, n // tn),
        in_specs=[
            pl.BlockSpec((rows, d), lambda l, j: (0, 0)),
            pl.BlockSpec((None, d, tn), lambda l, j: (l, 0, j)),
            pl.BlockSpec((None, 1, tn), lambda l, j: (l, 0, j)),
        ],
        out_specs=pl.BlockSpec((None, rows, tn), lambda l, j: (l, 0, j)),
        out_shape=jax.ShapeDtypeStruct((depth, rows, n), F32),
        compiler_params=_params("parallel", "parallel"),
        name="ada_mods",
    )(cvec, ada_w, ada_b.reshape(depth, 1, n))


def _row_vec_spec(d, row_fn):
    return pl.BlockSpec((None, 1, d), lambda i, *_: (row_fn(i), 0, 0))


def _qkv_kernel(x_ref, gain_ref, sh_ref, sc_ref, w_ref, qg_ref, kg_ref, cos_ref, sin_ref, bd_ref,
                q_ref, k_ref, v_ref, *, rotate):
    bd2 = bd_ref[...]
    qg = qg_ref[...]
    kg = kg_ref[...]
    nq = ATTN_HEADS * HEAD_DIM
    nk = ATTN_KV_HEADS * HEAD_DIM
    width = 2 * LANES

    def head_norm(xg, g):
        x2 = xg * xg
        hi = x2.astype(BF16)
        lo = (x2 - hi.astype(F32)).astype(BF16)
        ssq = jnp.dot(jnp.concatenate([hi, lo], axis=1), bd2, preferred_element_type=F32)
        return (xg * lax.rsqrt(ssq * (1.0 / HEAD_DIM) + NORM_EPS)) * g

    for r0 in range(0, x_ref.shape[0], ROW_SUB):
        rows = slice(r0, min(r0 + ROW_SUB, x_ref.shape[0]))
        h = _prenorm(x_ref[rows, :], gain_ref[...], sh_ref[...], sc_ref[...]).astype(BF16)

        def project(c0):
            return jnp.dot(h, w_ref[:, c0:c0 + width], preferred_element_type=F32)

        nxt = project(0)
        for j in range((nq + nk) // LANES):
            if j % 2 == 0:
                wide = nxt
                nxt = project((j + 2) * LANES)
            acc = wide[:, (j % 2) * LANES:(j % 2 + 1) * LANES]
            is_q = j < nq // LANES
            y = head_norm(acc, qg) * Q_SCALE if is_q else head_norm(acc, kg)
            if rotate:
                y = y * cos_ref[rows, :] + pltpu.roll(y, LANES // 2, 1) * sin_ref[rows, :]
            if is_q:
                q_ref[rows, j * LANES:(j + 1) * LANES] = y.astype(BF16)
            else:
                k_ref[rows, j * LANES - nq:(j + 1) * LANES - nq] = y.astype(BF16)
        v_ref[rows, :] = nxt.astype(BF16)


def _pair_layout():
    lane = np.arange(LANES)
    head_of_pair = (lane // (HEAD_DIM // 2)) % 2
    dim = lane % (HEAD_DIM // 2) + (HEAD_DIM // 2) * (lane // HEAD_DIM)
    return head_of_pair, dim


def _attn_qkv(x, gain, mods, row_fn, w, slot, qg, kg, cos_t, sin_t, bd, *, rotate, tm, tiles_per_seq):
    m, d = x.shape
    nq = ATTN_HEADS * HEAD_DIM
    nk = ATTN_KV_HEADS * HEAD_DIM
    sh, sc = mods
    tab_spec = pl.BlockSpec((tm, LANES), lambda i: (i % tiles_per_seq, 0))
    return pl.pallas_call(
        functools.partial(_qkv_kernel, rotate=rotate),
        grid=(m // tm,),
        in_specs=[
            pl.BlockSpec((tm, d), lambda i: (i, 0)),
            _resident((1, d)),
            _row_vec_spec(d, row_fn),
            _row_vec_spec(d, row_fn),
            _stacked(w, slot),
            _resident((1, LANES)),
            _resident((1, LANES)),
            tab_spec,
            tab_spec,
            _resident((2 * LANES, LANES)),
        ],
        out_specs=[
            pl.BlockSpec((tm, nq), lambda i: (i, 0)),
            pl.BlockSpec((tm, nk), lambda i: (i, 0)),
            pl.BlockSpec((tm, nk), lambda i: (i, 0)),
        ],
        out_shape=[
            jax.ShapeDtypeStruct((m, nq), BF16),
            jax.ShapeDtypeStruct((m, nk), BF16),
            jax.ShapeDtypeStruct((m, nk), BF16),
        ],
        compiler_params=_params("parallel"),
        name="attn_qkv_rot" if rotate else "attn_qkv",
    )(x, gain, sh, sc, w, qg, kg, cos_t, sin_t, bd)


ATTN_SHIFT_LIMIT = 30.0
LOG2E = 1.4426950408889634
Q_SCALE = HEAD_DIM ** -0.5 * LOG2E
ATTN_QB = 4


def _attn_heads(bound_ref, sink_ref, q_ref, q_rows, k_parts, v_parts, mask_ref, o_ref, *, use_bound):
    blk = q_rows.stop - q_rows.start
    lane = lax.broadcasted_iota(jnp.int32, (1, LANES), 1)
    low = lane < HEAD_DIM
    pair_bit = (lane // (HEAD_DIM // 2)) % 2
    one = jnp.ones((), BF16)
    zero = jnp.zeros((), BF16)

    def rows(parts, cols):
        pieces = [r[rs, cols] for r, rs in parts]
        return pieces[0] if len(pieces) == 1 else jnp.concatenate(pieces, axis=0)

    def scores(h):
        par = h % 2
        cols = slice((h // 2) * LANES, (h // 2 + 1) * LANES)
        own_qk = pair_bit == par
        k2 = rows(k_parts, cols)
        q_parts = []
        for g in range(ATTN_GROUP):
            hd = h * ATTN_GROUP + g
            qt = q_ref[q_rows, (hd // 2) * LANES:(hd // 2 + 1) * LANES]
            if hd % 2 != par:
                qt = pltpu.roll(qt, HEAD_DIM // 2 if par == 1 else LANES - HEAD_DIM // 2, 1)
            q_parts.append(jnp.where(own_qk, qt, zero))
        q2 = jnp.concatenate(q_parts, axis=0)
        return lax.dot_general(q2, k2, (((1,), (1,)), ((), ())), preferred_element_type=F32)

    def finish(h, s):
        par = h % 2
        cols = slice((h // 2) * LANES, (h // 2 + 1) * LANES)
        own_v = low if par == 0 else jnp.logical_not(low)
        v2 = jnp.where(own_v, rows(v_parts, cols), one)
        heads = [h * ATTN_GROUP + g for g in range(ATTN_GROUP)]
        if mask_ref is not None:
            b = ATTN_BLOCK
            s = jnp.concatenate([s[:, :b] + mask_ref[:, :b], s[:, b:2 * b],
                                 s[:, 2 * b:3 * b] + mask_ref[:, b:], s[:, 3 * b:]], axis=1)
        if use_bound:
            shifts = [jnp.maximum(bound_ref[0], sink_ref[hd]) for hd in heads]
            shift = jnp.concatenate([jnp.full((blk, 1), sh, F32) for sh in shifts], axis=0)
            es = [jnp.exp2(jnp.full((1, LANES), sink_ref[hd] - sh, F32)) for hd, sh in zip(heads, shifts)]
        else:
            sink = jnp.concatenate([jnp.full((blk, 1), sink_ref[hd], F32) for hd in heads], axis=0)
            shift = jnp.maximum(jnp.max(s, axis=1, keepdims=True), sink)
            es_all = jnp.exp2(sink - shift)
            es = [es_all[g * blk:(g + 1) * blk] for g in range(ATTN_GROUP)]
        p = jnp.exp2(s - shift).astype(BF16)
        oa = jnp.dot(p, v2, preferred_element_type=F32)
        res = []
        for g, hd in enumerate(heads):
            a = oa[g * blk:(g + 1) * blk]
            r = pltpu.roll(a, HEAD_DIM, 1)
            res.append(a / (r + es[g]) if hd % 2 == par else r / (a + es[g]))
        for t in range(ATTN_GROUP // 2):
            tile = jnp.where(low, res[2 * t], res[2 * t + 1])
            c0 = (heads[2 * t] // 2) * LANES
            o_ref[q_rows, c0:c0 + LANES] = tile.astype(BF16)

    s_next = scores(0)
    for h in range(ATTN_KV_HEADS):
        s = s_next
        if h + 1 < ATTN_KV_HEADS:
            s_next = scores(h + 1)
        finish(h, s)


def _attn_kernel(bound_ref, sink_ref, q_ref, *refs, local):
    full = slice(None)
    if local:
        kp_ref, kc_ref, kn_ref, vp_ref, vc_ref, vn_ref, kx_ref, vx_ref = refs[:8]
        mask_refs = refs[8:8 + ATTN_QB]
        o_ref = refs[8 + ATTN_QB]
        blocks = [slice(j * ATTN_BLOCK, (j + 1) * ATTN_BLOCK) for j in range(ATTN_QB)]
        k_blocks = [(kp_ref, full)] + [(kc_ref, rs) for rs in blocks] + [(kn_ref, full)]
        v_blocks = [(vp_ref, full)] + [(vc_ref, rs) for rs in blocks] + [(vn_ref, full)]
        work = [(blocks[j], k_blocks[j:j + 3] + [(kx_ref, full)], v_blocks[j:j + 3] + [(vx_ref, full)], mask_refs[j])
                for j in range(ATTN_QB)]
    else:
        kx_ref, vx_ref, o_ref = refs
        work = [(slice(0, q_ref.shape[0]), [(kx_ref, full)], [(vx_ref, full)], None)]
    use_bound = bound_ref[0] <= ATTN_SHIFT_LIMIT * LOG2E

    def run(flag):
        for q_rows, k_parts, v_parts, mask_ref in work:
            _attn_heads(bound_ref, sink_ref, q_ref, q_rows, k_parts, v_parts, mask_ref, o_ref, use_bound=flag)

    @pl.when(use_bound)
    def _():
        run(True)

    @pl.when(jnp.logical_not(use_bound))
    def _():
        run(False)


def _attn_mask_table():
    span = 3 * ATTN_BLOCK
    offs = np.arange(span) - ATTN_BLOCK
    rel = offs[None, :] - np.arange(ATTN_BLOCK)[:, None]
    near = np.abs(rel) <= WINDOW
    tabs = []
    for variant in range(3):
        ok = near.copy()
        if variant == 0:
            ok[:, :ATTN_BLOCK] = False
        if variant == 2:
            ok[:, 2 * ATTN_BLOCK:] = False
        tabs.append(np.where(ok, 0.0, NEG_INF).astype(np.float32))
    return np.stack(tabs)


def _attention(bound, sink, q, k, v, kx, vx, *, batch, seq, ctx_len, local):
    nblk = seq // ATTN_BLOCK
    nq = ATTN_HEADS * HEAD_DIM
    nk = ATTN_KV_HEADS * HEAD_DIM
    qb = ATTN_QB if local else 1
    nstep = nblk // qb
    q_spec = pl.BlockSpec((qb * ATTN_BLOCK, nq), lambda b, i: (b * nstep + i, 0))
    x_spec = pl.BlockSpec((ctx_len, nk), lambda b, i: (b, 0))
    smem = pl.BlockSpec(memory_space=pltpu.SMEM)
    if local:
        local_mask = _attn_mask_table()
        mask = np.concatenate([local_mask[:, :, :ATTN_BLOCK], local_mask[:, :, 2 * ATTN_BLOCK:]], axis=2)
        mask = jnp.asarray(np.tile(mask, (1, ATTN_GROUP, 1)))
        prev = pl.BlockSpec((ATTN_BLOCK, nk), lambda b, i: (b * nblk + jnp.maximum(qb * i - 1, 0), 0))
        cur = pl.BlockSpec((qb * ATTN_BLOCK, nk), lambda b, i: (b * nstep + i, 0))
        nxt = pl.BlockSpec((ATTN_BLOCK, nk), lambda b, i: (b * nblk + jnp.minimum(qb * (i + 1), nblk - 1), 0))

        def mask_spec(j):
            def index(b, i):
                g = qb * i + j
                return (jnp.where(g == 0, 0, jnp.where(g == nblk - 1, 2, 1)), 0, 0)
            return pl.BlockSpec((None,) + mask.shape[1:], index)

        in_specs = ([smem, smem, q_spec, prev, cur, nxt, prev, cur, nxt, x_spec, x_spec]
                    + [mask_spec(j) for j in range(qb)])
        args = (bound, sink, q, k, k, k, v, v, v, kx, vx) + (mask,) * qb
    else:
        in_specs = [smem, smem, q_spec, x_spec, x_spec]
        args = (bound, sink, q, kx, vx)
    return pl.pallas_call(
        functools.partial(_attn_kernel, local=local),
        grid=(batch, nstep),
        in_specs=in_specs,
        out_specs=q_spec,
        out_shape=jax.ShapeDtypeStruct(q.shape, BF16),
        compiler_params=_params("parallel", "parallel"),
        name="attn_local" if local else "attn_ctx",
    )(*args)


def _pool_kernel(xp_ref, x_ref, xn_ref, gain_ref, sh_ref, sc_ref, g_ref, w_ref, ls_ref, inv_ref, o_ref,
                 h_ref, pair_ref, quad_ref, oct_ref, *, tiles_per_seq):
    tm, d = x_ref.shape
    it = pl.program_id(0) % tiles_per_seq
    gain, sh, sc = gain_ref[...], sh_ref[...], sc_ref[...]
    hp = jnp.where(it > 0, _prenorm(xp_ref[...], gain, sh, sc), 0.0)
    hn = jnp.where(it < tiles_per_seq - 1, _prenorm(xn_ref[...], gain, sh, sc), 0.0)
    x = x_ref[...]
    base = POOL_HALO
    h_ref[0:base, :] = hp
    h_ref[base:base + tm, :] = _prenorm(x, gain, sh, sc)
    h_ref[base + tm:base + tm + POOL_HALO, :] = hn
    h_ref[base + tm + POOL_HALO:, :] = jnp.zeros((POOL_PAD, d), F32)
    gate_scale = g_ref[...] * ls_ref[...]
    gd = d // len(POOL_WINDOWS)
    n_pair, n_quad, n_oct = pair_ref.shape[0], quad_ref.shape[0], oct_ref.shape[0]
    for gi, win in enumerate(POOL_WINDOWS):
        cols = slice(gi * gd, (gi + 1) * gd)
        if win >= 4:
            pair_ref[:, cols] = h_ref[0:n_pair, cols] + h_ref[1:n_pair + 1, cols]
        if win >= 8:
            quad_ref[:, cols] = pair_ref[0:n_quad, cols] + pair_ref[2:n_quad + 2, cols]
        if win == 2:
            tot = h_ref[base - 1:base - 1 + tm, cols] + h_ref[base:base + tm, cols]
        elif win == 4:
            tot = pair_ref[base - 2:base - 2 + tm, cols] + pair_ref[base:base + tm, cols]
        elif win == 8:
            tot = quad_ref[base - 4:base - 4 + tm, cols] + quad_ref[base:base + tm, cols]
        else:
            oct_ref[:, cols] = quad_ref[0:n_oct, cols] + quad_ref[4:n_oct + 4, cols]
            tot = oct_ref[0:tm, cols] + oct_ref[base:base + tm, cols]
        pooled = tot * inv_ref[:, gi:gi + 1] - h_ref[base:base + tm, cols]
        y = jnp.dot(pooled.astype(BF16), w_ref[gi], preferred_element_type=F32)
        o_ref[:, cols] = x[:, cols] + gate_scale[:, cols] * y


def _pool_inv_counts(seq):
    t = np.arange(seq)
    cols = [1.0 / (np.minimum(t + w // 2, seq) - np.maximum(t - w // 2, 0)) for w in POOL_WINDOWS]
    return jnp.asarray(np.stack(cols, axis=1).astype(np.float32))


def _pool_mixer(x, gain, mods, row_fn, w, layer_scale, *, tm, seq):
    m, d = x.shape
    sh, sc, g = mods
    tiles_per_seq = seq // tm
    hb = tm // POOL_HALO
    nhalo = m // POOL_HALO
    return pl.pallas_call(
        functools.partial(_pool_kernel, tiles_per_seq=tiles_per_seq),
        grid=(m // tm,),
        in_specs=[
            pl.BlockSpec((POOL_HALO, d), lambda i: (jnp.maximum(i * hb - 1, 0), 0)),
            pl.BlockSpec((tm, d), lambda i: (i, 0)),
            pl.BlockSpec((POOL_HALO, d), lambda i: (jnp.minimum((i + 1) * hb, nhalo - 1), 0)),
            _resident((1, d)),
            _row_vec_spec(d, row_fn),
            _row_vec_spec(d, row_fn),
            _row_vec_spec(d, row_fn),
            _resident(w.shape),
            _resident((1, d)),
            pl.BlockSpec((tm, len(POOL_WINDOWS)), lambda i: (i % tiles_per_seq, 0)),
        ],
        out_specs=pl.BlockSpec((tm, d), lambda i: (i, 0)),
        out_shape=jax.ShapeDtypeStruct((m, d), F32),
        scratch_shapes=[pltpu.VMEM((tm + 2 * POOL_HALO + POOL_PAD, d), F32),
                        pltpu.VMEM((tm + 3 * POOL_HALO, d), F32),
                        pltpu.VMEM((tm + 2 * POOL_HALO, d), F32),
                        pltpu.VMEM((tm + POOL_HALO, d), F32)],
        compiler_params=_params("parallel"),
        name="pool_mixer",
    )(x, x, x, gain, sh, sc, g, w, layer_scale, _pool_inv_counts(seq))


RET_TN = 512
RET_QK_COLS = 2 * RET_HEADS * RET_DK
RET_V_COLS = RET_HEADS * RET_DV
RET_QKV_COLS = RET_QK_COLS + RET_V_COLS


def _retproj_kernel(x_ref, gain_ref, sh_ref, sc_ref, w_ref, cos_ref, sin_ref, o_ref, *, rotate):
    half = RET_DK // 2
    q_cols = RET_HEADS * RET_DK
    for r0 in range(0, x_ref.shape[0], ROW_SUB):
        rows = slice(r0, min(r0 + ROW_SUB, x_ref.shape[0]))
        h = _prenorm(x_ref[rows, :], gain_ref[...], sh_ref[...], sc_ref[...]).astype(BF16)
        for c0 in range(0, RET_QKV_COLS, RET_TN):
            acc = jnp.dot(h, w_ref[:, c0:c0 + RET_TN], preferred_element_type=F32)
            if c0 >= RET_QK_COLS:
                o_ref[rows, c0:c0 + RET_TN] = acc.astype(BF16)
                continue
            scale = 1.0 if c0 < q_cols else RET_DK ** -0.5
            for hd in range(RET_TN // RET_DK):
                x1 = acc[:, hd * RET_DK:hd * RET_DK + half]
                x2 = acc[:, hd * RET_DK + half:(hd + 1) * RET_DK]
                if rotate:
                    cos, sin = cos_ref[rows, :], sin_ref[rows, :]
                    x1, x2 = x1 * cos - x2 * sin, x2 * cos + x1 * sin
                o_ref[rows, c0 + hd * RET_DK:c0 + hd * RET_DK + half] = (x1 * scale).astype(BF16)
                o_ref[rows, c0 + hd * RET_DK + half:c0 + (hd + 1) * RET_DK] = (x2 * scale).astype(BF16)


def _ret_proj(x, gain, mods, row_fn, w_in, slot, cos_t, sin_t, *, rotate, tm, tiles_per_seq):
    m, d = x.shape
    n = RET_QKV_COLS
    sh, sc = mods
    tab_spec = pl.BlockSpec((tm, RET_DK // 2), lambda i: (i % tiles_per_seq, 0))
    return pl.pallas_call(
        functools.partial(_retproj_kernel, rotate=rotate),
        grid=(m // tm,),
        in_specs=[
            pl.BlockSpec((tm, d), lambda i: (i, 0)),
            _resident((1, d)),
            _row_vec_spec(d, row_fn),
            _row_vec_spec(d, row_fn),
            _stacked(w_in, slot, col_block=0, cols=n),
            tab_spec,
            tab_spec,
        ],
        out_specs=pl.BlockSpec((tm, n), lambda i: (i, 0)),
        out_shape=jax.ShapeDtypeStruct((m, n), BF16),
        compiler_params=_params("parallel"),
        name="ret_proj_rot" if rotate else "ret_proj",
    )(x, gain, sh, sc, w_in, cos_t, sin_t)


def _ret_kernel(qf_ref, kf_ref, vf_ref, qb_ref, kb_ref, vb_ref, intra_ref, qdec_ref, kdec_ref, cdec_ref,
                *rest, sub, zero_init):
    if zero_init:
        yf_ref, yb_ref, sout_ref, state_ref = rest
    else:
        s0_ref, yf_ref, yb_ref, sout_ref, state_ref = rest
    c = pl.program_id(2)
    hd = pl.program_id(1)

    @pl.when(c == 0)
    def _():
        state_ref[...] = jnp.zeros_like(state_ref) if zero_init else s0_ref[...]

    n_sub = qf_ref.shape[0] // sub
    refs = ((qf_ref, kf_ref, vf_ref, yf_ref), (qb_ref, kb_ref, vb_ref, yb_ref))
    steps = []
    for j in range(n_sub):
        steps.append((0, slice(j * sub, (j + 1) * sub)))
        steps.append((1, slice((n_sub - 1 - j) * sub, (n_sub - j) * sub)))

    decayed = []
    for d, rows in steps:
        q_ref, k_ref, _, _ = refs[d]
        sc = lax.dot_general(q_ref[rows, :], k_ref[rows, :], (((1,), (1,)), ((), ())), preferred_element_type=F32)
        decayed.append((sc * intra_ref[d]).astype(BF16))

    for (d, rows), sc in zip(steps, decayed):
        q_ref, k_ref, v_ref, y_ref = refs[d]
        q, k, v = q_ref[rows, :], k_ref[rows, :], v_ref[rows, :]
        s = state_ref[d]
        y = (jnp.dot(sc, v, preferred_element_type=F32)
             + jnp.dot(q, s.astype(BF16), preferred_element_type=F32) * qdec_ref[d])
        kd = (k.astype(F32) * kdec_ref[d]).astype(BF16)
        state_ref[d] = s * cdec_ref[d * RET_HEADS + hd] + lax.dot_general(
            kd, v, (((0,), (0,)), ((), ())), preferred_element_type=F32)
        y_ref[rows, :] = y.astype(BF16)

    @pl.when(c == pl.num_programs(2) - 1)
    def _():
        sout_ref[...] = state_ref[...]


def _ret_tables(chunk):
    heads = np.arange(RET_HEADS, dtype=np.float64)
    log_g = [np.log1p(-np.exp2(-5.0 - heads)), np.log1p(-np.exp2(-5.0 - RET_BWD_OFFSET - heads))]
    pos = np.arange(chunk, dtype=np.float64)
    diff = pos[:, None] - pos[None, :]
    intra, qdec, kdec, cdec = [], [], [], []
    for d, lg in enumerate(log_g):
        lg3 = lg[:, None, None]
        if d == 0:
            intra.append(np.where(diff >= 0, np.exp(np.maximum(diff, 0.0)[None] * lg3), 0.0))
            qdec.append(np.exp((pos[None, :] + 1.0) * lg[:, None]))
            kdec.append(np.exp((chunk - 1.0 - pos)[None, :] * lg[:, None]))
        else:
            intra.append(np.where(diff <= 0, np.exp(np.maximum(-diff, 0.0)[None] * lg3), 0.0))
            qdec.append(np.exp((chunk - pos)[None, :] * lg[:, None]))
            kdec.append(np.exp(pos[None, :] * lg[:, None]))
        cdec.append(np.exp(chunk * lg))
    intra = np.stack(intra).astype(np.float32)
    qdec = np.stack(qdec).astype(np.float32)[..., None]
    kdec = np.stack(kdec).astype(np.float32)[..., None]
    cdec = np.stack(cdec).astype(np.float32).reshape(-1)
    return jnp.asarray(intra), jnp.asarray(qdec), jnp.asarray(kdec), jnp.asarray(cdec)


def _retention(proj, s0, *, batch, seq, block):
    sub = min(block, RET_SUB)
    intra, qdec, kdec, cdec = _ret_tables(sub)
    n = seq // block
    kq = RET_HEADS
    kv = RET_QK_COLS // RET_DV

    def fwd(off, width):
        return pl.BlockSpec((block, width), lambda b, h, c: (b * n + c, off + h))

    def bwd(off, width):
        return pl.BlockSpec((block, width), lambda b, h, c: (b * n + n - 1 - c, off + h))

    tab = lambda a: pl.BlockSpec((2, None) + a.shape[2:], lambda b, h, c: (0, h) + (0,) * (a.ndim - 2))
    state_spec = pl.BlockSpec((None, None, 2, RET_DK, RET_DV), lambda b, h, c: (b, h, 0, 0, 0))
    y_shape = jax.ShapeDtypeStruct((batch * seq, RET_V_COLS), BF16)
    return pl.pallas_call(
        functools.partial(_ret_kernel, sub=sub, zero_init=s0 is None),
        grid=(batch, RET_HEADS, n),
        in_specs=[
            fwd(0, RET_DK), fwd(kq, RET_DK), fwd(kv, RET_DV),
            bwd(0, RET_DK), bwd(kq, RET_DK), bwd(kv, RET_DV),
            tab(intra), tab(qdec), tab(kdec),
            pl.BlockSpec(memory_space=pltpu.SMEM),
        ] + ([] if s0 is None else [state_spec]),
        out_specs=[
            pl.BlockSpec((block, RET_DV), lambda b, h, c: (b * n + c, h)),
            pl.BlockSpec((block, RET_DV), lambda b, h, c: (b * n + n - 1 - c, h)),
            state_spec,
        ],
        out_shape=[y_shape, y_shape, jax.ShapeDtypeStruct((batch, RET_HEADS, 2, RET_DK, RET_DV), F32)],
        scratch_shapes=[pltpu.VMEM((2, RET_DK, RET_DV), F32)],
        compiler_params=_params("parallel", "parallel", "arbitrary"),
        name="retention",
    )(*((proj,) * 6 + (intra, qdec, kdec, cdec) + (() if s0 is None else (s0,))))


def _rms_unit(y):
    y = y.astype(F32)
    return y * lax.rsqrt(jnp.mean(y * y, axis=-1, keepdims=True) + NORM_EPS)


def _retout_kernel(x_ref, yf_ref, yb_ref, gain_ref, sh_ref, sc_ref, g_ref, wg_ref, wo_ref, o_ref, a_ref):
    hv = RET_V_COLS
    for r0 in range(0, x_ref.shape[0], RETOUT_SUB):
        rows = slice(r0, min(r0 + RETOUT_SUB, x_ref.shape[0]))
        x = x_ref[rows, :]
        h = _prenorm(x, gain_ref[...], sh_ref[...], sc_ref[...]).astype(BF16)
        for c0 in range(0, hv, RET_DV):
            cols = slice(c0, c0 + RET_DV)
            gf = jnp.dot(h, wg_ref[:, c0:c0 + RET_DV], preferred_element_type=F32)
            gb = jnp.dot(h, wg_ref[:, hv + c0:hv + c0 + RET_DV], preferred_element_type=F32)
            y = _silu(gf) * _rms_unit(yf_ref[rows, cols]) + _silu(gb) * _rms_unit(yb_ref[rows, cols])
            a_ref[rows, cols] = y.astype(BF16)
        out = jnp.dot(a_ref[rows, :], wo_ref[...], preferred_element_type=F32)
        o_ref[rows, :] = x + g_ref[...] * out


def _ret_out(x, yf, yb, gain, mods, row_fn, w_in, w_o, slot, *, tm):
    m, d = x.shape
    hv = RET_V_COLS
    sh, sc, g = mods
    return pl.pallas_call(
        _retout_kernel,
        grid=(m // tm,),
        in_specs=[
            pl.BlockSpec((tm, d), lambda i: (i, 0)),
            pl.BlockSpec((tm, hv), lambda i: (i, 0)),
            pl.BlockSpec((tm, hv), lambda i: (i, 0)),
            _resident((1, d)),
            _row_vec_spec(d, row_fn),
            _row_vec_spec(d, row_fn),
            _row_vec_spec(d, row_fn),
            _stacked(w_in, slot, col_block=1, cols=2 * hv),
            _stacked(w_o, slot),
        ],
        out_specs=pl.BlockSpec((tm, d), lambda i: (i, 0)),
        out_shape=jax.ShapeDtypeStruct((m, d), F32),
        scratch_shapes=[pltpu.VMEM((tm, hv), BF16)],
        compiler_params=_params("parallel"),
        name="ret_out",
    )(x, yf, yb, gain, sh, sc, g, w_in, w_o)


FFN_CHUNK = 256
FFN_SUB = 256


def _ffn_kernel(x_ref, gain_ref, sh_ref, sc_ref, g_ref, wg_ref, wu_ref, wd_ref, *rest, mixer_proj):
    if mixer_proj:
        y_ref, wo_ref, gm_ref, o_ref, a_ref = rest
    else:
        o_ref, a_ref = rest
    sub = ROW_SUB if mixer_proj else FFN_SUB
    for r0 in range(0, x_ref.shape[0], sub):
        rows = slice(r0, min(r0 + sub, x_ref.shape[0]))
        x = x_ref[rows, :]
        if mixer_proj:
            x = x + gm_ref[...] * jnp.dot(y_ref[rows, :], wo_ref[...], preferred_element_type=F32)
        h = _prenorm(x, gain_ref[...], sh_ref[...], sc_ref[...]).astype(BF16)
        for c0 in range(0, wg_ref.shape[1], FFN_CHUNK):
            cols = slice(c0, c0 + FFN_CHUNK)
            gate = jnp.dot(h, wg_ref[:, cols], preferred_element_type=F32)
            up = jnp.dot(h, wu_ref[:, cols], preferred_element_type=F32)
            a_ref[rows, cols] = (_silu(gate) * up).astype(BF16)
        out = jnp.dot(a_ref[rows, :], wd_ref[...], preferred_element_type=F32)
        o_ref[rows, :] = x + g_ref[...] * out


def _ffn(x, gain, mods, row_fn, wg, wu, wd, layer, *, tm, mixer_proj=None):
    m, d = x.shape
    sh, sc, g = mods
    in_specs = [
        pl.BlockSpec((tm, d), lambda i: (i, 0)),
        _resident((1, d)),
        _row_vec_spec(d, row_fn),
        _row_vec_spec(d, row_fn),
        _row_vec_spec(d, row_fn),
        _stacked(wg, layer),
        _stacked(wu, layer),
        _stacked(wd, layer),
    ]
    args = [x, gain, sh, sc, g, wg, wu, wd]
    if mixer_proj is not None:
        y, w_o, slot, gm = mixer_proj
        in_specs += [
            pl.BlockSpec((tm, y.shape[1]), lambda i: (i, 0)),
            _stacked(w_o, slot),
            _row_vec_spec(d, row_fn),
        ]
        args += [y, w_o, gm]
    return pl.pallas_call(
        functools.partial(_ffn_kernel, mixer_proj=mixer_proj is not None),
        grid=(m // tm,),
        in_specs=in_specs,
        out_specs=pl.BlockSpec((tm, d), lambda i: (i, 0)),
        out_shape=jax.ShapeDtypeStruct((m, d), F32),
        scratch_shapes=[pltpu.VMEM((tm, wg.shape[2]), BF16)],
        compiler_params=_params("parallel"),
        name="ffn_proj" if mixer_proj is not None else "ffn",
    )(*args)


def _axial_tables(n_tokens):
    rows = n_tokens // GRID_W
    row = np.repeat(np.arange(rows, dtype=np.float32), GRID_W)
    col = np.tile(np.arange(GRID_W, dtype=np.float32), rows)
    n_freq = HEAD_DIM // 4
    inv = jnp.asarray(ROPE_BASE, F32) ** (-jnp.arange(n_freq, dtype=F32) / n_freq)
    ang = jnp.concatenate([jnp.asarray(row)[:, None] * inv, jnp.asarray(col)[:, None] * inv], axis=-1)
    cos, sin = jnp.cos(ang), jnp.sin(ang)
    return jnp.tile(cos, (1, 4)), jnp.concatenate([-sin, -sin, sin, sin], axis=-1)


def _retention_rot_tables(n_tokens):
    inv = jnp.asarray(ROPE_BASE, F32) ** (-jnp.linspace(0.0, 1.0, RET_DK // 2, dtype=F32))
    ang = jnp.arange(n_tokens, dtype=F32)[:, None] * inv
    return jnp.cos(ang), jnp.sin(ang)


def kernel(x, c, ctx, c_ctx, ada_w, ada_b, norm_mix, norm_ffn, attn_w_qkv, attn_w_o, attn_q_norm, attn_k_norm,
           attn_sink, pool_w, pool_scale, ret_w_in, ret_w_o, ffn_w_gate, ffn_w_up, ffn_w_down):
    batch, seq, d = x.shape
    ctx_len = ctx.shape[1]
    mod_rows = 16
    tm_lat, tm_ctx, tm_pool = 1024, 256, 512
    assert batch < mod_rows and seq % GRID_W == 0
    assert seq % max(tm_lat, RET_BLOCK, ATTN_QB * ATTN_BLOCK) == 0 and ctx_len % tm_ctx == 0

    cvec = jnp.zeros((mod_rows, d), F32).at[:batch].set(c).at[batch].set(c_ctx)
    mods = _ada_mods(cvec, ada_w, ada_b)
    mods = mods.reshape(DEPTH, mod_rows, 6, 1, d)

    x_lat = x.reshape(batch * seq, d)
    x_ctx = ctx.reshape(batch * ctx_len, d)

    lat_row = lambda i: i // (seq // tm_lat)
    ctx_row = lambda i: batch

    attn_cos, attn_sin = _axial_tables(seq)
    ret_cos, ret_sin = _retention_rot_tables(seq)
    head_of_pair, _ = _pair_layout()
    same_head = (head_of_pair[:, None] == head_of_pair[None, :]).astype(np.float32)
    bd = jnp.asarray(np.concatenate([same_head, same_head], axis=0), BF16)
    half = HEAD_DIM // 2

    def pair_gain(g):
        return jnp.broadcast_to(g.reshape(2, 1, half), (2, 2, half)).reshape(1, LANES)

    nqk = (ATTN_HEADS + ATTN_KV_HEADS) * HEAD_DIM
    slots = attn_w_qkv.shape[0]
    w_qk = attn_w_qkv[:, :, :nqk].reshape(slots, d, nqk // LANES, 2, 2, half)
    w_qk = jnp.swapaxes(w_qk, 3, 4).reshape(slots, d, nqk)
    attn_qkv_w = jnp.concatenate([w_qk, attn_w_qkv[:, :, nqk:]], axis=-1).astype(BF16)
    attn_o_w = attn_w_o.astype(BF16)
    ret_in_w = ret_w_in.astype(BF16)
    ret_o_w = ret_w_o.astype(BF16)
    pool_w_bf = pool_w.astype(BF16)
    wg_all, wu_all, wd_all = ffn_w_gate.astype(BF16), ffn_w_up.astype(BF16), ffn_w_down.astype(BF16)

    for i in range(DEPTH):
        kind, slot = i % N_MIXERS, i // N_MIXERS
        need_ctx_out = i < DEPTH - 1
        mod = [mods[i, :, k] for k in range(6)]
        sh_m, sc_m, g_m, sh_f, sc_f, g_f = mod
        gain_m = norm_mix[i].reshape(1, d)
        gain_f = norm_ffn[i].reshape(1, d)
        proj_lat = proj_ctx = None

        if kind == 0:
            qg = pair_gain(attn_q_norm[slot])
            kg = pair_gain(attn_k_norm[slot])
            sink = attn_sink[slot].astype(F32) * LOG2E
            bound = (1.02 * LOG2E * HEAD_DIM ** 0.5 * jnp.max(jnp.abs(attn_q_norm[slot]))
                     * jnp.max(jnp.abs(attn_k_norm[slot]))).astype(F32).reshape(1)
            q_c, k_c, v_c = _attn_qkv(x_ctx, gain_m, (sh_m, sc_m), ctx_row, attn_qkv_w, slot, qg, kg, attn_cos,
                                      attn_sin, bd, rotate=False, tm=tm_ctx, tiles_per_seq=1)
            q_l, k_l, v_l = _attn_qkv(x_lat, gain_m, (sh_m, sc_m), lat_row, attn_qkv_w, slot, qg, kg, attn_cos,
                                      attn_sin, bd, rotate=True, tm=tm_lat, tiles_per_seq=seq // tm_lat)
            a_l = _attention(bound, sink, q_l, k_l, v_l, k_c, v_c, batch=batch, seq=seq, ctx_len=ctx_len, local=True)
            proj_lat = (a_l, attn_o_w, slot, g_m)
            if need_ctx_out:
                a_c = _attention(bound, sink, q_c, None, None, k_c, v_c, batch=batch, seq=ctx_len, ctx_len=ctx_len,
                                 local=False)
                proj_ctx = (a_c, attn_o_w, slot, g_m)
        elif kind == 1:
            w_p = pool_w_bf[slot]
            ls = pool_scale[slot].reshape(1, d)
            pool_row = lambda i: i // (seq // tm_pool)
            x_lat = _pool_mixer(x_lat, gain_m, (sh_m, sc_m, g_m), pool_row, w_p, ls, tm=tm_pool, seq=seq)
            if need_ctx_out:
                x_ctx = _pool_mixer(x_ctx, gain_m, (sh_m, sc_m, g_m), ctx_row, w_p, ls, tm=tm_ctx, seq=ctx_len)
        else:
            p_c = _ret_proj(x_ctx, gain_m, (sh_m, sc_m), ctx_row, ret_in_w, slot, ret_cos, ret_sin,
                            rotate=False, tm=tm_ctx, tiles_per_seq=1)
            yc_f, yc_b, s_c = _retention(p_c, None, batch=batch, seq=ctx_len, block=min(ctx_len, RET_BLOCK))
            p_l = _ret_proj(x_lat, gain_m, (sh_m, sc_m), lat_row, ret_in_w, slot, ret_cos, ret_sin,
                            rotate=True, tm=tm_lat, tiles_per_seq=seq // tm_lat)
            yl_f, yl_b, _ = _retention(p_l, s_c, batch=batch, seq=seq, block=RET_BLOCK)
            x_lat = _ret_out(x_lat, yl_f, yl_b, gain_m, (sh_m, sc_m, g_m), lat_row, ret_in_w, ret_o_w, slot,
                             tm=tm_lat)
            if need_ctx_out:
                x_ctx = _ret_out(x_ctx, yc_f, yc_b, gain_m, (sh_m, sc_m, g_m), ctx_row, ret_in_w, ret_o_w, slot,
                                 tm=tm_ctx)

        x_lat = _ffn(x_lat, gain_f, (sh_f, sc_f, g_f), lat_row, wg_all, wu_all, wd_all, i, tm=tm_lat,
                     mixer_proj=proj_lat)
        if need_ctx_out:
            x_ctx = _ffn(x_ctx, gain_f, (sh_f, sc_f, g_f), ctx_row, wg_all, wu_all, wd_all, i, tm=tm_ctx,
                         mixer_proj=proj_ctx)

    return x_lat.reshape(batch, seq, d)
```

```python
import functools

import jax
import jax.numpy as jnp
import numpy as np
from jax import lax
from jax.experimental import pallas as pl
from jax.experimental.pallas import tpu as pltpu

F32 = jnp.float32
BF16 = jnp.bfloat16

DEPTH = 4
GRID_W = 64
N_MIXERS = 3
ATTN_HEADS = 16
ATTN_KV_HEADS = 4
ATTN_GROUP = ATTN_HEADS // ATTN_KV_HEADS
HEAD_DIM = 64
WINDOW = 128
ATTN_BLOCK = 128
ROPE_BASE = 10000.0
NEG_INF = -1e30
POOL_WINDOWS = (2, 4, 8, 16)
POOL_HALO = 8
POOL_PAD = 16
RET_HEADS = 4
RET_DK = 256
RET_DV = 512
RET_SUB = 256
RET_BLOCK = 2048
RET_BWD_OFFSET = 0.5
NORM_EPS = 1e-6

VMEM_LIMIT_BYTES = 56 * 1024 * 1024
LANES = 128
ROW_SUB = 512
RETOUT_SUB = 256


def _params(*sem):
    return pltpu.CompilerParams(dimension_semantics=sem, vmem_limit_bytes=VMEM_LIMIT_BYTES)


def _silu(x):
    return x * jax.nn.sigmoid(x)


def _prenorm(x, gain, shift, scale):
    y = x * lax.rsqrt(jnp.mean(x * x, axis=-1, keepdims=True) + NORM_EPS)
    return y * (gain * (1.0 + scale)) + shift


def _resident(shape):
    nd = len(shape)
    return pl.BlockSpec(shape, lambda *_: (0,) * nd)


def _stacked(arr, index, col_block=0, cols=None):
    shape = arr.shape[1:] if cols is None else arr.shape[1:-1] + (cols,)
    nd = len(shape)
    return pl.BlockSpec((None,) + shape, lambda *_: (index,) + (0,) * (nd - 1) + (col_block,),
                        pipeline_mode=pl.Buffered(1))


def _ada_kernel(c_ref, w_ref, b_ref, o_ref):
    cond = _silu(c_ref[...])
    o_ref[...] = jnp.dot(cond.astype(BF16), w_ref[...].astype(BF16), preferred_element_type=F32) + b_ref[...]


def _ada_mods(cvec, ada_w, ada_b):
    depth, d, n = ada_w.shape
    rows = cvec.shape[0]
    tn = 1536
    return pl.pallas_call(
        _ada_kernel,
        grid=(depth, n // tn),
        in_specs=[
            pl.BlockSpec((rows, d), lambda l, j: (0, 0)),
            pl.BlockSpec((None, d, tn), lambda l, j: (l, 0, j)),
            pl.BlockSpec((None, 1, tn), lambda l, j: (l, 0, j)),
        ],
        out_specs=pl.BlockSpec((None, rows, tn), lambda l, j: (l, 0, j)),
        out_shape=jax.ShapeDtypeStruct((depth, rows, n), F32),
        compiler_params=_params("parallel", "parallel"),
        name="ada_mods",
    )(cvec, ada_w, ada_b.reshape(depth, 1, n))


def _row_vec_spec(d, row_fn):
    return pl.BlockSpec((None, 1, d), lambda i, *_: (row_fn(i), 0, 0))


def _qkv_kernel(x_ref, gain_ref, sh_ref, sc_ref, w_ref, qg_ref, kg_ref, cos_ref, sin_ref, bd_ref,
                q_ref, k_ref, v_ref, *, rotate):
    bd2 = bd_ref[...]
    qg = qg_ref[...]
    kg = kg_ref[...]
    nq = ATTN_HEADS * HEAD_DIM
    nk = ATTN_KV_HEADS * HEAD_DIM
    width = 2 * LANES

    def head_norm(xg, g):
        x2 = xg * xg
        hi = x2.astype(BF16)
        lo = (x2 - hi.astype(F32)).astype(BF16)
        ssq = jnp.dot(jnp.concatenate([hi, lo], axis=1), bd2, preferred_element_type=F32)
        return (xg * lax.rsqrt(ssq * (1.0 / HEAD_DIM) + NORM_EPS)) * g

    for r0 in range(0, x_ref.shape[0], ROW_SUB):
        rows = slice(r0, min(r0 + ROW_SUB, x_ref.shape[0]))
        h = _prenorm(x_ref[rows, :], gain_ref[...], sh_ref[...], sc_ref[...]).astype(BF16)

        def project(c0):
            return jnp.dot(h, w_ref[:, c0:c0 + width], preferred_element_type=F32)

        nxt = project(0)
        for j in range((nq + nk) // LANES):
            if j % 2 == 0:
                wide = nxt
                nxt = project((j + 2) * LANES)
            acc = wide[:, (j % 2) * LANES:(j % 2 + 1) * LANES]
            is_q = j < nq // LANES
            y = head_norm(acc, qg) * Q_SCALE if is_q else head_norm(acc, kg)
            if rotate:
                y = y * cos_ref[rows, :] + pltpu.roll(y, LANES // 2, 1) * sin_ref[rows, :]
            if is_q:
                q_ref[rows, j * LANES:(j + 1) * LANES] = y.astype(BF16)
            else:
                k_ref[rows, j * LANES - nq:(j + 1) * LANES - nq] = y.astype(BF16)
        v_ref[rows, :] = nxt.astype(BF16)


def _pair_layout():
    lane = np.arange(LANES)
    head_of_pair = (lane // (HEAD_DIM // 2)) % 2
    dim = lane % (HEAD_DIM // 2) + (HEAD_DIM // 2) * (lane // HEAD_DIM)
    return head_of_pair, dim


def _attn_qkv(x, gain, mods, row_fn, w, slot, qg, kg, cos_t, sin_t, bd, *, rotate, tm, tiles_per_seq):
    m, d = x.shape
    nq = ATTN_HEADS * HEAD_DIM
    nk = ATTN_KV_HEADS * HEAD_DIM
    sh, sc = mods
    tab_spec = pl.BlockSpec((tm, LANES), lambda i: (i % tiles_per_seq, 0))
    return pl.pallas_call(
        functools.partial(_qkv_kernel, rotate=rotate),
        grid=(m // tm,),
        in_specs=[
            pl.BlockSpec((tm, d), lambda i: (i, 0)),
            _resident((1, d)),
            _row_vec_spec(d, row_fn),
            _row_vec_spec(d, row_fn),
            _stacked(w, slot),
            _resident((1, LANES)),
            _resident((1, LANES)),
            tab_spec,
            tab_spec,
            _resident((2 * LANES, LANES)),
        ],
        out_specs=[
            pl.BlockSpec((tm, nq), lambda i: (i, 0)),
            pl.BlockSpec((tm, nk), lambda i: (i, 0)),
            pl.BlockSpec((tm, nk), lambda i: (i, 0)),
        ],
        out_shape=[
            jax.ShapeDtypeStruct((m, nq), BF16),
            jax.ShapeDtypeStruct((m, nk), BF16),
            jax.ShapeDtypeStruct((m, nk), BF16),
        ],
        compiler_params=_params("parallel"),
        name="attn_qkv_rot" if rotate else "attn_qkv",
    )(x, gain, sh, sc, w, qg, kg, cos_t, sin_t, bd)


ATTN_SHIFT_LIMIT = 30.0
LOG2E = 1.4426950408889634
Q_SCALE = HEAD_DIM ** -0.5 * LOG2E
ATTN_QB = 4


def _attn_heads(bound_ref, sink_ref, q_ref, q_rows, k_parts, v_parts, mask_ref, o_ref, *, use_bound):
    blk = q_rows.stop - q_rows.start
    lane = lax.broadcasted_iota(jnp.int32, (1, LANES), 1)
    low = lane < HEAD_DIM
    pair_bit = (lane // (HEAD_DIM // 2)) % 2
    one = jnp.ones((), BF16)
    zero = jnp.zeros((), BF16)

    def rows(parts, cols):
        pieces = [r[rs, cols] for r, rs in parts]
        return pieces[0] if len(pieces) == 1 else jnp.concatenate(pieces, axis=0)

    def scores(h):
        par = h % 2
        cols = slice((h // 2) * LANES, (h // 2 + 1) * LANES)
        own_qk = pair_bit == par
        k2 = rows(k_parts, cols)
        q_parts = []
        for g in range(ATTN_GROUP):
            hd = h * ATTN_GROUP + g
            qt = q_ref[q_rows, (hd // 2) * LANES:(hd // 2 + 1) * LANES]
            if hd % 2 != par:
                qt = pltpu.roll(qt, HEAD_DIM // 2 if par == 1 else LANES - HEAD_DIM // 2, 1)
            q_parts.append(jnp.where(own_qk, qt, zero))
        q2 = jnp.concatenate(q_parts, axis=0)
        return lax.dot_general(q2, k2, (((1,), (1,)), ((), ())), preferred_element_type=F32)

    def finish(h, s):
        par = h % 2
        cols = slice((h // 2) * LANES, (h // 2 + 1) * LANES)
        own_v = low if par == 0 else jnp.logical_not(low)
        v2 = jnp.where(own_v, rows(v_parts, cols), one)
        heads = [h * ATTN_GROUP + g for g in range(ATTN_GROUP)]
        if mask_ref is not None:
            b = ATTN_BLOCK
            s = jnp.concatenate([s[:, :b] + mask_ref[:, :b], s[:, b:2 * b],
                                 s[:, 2 * b:3 * b] + mask_ref[:, b:], s[:, 3 * b:]], axis=1)
        if use_bound:
            shifts = [jnp.maximum(bound_ref[0], sink_ref[hd]) for hd in heads]
            shift = jnp.concatenate([jnp.full((blk, 1), sh, F32) for sh in shifts], axis=0)
            es = [jnp.exp2(jnp.full((1, LANES), sink_ref[hd] - sh, F32)) for hd, sh in zip(heads, shifts)]
        else:
            sink = jnp.concatenate([jnp.full((blk, 1), sink_ref[hd], F32) for hd in heads], axis=0)
            shift = jnp.maximum(jnp.max(s, axis=1, keepdims=True), sink)
            es_all = jnp.exp2(sink - shift)
            es = [es_all[g * blk:(g + 1) * blk] for g in range(ATTN_GROUP)]
        p = jnp.exp2(s - shift).astype(BF16)
        oa = jnp.dot(p, v2, preferred_element_type=F32)
        res = []
        for g, hd in enumerate(heads):
            a = oa[g * blk:(g + 1) * blk]
            r = pltpu.roll(a, HEAD_DIM, 1)
            res.append(a / (r + es[g]) if hd % 2 == par else r / (a + es[g]))
        for t in range(ATTN_GROUP // 2):
            tile = jnp.where(low, res[2 * t], res[2 * t + 1])
            c0 = (heads[2 * t] // 2) * LANES
            o_ref[q_rows, c0:c0 + LANES] = tile.astype(BF16)

    s_next = scores(0)
    for h in range(ATTN_KV_HEADS):
        s = s_next
        if h + 1 < ATTN_KV_HEADS:
            s_next = scores(h + 1)
        finish(h, s)


def _attn_kernel(bound_ref, sink_ref, q_ref, *refs, local):
    full = slice(None)
    if local:
        kp_ref, kc_ref, kn_ref, vp_ref, vc_ref, vn_ref, kx_ref, vx_ref = refs[:8]
        mask_refs = refs[8:8 + ATTN_QB]
        o_ref = refs[8 + ATTN_QB]
        blocks = [slice(j * ATTN_BLOCK, (j + 1) * ATTN_BLOCK) for j in range(ATTN_QB)]
        k_blocks = [(kp_ref, full)] + [(kc_ref, rs) for rs in blocks] + [(kn_ref, full)]
        v_blocks = [(vp_ref, full)] + [(vc_ref, rs) for rs in blocks] + [(vn_ref, full)]
        work = [(blocks[j], k_blocks[j:j + 3] + [(kx_ref, full)], v_blocks[j:j + 3] + [(vx_ref, full)], mask_refs[j])
                for j in range(ATTN_QB)]
    else:
        kx_ref, vx_ref, o_ref = refs
        work = [(slice(0, q_ref.shape[0]), [(kx_ref, full)], [(vx_ref, full)], None)]
    use_bound = bound_ref[0] <= ATTN_SHIFT_LIMIT * LOG2E

    def run(flag):
        for q_rows, k_parts, v_parts, mask_ref in work:
            _attn_heads(bound_ref, sink_ref, q_ref, q_rows, k_parts, v_parts, mask_ref, o_ref, use_bound=flag)

    @pl.when(use_bound)
    def _():
        run(True)

    @pl.when(jnp.logical_not(use_bound))
    def _():
        run(False)


def _attn_mask_table():
    span = 3 * ATTN_BLOCK
    offs = np.arange(span) - ATTN_BLOCK
    rel = offs[None, :] - np.arange(ATTN_BLOCK)[:, None]
    near = np.abs(rel) <= WINDOW
    tabs = []
    for variant in range(3):
        ok = near.copy()
        if variant == 0:
            ok[:, :ATTN_BLOCK] = False
        if variant == 2:
            ok[:, 2 * ATTN_BLOCK:] = False
        tabs.append(np.where(ok, 0.0, NEG_INF).astype(np.float32))
    return np.stack(tabs)


def _attention(bound, sink, q, k, v, kx, vx, *, batch, seq, ctx_len, local):
    nblk = seq // ATTN_BLOCK
    nq = ATTN_HEADS * HEAD_DIM
    nk = ATTN_KV_HEADS * HEAD_DIM
    qb = ATTN_QB if local else 1
    nstep = nblk // qb
    q_spec = pl.BlockSpec((qb * ATTN_BLOCK, nq), lambda b, i: (b * nstep + i, 0))
    x_spec = pl.BlockSpec((ctx_len, nk), lambda b, i: (b, 0))
    smem = pl.BlockSpec(memory_space=pltpu.SMEM)
    if local:
        local_mask = _attn_mask_table()
        mask = np.concatenate([local_mask[:, :, :ATTN_BLOCK], local_mask[:, :, 2 * ATTN_BLOCK:]], axis=2)
        mask = jnp.asarray(np.tile(mask, (1, ATTN_GROUP, 1)))
        prev = pl.BlockSpec((ATTN_BLOCK, nk), lambda b, i: (b * nblk + jnp.maximum(qb * i - 1, 0), 0))
        cur = pl.BlockSpec((qb * ATTN_BLOCK, nk), lambda b, i: (b * nstep + i, 0))
        nxt = pl.BlockSpec((ATTN_BLOCK, nk), lambda b, i: (b * nblk + jnp.minimum(qb * (i + 1), nblk - 1), 0))

        def mask_spec(j):
            def index(b, i):
                g = qb * i + j
                return (jnp.where(g == 0, 0, jnp.where(g == nblk - 1, 2, 1)), 0, 0)
            return pl.BlockSpec((None,) + mask.shape[1:], index)

        in_specs = ([smem, smem, q_spec, prev, cur, nxt, prev, cur, nxt, x_spec, x_spec]
                    + [mask_spec(j) for j in range(qb)])
        args = (bound, sink, q, k, k, k, v, v, v, kx, vx) + (mask,) * qb
    else:
        in_specs = [smem, smem, q_spec, x_spec, x_spec]
        args = (bound, sink, q, kx, vx)
    return pl.pallas_call(
        functools.partial(_attn_kernel, local=local),
        grid=(batch, nstep),
        in_specs=in_specs,
        out_specs=q_spec,
        out_shape=jax.ShapeDtypeStruct(q.shape, BF16),
        compiler_params=_params("parallel", "parallel"),
        name="attn_local" if local else "attn_ctx",
    )(*args)


def _pool_kernel(xp_ref, x_ref, xn_ref, gain_ref, sh_ref, sc_ref, g_ref, w_ref, ls_ref, inv_ref, o_ref,
                 h_ref, pair_ref, quad_ref, oct_ref, *, tiles_per_seq):
    tm, d = x_ref.shape
    it = pl.program_id(0) % tiles_per_seq
    gain, sh, sc = gain_ref[...], sh_ref[...], sc_ref[...]
    hp = jnp.where(it > 0, _prenorm(xp_ref[...], gain, sh, sc), 0.0)
    hn = jnp.where(it < tiles_per_seq - 1, _prenorm(xn_ref[...], gain, sh, sc), 0.0)
    x = x_ref[...]
    base = POOL_HALO
    h_ref[0:base, :] = hp
    h_ref[base:base + tm, :] = _prenorm(x, gain, sh, sc)
    h_ref[base + tm:base + tm + POOL_HALO, :] = hn
    h_ref[base + tm + POOL_HALO:, :] = jnp.zeros((POOL_PAD, d), F32)
    gate_scale = g_ref[...] * ls_ref[...]
    gd = d // len(POOL_WINDOWS)
    n_pair, n_quad, n_oct = pair_ref.shape[0], quad_ref.shape[0], oct_ref.shape[0]
    for gi, win in enumerate(POOL_WINDOWS):
        cols = slice(gi * gd, (gi + 1) * gd)
        if win >= 4:
            pair_ref[:, cols] = h_ref[0:n_pair, cols] + h_ref[1:n_pair + 1, cols]
        if win >= 8:
            quad_ref[:, cols] = pair_ref[0:n_quad, cols] + pair_ref[2:n_quad + 2, cols]
        if win == 2:
            tot = h_ref[base - 1:base - 1 + tm, cols] + h_ref[base:base + tm, cols]
        elif win == 4:
            tot = pair_ref[base - 2:base - 2 + tm, cols] + pair_ref[base:base + tm, cols]
        elif win == 8:
            tot = quad_ref[base - 4:base - 4 + tm, cols] + quad_ref[base:base + tm, cols]
        else:
            oct_ref[:, cols] = quad_ref[0:n_oct, cols] + quad_ref[4:n_oct + 4, cols]
            tot = oct_ref[0:tm, cols] + oct_ref[base:base + tm, cols]
        pooled = tot * inv_ref[:, gi:gi + 1] - h_ref[base:base + tm, cols]
        y = jnp.dot(pooled.astype(BF16), w_ref[gi], preferred_element_type=F32)
        o_ref[:, cols] = x[:, cols] + gate_scale[:, cols] * y


def _pool_inv_counts(seq):
    t = np.arange(seq)
    cols = [1.0 / (np.minimum(t + w // 2, seq) - np.maximum(t - w // 2, 0)) for w in POOL_WINDOWS]
    return jnp.asarray(np.stack(cols, axis=1).astype(np.float32))


def _pool_mixer(x, gain, mods, row_fn, w, layer_scale, *, tm, seq):
    m, d = x.shape
    sh, sc, g = mods
    tiles_per_seq = seq // tm
    hb = tm // POOL_HALO
    nhalo = m // POOL_HALO
    return pl.pallas_call(
        functools.partial(_pool_kernel, tiles_per_seq=tiles_per_seq),
        grid=(m // tm,),
        in_specs=[
            pl.BlockSpec((POOL_HALO, d), lambda i: (jnp.maximum(i * hb - 1, 0), 0)),
            pl.BlockSpec((tm, d), lambda i: (i, 0)),
            pl.BlockSpec((POOL_HALO, d), lambda i: (jnp.minimum((i + 1) * hb, nhalo - 1), 0)),
            _resident((1, d)),
            _row_vec_spec(d, row_fn),
            _row_vec_spec(d, row_fn),
            _row_vec_spec(d, row_fn),
            _resident(w.shape),
            _resident((1, d)),
            pl.BlockSpec((tm, len(POOL_WINDOWS)), lambda i: (i % tiles_per_seq, 0)),
        ],
        out_specs=pl.BlockSpec((tm, d), lambda i: (i, 0)),
        out_shape=jax.ShapeDtypeStruct((m, d), F32),
        scratch_shapes=[pltpu.VMEM((tm + 2 * POOL_HALO + POOL_PAD, d), F32),
                        pltpu.VMEM((tm + 3 * POOL_HALO, d), F32),
                        pltpu.VMEM((tm + 2 * POOL_HALO, d), F32),
                        pltpu.VMEM((tm + POOL_HALO, d), F32)],
        compiler_params=_params("parallel"),
        name="pool_mixer",
    )(x, x, x, gain, sh, sc, g, w, layer_scale, _pool_inv_counts(seq))


RET_TN = 512
RET_QK_COLS = 2 * RET_HEADS * RET_DK
RET_V_COLS = RET_HEADS * RET_DV
RET_QKV_COLS = RET_QK_COLS + RET_V_COLS


def _retproj_kernel(x_ref, gain_ref, sh_ref, sc_ref, w_ref, cos_ref, sin_ref, o_ref, *, rotate):
    half = RET_DK // 2
    q_cols = RET_HEADS * RET_DK
    for r0 in range(0, x_ref.shape[0], ROW_SUB):
        rows = slice(r0, min(r0 + ROW_SUB, x_ref.shape[0]))
        h = _prenorm(x_ref[rows, :], gain_ref[...], sh_ref[...], sc_ref[...]).astype(BF16)
        for c0 in range(0, RET_QKV_COLS, RET_TN):
            acc = jnp.dot(h, w_ref[:, c0:c0 + RET_TN], preferred_element_type=F32)
            if c0 >= RET_QK_COLS:
                o_ref[rows, c0:c0 + RET_TN] = acc.astype(BF16)
                continue
            scale = 1.0 if c0 < q_cols else RET_DK ** -0.5
            for hd in range(RET_TN // RET_DK):
                x1 = acc[:, hd * RET_DK:hd * RET_DK + half]
                x2 = acc[:, hd * RET_DK + half:(hd + 1) * RET_DK]
                if rotate:
                    cos, sin = cos_ref[rows, :], sin_ref[rows, :]
                    x1, x2 = x1 * cos - x2 * sin, x2 * cos + x1 * sin
                o_ref[rows, c0 + hd * RET_DK:c0 + hd * RET_DK + half] = (x1 * scale).astype(BF16)
                o_ref[rows, c0 + hd * RET_DK + half:c0 + (hd + 1) * RET_DK] = (x2 * scale).astype(BF16)


def _ret_proj(x, gain, mods, row_fn, w_in, slot, cos_t, sin_t, *, rotate, tm, tiles_per_seq):
    m, d = x.shape
    n = RET_QKV_COLS
    sh, sc = mods
    tab_spec = pl.BlockSpec((tm, RET_DK // 2), lambda i: (i % tiles_per_seq, 0))
    return pl.pallas_call(
        functools.partial(_retproj_kernel, rotate=rotate),
        grid=(m // tm,),
        in_specs=[
            pl.BlockSpec((tm, d), lambda i: (i, 0)),
            _resident((1, d)),
            _row_vec_spec(d, row_fn),
            _row_vec_spec(d, row_fn),
            _stacked(w_in, slot, col_block=0, cols=n),
            tab_spec,
            tab_spec,
        ],
        out_specs=pl.BlockSpec((tm, n), lambda i: (i, 0)),
        out_shape=jax.ShapeDtypeStruct((m, n), BF16),
        compiler_params=_params("parallel"),
        name="ret_proj_rot" if rotate else "ret_proj",
    )(x, gain, sh, sc, w_in, cos_t, sin_t)


def _ret_kernel(qf_ref, kf_ref, vf_ref, qb_ref, kb_ref, vb_ref, intra_ref, qdec_ref, kdec_ref, cdec_ref,
                *rest, sub, zero_init):
    if zero_init:
        yf_ref, yb_ref, sout_ref, state_ref = rest
    else:
        s0_ref, yf_ref, yb_ref, sout_ref, state_ref = rest
    c = pl.program_id(2)
    hd = pl.program_id(1)

    @pl.when(c == 0)
    def _():
        state_ref[...] = jnp.zeros_like(state_ref) if zero_init else s0_ref[...]

    n_sub = qf_ref.shape[0] // sub
    refs = ((qf_ref, kf_ref, vf_ref, yf_ref), (qb_ref, kb_ref, vb_ref, yb_ref))
    steps = []
    for j in range(n_sub):
        steps.append((0, slice(j * sub, (j + 1) * sub)))
        steps.append((1, slice((n_sub - 1 - j) * sub, (n_sub - j) * sub)))

    decayed = []
    for d, rows in steps:
        q_ref, k_ref, _, _ = refs[d]
        sc = lax.dot_general(q_ref[rows, :], k_ref[rows, :], (((1,), (1,)), ((), ())), preferred_element_type=F32)
        decayed.append((sc * intra_ref[d]).astype(BF16))

    for (d, rows), sc in zip(steps, decayed):
        q_ref, k_ref, v_ref, y_ref = refs[d]
        q, k, v = q_ref[rows, :], k_ref[rows, :], v_ref[rows, :]
        s = state_ref[d]
        y = (jnp.dot(sc, v, preferred_element_type=F32)
             + jnp.dot(q, s.astype(BF16), preferred_element_type=F32) * qdec_ref[d])
        kd = (k.astype(F32) * kdec_ref[d]).astype(BF16)
        state_ref[d] = s * cdec_ref[d * RET_HEADS + hd] + lax.dot_general(
            kd, v, (((0,), (0,)), ((), ())), preferred_element_type=F32)
        y_ref[rows, :] = y.astype(BF16)

    @pl.when(c == pl.num_programs(2) - 1)
    def _():
        sout_ref[...] = state_ref[...]


def _ret_tables(chunk):
    heads = np.arange(RET_HEADS, dtype=np.float64)
    log_g = [np.log1p(-np.exp2(-5.0 - heads)), np.log1p(-np.exp2(-5.0 - RET_BWD_OFFSET - heads))]
    pos = np.arange(chunk, dtype=np.float64)
    diff = pos[:, None] - pos[None, :]
    intra, qdec, kdec, cdec = [], [], [], []
    for d, lg in enumerate(log_g):
        lg3 = lg[:, None, None]
        if d == 0:
            intra.append(np.where(diff >= 0, np.exp(np.maximum(diff, 0.0)[None] * lg3), 0.0))
            qdec.append(np.exp((pos[None, :] + 1.0) * lg[:, None]))
            kdec.append(np.exp((chunk - 1.0 - pos)[None, :] * lg[:, None]))
        else:
            intra.append(np.where(diff <= 0, np.exp(np.maximum(-diff, 0.0)[None] * lg3), 0.0))
            qdec.append(np.exp((chunk - pos)[None, :] * lg[:, None]))
            kdec.append(np.exp(pos[None, :] * lg[:, None]))
        cdec.append(np.exp(chunk * lg))
    intra = np.stack(intra).astype(np.float32)
    qdec = np.stack(qdec).astype(np.float32)[..., None]
    kdec = np.stack(kdec).astype(np.float32)[..., None]
    cdec = np.stack(cdec).astype(np.float32).reshape(-1)
    return jnp.asarray(intra), jnp.asarray(qdec), jnp.asarray(kdec), jnp.asarray(cdec)


def _retention(proj, s0, *, batch, seq, block):
    sub = min(block, RET_SUB)
    intra, qdec, kdec, cdec = _ret_tables(sub)
    n = seq // block
    kq = RET_HEADS
    kv = RET_QK_COLS // RET_DV

    def fwd(off, width):
        return pl.BlockSpec((block, width), lambda b, h, c: (b * n + c, off + h))

    def bwd(off, width):
        return pl.BlockSpec((block, width), lambda b, h, c: (b * n + n - 1 - c, off + h))

    tab = lambda a: pl.BlockSpec((2, None) + a.shape[2:], lambda b, h, c: (0, h) + (0,) * (a.ndim - 2))
    state_spec = pl.BlockSpec((None, None, 2, RET_DK, RET_DV), lambda b, h, c: (b, h, 0, 0, 0))
    y_shape = jax.ShapeDtypeStruct((batch * seq, RET_V_COLS), BF16)
    return pl.pallas_call(
        functools.partial(_ret_kernel, sub=sub, zero_init=s0 is None),
        grid=(batch, RET_HEADS, n),
        in_specs=[
            fwd(0, RET_DK), fwd(kq, RET_DK), fwd(kv, RET_DV),
            bwd(0, RET_DK), bwd(kq, RET_DK), bwd(kv, RET_DV),
            tab(intra), tab(qdec), tab(kdec),
            pl.BlockSpec(memory_space=pltpu.SMEM),
        ] + ([] if s0 is None else [state_spec]),
        out_specs=[
            pl.BlockSpec((block, RET_DV), lambda b, h, c: (b * n + c, h)),
            pl.BlockSpec((block, RET_DV), lambda b, h, c: (b * n + n - 1 - c, h)),
            state_spec,
        ],
        out_shape=[y_shape, y_shape, jax.ShapeDtypeStruct((batch, RET_HEADS, 2, RET_DK, RET_DV), F32)],
        scratch_shapes=[pltpu.VMEM((2, RET_DK, RET_DV), F32)],
        compiler_params=_params("parallel", "parallel", "arbitrary"),
        name="retention",
    )(*((proj,) * 6 + (intra, qdec, kdec, cdec) + (() if s0 is None else (s0,))))


def _rms_unit(y):
    y = y.astype(F32)
    return y * lax.rsqrt(jnp.mean(y * y, axis=-1, keepdims=True) + NORM_EPS)


def _retout_kernel(x_ref, yf_ref, yb_ref, gain_ref, sh_ref, sc_ref, g_ref, wg_ref, wo_ref, o_ref, a_ref):
    hv = RET_V_COLS
    for r0 in range(0, x_ref.shape[0], RETOUT_SUB):
        rows = slice(r0, min(r0 + RETOUT_SUB, x_ref.shape[0]))
        x = x_ref[rows, :]
        h = _prenorm(x, gain_ref[...], sh_ref[...], sc_ref[...]).astype(BF16)
        for c0 in range(0, hv, RET_DV):
            cols = slice(c0, c0 + RET_DV)
            gf = jnp.dot(h, wg_ref[:, c0:c0 + RET_DV], preferred_element_type=F32)
            gb = jnp.dot(h, wg_ref[:, hv + c0:hv + c0 + RET_DV], preferred_element_type=F32)
            y = _silu(gf) * _rms_unit(yf_ref[rows, cols]) + _silu(gb) * _rms_unit(yb_ref[rows, cols])
            a_ref[rows, cols] = y.astype(BF16)
        out = jnp.dot(a_ref[rows, :], wo_ref[...], preferred_element_type=F32)
        o_ref[rows, :] = x + g_ref[...] * out


def _ret_out(x, yf, yb, gain, mods, row_fn, w_in, w_o, slot, *, tm):
    m, d = x.shape
    hv = RET_V_COLS
    sh, sc, g = mods
    return pl.pallas_call(
        _retout_kernel,
        grid=(m // tm,),
        in_specs=[
            pl.BlockSpec((tm, d), lambda i: (i, 0)),
            pl.BlockSpec((tm, hv), lambda i: (i, 0)),
            pl.BlockSpec((tm, hv), lambda i: (i, 0)),
            _resident((1, d)),
            _row_vec_spec(d, row_fn),
            _row_vec_spec(d, row_fn),
            _row_vec_spec(d, row_fn),
            _stacked(w_in, slot, col_block=1, cols=2 * hv),
            _stacked(w_o, slot),
        ],
        out_specs=pl.BlockSpec((tm, d), lambda i: (i, 0)),
        out_shape=jax.ShapeDtypeStruct((m, d), F32),
        scratch_shapes=[pltpu.VMEM((tm, hv), BF16)],
        compiler_params=_params("parallel"),
        name="ret_out",
    )(x, yf, yb, gain, sh, sc, g, w_in, w_o)


FFN_CHUNK = 256
FFN_SUB = 256


def _ffn_kernel(x_ref, gain_ref, sh_ref, sc_ref, g_ref, wg_ref, wu_ref, wd_ref, *rest, mixer_proj):
    if mixer_proj:
        y_ref, wo_ref, gm_ref, o_ref, a_ref = rest
    else:
        o_ref, a_ref = rest
    sub = ROW_SUB if mixer_proj else FFN_SUB
    for r0 in range(0, x_ref.shape[0], sub):
        rows = slice(r0, min(r0 + sub, x_ref.shape[0]))
        x = x_ref[rows, :]
        if mixer_proj:
            x = x + gm_ref[...] * jnp.dot(y_ref[rows, :], wo_ref[...], preferred_element_type=F32)
        h = _prenorm(x, gain_ref[...], sh_ref[...], sc_ref[...]).astype(BF16)
        for c0 in range(0, wg_ref.shape[1], FFN_CHUNK):
            cols = slice(c0, c0 + FFN_CHUNK)
            gate = jnp.dot(h, wg_ref[:, cols], preferred_element_type=F32)
            up = jnp.dot(h, wu_ref[:, cols], preferred_element_type=F32)
            a_ref[rows, cols] = (_silu(gate) * up).astype(BF16)
        out = jnp.dot(a_ref[rows, :], wd_ref[...], preferred_element_type=F32)
        o_ref[rows, :] = x + g_ref[...] * out


def _ffn(x, gain, mods, row_fn, wg, wu, wd, layer, *, tm, mixer_proj=None):
    m, d = x.shape
    sh, sc, g = mods
    in_specs = [
        pl.BlockSpec((tm, d), lambda i: (i, 0)),
        _resident((1, d)),
        _row_vec_spec(d, row_fn),
        _row_vec_spec(d, row_fn),
        _row_vec_spec(d, row_fn),
        _stacked(wg, layer),
        _stacked(wu, layer),
        _stacked(wd, layer),
    ]
    args = [x, gain, sh, sc, g, wg, wu, wd]
    if mixer_proj is not None:
        y, w_o, slot, gm = mixer_proj
        in_specs += [
            pl.BlockSpec((tm, y.shape[1]), lambda i: (i, 0)),
            _stacked(w_o, slot),
            _row_vec_spec(d, row_fn),
        ]
        args += [y, w_o, gm]
    return pl.pallas_call(
        functools.partial(_ffn_kernel, mixer_proj=mixer_proj is not None),
        grid=(m // tm,),
        in_specs=in_specs,
        out_specs=pl.BlockSpec((tm, d), lambda i: (i, 0)),
        out_shape=jax.ShapeDtypeStruct((m, d), F32),
        scratch_shapes=[pltpu.VMEM((tm, wg.shape[2]), BF16)],
        compiler_params=_params("parallel"),
        name="ffn_proj" if mixer_proj is not None else "ffn",
    )(*args)


def _pool_ffn_kernel(xp_ref, x_ref, xn_ref, gain_m_ref, sh_m_ref, sc_m_ref, gm_ref, wp_ref, ls_ref, inv_ref,
                     gain_f_ref, sh_f_ref, sc_f_ref, gf_ref, wg_ref, wu_ref, wd_ref, o_ref,
                     h_ref, pair_ref, quad_ref, oct_ref, a_ref, *, tiles_per_seq):
    tm, d = x_ref.shape
    it = pl.program_id(0) % tiles_per_seq
    gain, sh, sc = gain_m_ref[...], sh_m_ref[...], sc_m_ref[...]
    hp = jnp.where(it > 0, _prenorm(xp_ref[...], gain, sh, sc), 0.0)
    hn = jnp.where(it < tiles_per_seq - 1, _prenorm(xn_ref[...], gain, sh, sc), 0.0)
    base = POOL_HALO
    h_ref[0:base, :] = hp
    h_ref[base:base + tm, :] = _prenorm(x_ref[...], gain, sh, sc)
    h_ref[base + tm:base + tm + POOL_HALO, :] = hn
    h_ref[base + tm + POOL_HALO:, :] = jnp.zeros((POOL_PAD, d), F32)
    gate_scale = gm_ref[...] * ls_ref[...]
    gd = d // len(POOL_WINDOWS)
    n_pair, n_quad, n_oct = pair_ref.shape[1], quad_ref.shape[1], oct_ref.shape[1]
    sub = n_oct - POOL_HALO
    for si, r0 in enumerate(range(0, tm, sub)):
        rows = slice(r0, r0 + sub)
        x = x_ref[rows, :]
        parts = []
        for gi, win in enumerate(POOL_WINDOWS):
            cols = slice(gi * gd, (gi + 1) * gd)
            if win >= 4:
                pair_ref[si, :, cols] = h_ref[r0:r0 + n_pair, cols] + h_ref[r0 + 1:r0 + n_pair + 1, cols]
            if win >= 8:
                quad_ref[si, :, cols] = pair_ref[si, 0:n_quad, cols] + pair_ref[si, 2:n_quad + 2, cols]
            if win == 2:
                tot = h_ref[r0 + base - 1:r0 + base - 1 + sub, cols] + h_ref[r0 + base:r0 + base + sub, cols]
            elif win == 4:
                tot = pair_ref[si, base - 2:base - 2 + sub, cols] + pair_ref[si, base:base + sub, cols]
            elif win == 8:
                tot = quad_ref[si, base - 4:base - 4 + sub, cols] + quad_ref[si, base:base + sub, cols]
            else:
                oct_ref[si, :, cols] = quad_ref[si, 0:n_oct, cols] + quad_ref[si, 4:n_oct + 4, cols]
                tot = oct_ref[si, 0:sub, cols] + oct_ref[si, base:base + sub, cols]
            pooled = tot * inv_ref[rows, gi:gi + 1] - h_ref[r0 + base:r0 + base + sub, cols]
            y = jnp.dot(pooled.astype(BF16), wp_ref[gi], preferred_element_type=F32)
            parts.append(x[:, cols] + gate_scale[:, cols] * y)
        x1 = jnp.concatenate(parts, axis=1)
        h = _prenorm(x1, gain_f_ref[...], sh_f_ref[...], sc_f_ref[...]).astype(BF16)
        for c0 in range(0, wg_ref.shape[1], FFN_CHUNK):
            cc = slice(c0, c0 + FFN_CHUNK)
            gate = jnp.dot(h, wg_ref[:, cc], preferred_element_type=F32)
            up = jnp.dot(h, wu_ref[:, cc], preferred_element_type=F32)
            a_ref[rows, cc] = (_silu(gate) * up).astype(BF16)
        out = jnp.dot(a_ref[rows, :], wd_ref[...], preferred_element_type=F32)
        o_ref[rows, :] = x1 + gf_ref[...] * out


def _pool_ffn(x, gain_m, mods_m, w_pool, layer_scale, gain_f, mods_f, row_fn, wg, wu, wd, layer, *, tm, seq):
    m, d = x.shape
    sh_m, sc_m, g_m = mods_m
    sh_f, sc_f, g_f = mods_f
    tiles_per_seq = seq // tm
    hb = tm // POOL_HALO
    nhalo = m // POOL_HALO
    sub = FFN_SUB
    n_sub = tm // sub
    vec = lambda: _row_vec_spec(d, row_fn)
    return pl.pallas_call(
        functools.partial(_pool_ffn_kernel, tiles_per_seq=tiles_per_seq),
        grid=(m // tm,),
        in_specs=[
            pl.BlockSpec((POOL_HALO, d), lambda i: (jnp.maximum(i * hb - 1, 0), 0)),
            pl.BlockSpec((tm, d), lambda i: (i, 0)),
            pl.BlockSpec((POOL_HALO, d), lambda i: (jnp.minimum((i + 1) * hb, nhalo - 1), 0)),
            _resident((1, d)), vec(), vec(), vec(),
            _resident(w_pool.shape),
            _resident((1, d)),
            pl.BlockSpec((tm, len(POOL_WINDOWS)), lambda i: (i % tiles_per_seq, 0)),
            _resident((1, d)), vec(), vec(), vec(),
            _stacked(wg, layer), _stacked(wu, layer), _stacked(wd, layer),
        ],
        out_specs=pl.BlockSpec((tm, d), lambda i: (i, 0)),
        out_shape=jax.ShapeDtypeStruct((m, d), F32),
        scratch_shapes=[pltpu.VMEM((tm + 2 * POOL_HALO + POOL_PAD, d), F32),
                        pltpu.VMEM((n_sub, sub + 3 * POOL_HALO, d), F32),
                        pltpu.VMEM((n_sub, sub + 2 * POOL_HALO, d), F32),
                        pltpu.VMEM((n_sub, sub + POOL_HALO, d), F32),
                        pltpu.VMEM((tm, wg.shape[2]), BF16)],
        compiler_params=_params("parallel"),
        name="pool_ffn",
    )(x, x, x, gain_m, sh_m, sc_m, g_m, w_pool, layer_scale, _pool_inv_counts(seq),
      gain_f, sh_f, sc_f, g_f, wg, wu, wd)


def _axial_tables(n_tokens):
    rows = n_tokens // GRID_W
    row = np.repeat(np.arange(rows, dtype=np.float32), GRID_W)
    col = np.tile(np.arange(GRID_W, dtype=np.float32), rows)
    n_freq = HEAD_DIM // 4
    inv = jnp.asarray(ROPE_BASE, F32) ** (-jnp.arange(n_freq, dtype=F32) / n_freq)
    ang = jnp.concatenate([jnp.asarray(row)[:, None] * inv, jnp.asarray(col)[:, None] * inv], axis=-1)
    cos, sin = jnp.cos(ang), jnp.sin(ang)
    return jnp.tile(cos, (1, 4)), jnp.concatenate([-sin, -sin, sin, sin], axis=-1)


def _retention_rot_tables(n_tokens):
    inv = jnp.asarray(ROPE_BASE, F32) ** (-jnp.linspace(0.0, 1.0, RET_DK // 2, dtype=F32))
    ang = jnp.arange(n_tokens, dtype=F32)[:, None] * inv
    return jnp.cos(ang), jnp.sin(ang)


def kernel(x, c, ctx, c_ctx, ada_w, ada_b, norm_mix, norm_ffn, attn_w_qkv, attn_w_o, attn_q_norm, attn_k_norm,
           attn_sink, pool_w, pool_scale, ret_w_in, ret_w_o, ffn_w_gate, ffn_w_up, ffn_w_down):
    batch, seq, d = x.shape
    ctx_len = ctx.shape[1]
    mod_rows = 16
    tm_lat, tm_ctx, tm_pool = 1024, 256, 512
    assert batch < mod_rows and seq % GRID_W == 0
    assert seq % max(tm_lat, RET_BLOCK, ATTN_QB * ATTN_BLOCK) == 0 and ctx_len % tm_ctx == 0

    cvec = jnp.zeros((mod_rows, d), F32).at[:batch].set(c).at[batch].set(c_ctx)
    mods = _ada_mods(cvec, ada_w, ada_b)
    mods = mods.reshape(DEPTH, mod_rows, 6, 1, d)

    x_lat = x.reshape(batch * seq, d)
    x_ctx = ctx.reshape(batch * ctx_len, d)

    lat_row = lambda i: i // (seq // tm_lat)
    ctx_row = lambda i: batch

    attn_cos, attn_sin = _axial_tables(seq)
    ret_cos, ret_sin = _retention_rot_tables(seq)
    head_of_pair, _ = _pair_layout()
    same_head = (head_of_pair[:, None] == head_of_pair[None, :]).astype(np.float32)
    bd = jnp.asarray(np.concatenate([same_head, same_head], axis=0), BF16)
    half = HEAD_DIM // 2

    def pair_gain(g):
        return jnp.broadcast_to(g.reshape(2, 1, half), (2, 2, half)).reshape(1, LANES)

    nqk = (ATTN_HEADS + ATTN_KV_HEADS) * HEAD_DIM
    slots = attn_w_qkv.shape[0]
    w_qk = attn_w_qkv[:, :, :nqk].reshape(slots, d, nqk // LANES, 2, 2, half)
    w_qk = jnp.swapaxes(w_qk, 3, 4).reshape(slots, d, nqk)
    attn_qkv_w = jnp.concatenate([w_qk, attn_w_qkv[:, :, nqk:]], axis=-1).astype(BF16)
    attn_o_w = attn_w_o.astype(BF16)
    ret_in_w = ret_w_in.astype(BF16)
    ret_o_w = ret_w_o.astype(BF16)
    pool_w_bf = pool_w.astype(BF16)
    wg_all, wu_all, wd_all = ffn_w_gate.astype(BF16), ffn_w_up.astype(BF16), ffn_w_down.astype(BF16)

    for i in range(DEPTH):
        kind, slot = i % N_MIXERS, i // N_MIXERS
        need_ctx_out = i < DEPTH - 1
        mod = [mods[i, :, k] for k in range(6)]
        sh_m, sc_m, g_m, sh_f, sc_f, g_f = mod
        gain_m = norm_mix[i].reshape(1, d)
        gain_f = norm_ffn[i].reshape(1, d)
        proj_lat = proj_ctx = None

        if kind == 0:
            qg = pair_gain(attn_q_norm[slot])
            kg = pair_gain(attn_k_norm[slot])
            sink = attn_sink[slot].astype(F32) * LOG2E
            bound = (1.02 * LOG2E * HEAD_DIM ** 0.5 * jnp.max(jnp.abs(attn_q_norm[slot]))
                     * jnp.max(jnp.abs(attn_k_norm[slot]))).astype(F32).reshape(1)
            q_c, k_c, v_c = _attn_qkv(x_ctx, gain_m, (sh_m, sc_m), ctx_row, attn_qkv_w, slot, qg, kg, attn_cos,
                                      attn_sin, bd, rotate=False, tm=tm_ctx, tiles_per_seq=1)
            q_l, k_l, v_l = _attn_qkv(x_lat, gain_m, (sh_m, sc_m), lat_row, attn_qkv_w, slot, qg, kg, attn_cos,
                                      attn_sin, bd, rotate=True, tm=tm_lat, tiles_per_seq=seq // tm_lat)
            a_l = _attention(bound, sink, q_l, k_l, v_l, k_c, v_c, batch=batch, seq=seq, ctx_len=ctx_len, local=True)
            proj_lat = (a_l, attn_o_w, slot, g_m)
            if need_ctx_out:
                a_c = _attention(bound, sink, q_c, None, None, k_c, v_c, batch=batch, seq=ctx_len, ctx_len=ctx_len,
                                 local=False)
                proj_ctx = (a_c, attn_o_w, slot, g_m)
        elif kind == 1:
            w_p = pool_w_bf[slot]
            ls = pool_scale[slot].reshape(1, d)
            pool_row = lambda i: i // (seq // tm_pool)
            x_lat = _pool_ffn(x_lat, gain_m, (sh_m, sc_m, g_m), w_p, ls, gain_f, (sh_f, sc_f, g_f), pool_row,
                              wg_all, wu_all, wd_all, i, tm=tm_pool, seq=seq)
            if need_ctx_out:
                x_ctx = _pool_mixer(x_ctx, gain_m, (sh_m, sc_m, g_m), ctx_row, w_p, ls, tm=tm_ctx, seq=ctx_len)
        else:
            p_c = _ret_proj(x_ctx, gain_m, (sh_m, sc_m), ctx_row, ret_in_w, slot, ret_cos, ret_sin,
                            rotate=False, tm=tm_ctx, tiles_per_seq=1)
            yc_f, yc_b, s_c = _retention(p_c, None, batch=batch, seq=ctx_len, block=min(ctx_len, RET_BLOCK))
            p_l = _ret_proj(x_lat, gain_m, (sh_m, sc_m), lat_row, ret_in_w, slot, ret_cos, ret_sin,
                            rotate=True, tm=tm_lat, tiles_per_seq=seq // tm_lat)
            yl_f, yl_b, _ = _retention(p_l, s_c, batch=batch, seq=seq, block=RET_BLOCK)
            x_lat = _ret_out(x_lat, yl_f, yl_b, gain_m, (sh_m, sc_m, g_m), lat_row, ret_in_w, ret_o_w, slot,
                             tm=tm_lat)
            if need_ctx_out:
                x_ctx = _ret_out(x_ctx, yc_f, yc_b, gain_m, (sh_m, sc_m, g_m), ctx_row, ret_in_w, ret_o_w, slot,
                                 tm=tm_ctx)

        if kind != 1:
            x_lat = _ffn(x_lat, gain_f, (sh_f, sc_f, g_f), lat_row, wg_all, wu_all, wd_all, i, tm=tm_lat,
                         mixer_proj=proj_lat)
        if need_ctx_out:
            x_ctx = _ffn(x_ctx, gain_f, (sh_f, sc_f, g_f), ctx_row, wg_all, wu_all, wd_all, i, tm=tm_ctx,
                         mixer_proj=proj_ctx)

    return x_lat.reshape(batch, seq, d)
```
